```python
import math
import jax, jax.numpy as jnp
from jax import lax
import numpy as np

D_MODEL = 2048
BATCH = 8
SEQ = 8192
DEPTH = 4

LRU_WIDTH = D_MODEL // 2
LRU_BLOCKS = 16
LRU_BLOCK = LRU_WIDTH // LRU_BLOCKS
LRU_CONV = 4
LRU_C = 8.0
HG_WIDTH = D_MODEL // 2
HG_EXPAND = 128
HG_HEADS = HG_WIDTH // HG_EXPAND
HG_DK = HG_EXPAND
HG_DV = HG_WIDTH // HG_HEADS
HG_CHUNK = 64
EVEN_IN = 2 * LRU_WIDTH + 4 * HG_WIDTH
MIX_WIDTH = LRU_WIDTH + HG_WIDTH
SSD_INNER = 2 * D_MODEL
SSD_HEADDIM = 64
SSD_HEADS = SSD_INNER // SSD_HEADDIM
SSD_GROUPS = 8
SSD_HPG = SSD_HEADS // SSD_GROUPS
SSD_STATE = 128
SSD_CONV = 4
SSD_CHUNK = 64
SSD_CONV_DIM = SSD_INNER + 2 * SSD_GROUPS * SSD_STATE
ODD_IN = SSD_INNER + SSD_CONV_DIM + SSD_HEADS
D_FF = 5632
FFN_CONV = 3
EPS = 1e-6
N_EVEN = (DEPTH + 1) // 2
N_ODD = DEPTH // 2

kernel_name = 'hybrid_rglru_hgrn2_mamba2_convffn'


def rmsnorm(x, w):
    xf = x.astype(jnp.float32)
    y = xf * lax.rsqrt(jnp.mean(xf * xf, axis=-1, keepdims=True) + EPS)
    return (y * w.astype(jnp.float32)).astype(x.dtype)


def causal_dwconv(x, w, b):
    k, c = w.shape
    y = lax.conv_general_dilated(x, w.astype(x.dtype)[:, None, :], window_strides=(1,),
                                 padding=[(k - 1, 0)], dimension_numbers=('NWC', 'WIO', 'NWC'),
                                 feature_group_count=c)
    return y + b.astype(x.dtype)


def to_chunks(t, chunk):
    return t.reshape(t.shape[0], t.shape[1] // chunk, chunk, *t.shape[2:])


def rg_lru(x, w_r, b_r, w_i, b_i, lam):
    bsz, s, w = x.shape
    xf = x.astype(jnp.float32)
    xb = xf.reshape(bsz, s, LRU_BLOCKS, LRU_BLOCK)
    r = jax.nn.sigmoid(jnp.einsum('bshi,hij->bshj', xb, w_r.astype(jnp.float32)).reshape(bsz, s, w) + b_r)
    gi = jax.nn.sigmoid(jnp.einsum('bshi,hij->bshj', xb, w_i.astype(jnp.float32)).reshape(bsz, s, w) + b_i)
    log_a = -LRU_C * r * jax.nn.softplus(-lam.astype(jnp.float32))
    a = jnp.exp(log_a)
    u = jnp.sqrt(-jnp.expm1(2.0 * log_a)) * (gi * xf)

    def combine(lhs, rhs):
        a1, b1 = lhs
        a2, b2 = rhs
        return a1 * a2, a2 * b1 + b2

    _, h = lax.associative_scan(combine, (a, u), axis=1)
    return h


def hgrn2(q, f_raw, v, lb):
    bsz, s, _ = q.shape
    lb = lb.astype(jnp.float32)
    f = lb + (1.0 - lb) * jax.nn.sigmoid(f_raw.astype(jnp.float32))
    k = 1.0 - f
    g = jnp.log(f)
    qs = jax.nn.silu(q.astype(jnp.float32)) * (HG_DK ** -0.5)
    hd = (HG_HEADS, HG_DK)
    qc = to_chunks(qs.reshape(bsz, s, *hd), HG_CHUNK)
    kc = to_chunks(k.reshape(bsz, s, *hd), HG_CHUNK)
    gc = to_chunks(g.reshape(bsz, s, *hd), HG_CHUNK)
    vc = to_chunks(v.astype(jnp.float32).reshape(bsz, s, HG_HEADS, HG_DV), HG_CHUNK)
    cum = jnp.cumsum(gc, axis=2)
    tot = cum[:, :, -1]
    mid = cum[:, :, HG_CHUNK // 2 - 1][:, :, None]
    causal = jnp.tril(jnp.ones((HG_CHUNK, HG_CHUNK), dtype=bool))
    scores = jnp.einsum('bnthd,bnshd->bnhts', qc * jnp.exp(cum - mid), kc * jnp.exp(mid - cum))
    scores = jnp.where(causal, scores, 0.0)
    o_intra = jnp.einsum('bnhts,bnshv->bnthv', scores, vc)

    def step(state, inp):
        q_c, k_c, v_c, cum_c, tot_c = inp
        o = jnp.einsum('bthd,bhdv->bthv', q_c * jnp.exp(cum_c), state)
        k_dec = k_c * jnp.exp(tot_c[:, None] - cum_c)
        new = jnp.exp(tot_c)[..., None] * state + jnp.einsum('bshd,bshv->bhdv', k_dec, v_c)
        return new, o

    state0 = jnp.zeros((bsz, HG_HEADS, HG_DK, HG_DV), jnp.float32)
    xs = tuple(jnp.moveaxis(t, 1, 0) for t in (qc, kc, vc, cum, tot))
    _, o_inter = lax.scan(step, state0, xs)
    o = o_intra + jnp.moveaxis(o_inter, 0, 1)
    return o.reshape(bsz, s, HG_HEADS, HG_DV)


def even_mixer(h, w_in, lru_conv_w, lru_conv_b, lru_w_r, lru_b_r, lru_w_i, lru_b_i, lru_lambda,
               lb, hg_norm_w, w_out):
    bsz, s, _ = h.shape
    proj = h @ w_in
    cuts = [LRU_WIDTH, 2 * LRU_WIDTH, 2 * LRU_WIDTH + HG_WIDTH,
            2 * LRU_WIDTH + 2 * HG_WIDTH, 2 * LRU_WIDTH + 3 * HG_WIDTH]
    xa, ga, qb, fb, ib, gb = jnp.split(proj, cuts, axis=-1)
    ha = rg_lru(causal_dwconv(xa, lru_conv_w, lru_conv_b), lru_w_r, lru_b_r, lru_w_i, lru_b_i, lru_lambda)
    ya = ha * jax.nn.gelu(ga.astype(jnp.float32))
    ob = hgrn2(qb, fb, ib, lb)
    ob = ob * lax.rsqrt(jnp.mean(ob * ob, axis=-1, keepdims=True) + EPS) * hg_norm_w.astype(jnp.float32)
    yb = ob.reshape(bsz, s, HG_WIDTH) * jax.nn.silu(gb.astype(jnp.float32))
    y = jnp.concatenate([ya, yb], axis=-1).astype(h.dtype)
    return y @ w_out


def ssd_scan(x, dt, a_neg, bm, cm):
    bsz, s, _ = x.shape
    xc = to_chunks(x.reshape(bsz, s, SSD_GROUPS, SSD_HPG, SSD_HEADDIM), SSD_CHUNK)
    dtc = to_chunks(dt.reshape(bsz, s, SSD_GROUPS, SSD_HPG), SSD_CHUNK)
    bc = to_chunks(bm, SSD_CHUNK)
    cc = to_chunks(cm, SSD_CHUNK)
    xdt = xc * dtc[..., None]
    cum = jnp.cumsum(dtc * a_neg.reshape(SSD_GROUPS, SSD_HPG), axis=2)
    tot = cum[:, :, -1]
    causal = jnp.tril(jnp.ones((SSD_CHUNK, SSD_CHUNK), dtype=bool))[:, :, None, None]
    seg = cum[:, :, :, None] - cum[:, :, None, :]
    decay = jnp.exp(jnp.where(causal, seg, -jnp.inf))
    cb = jnp.einsum('bntgk,bnsgk->bntsg', cc, bc)
    y_diag = jnp.einsum('bntsgh,bnsghp->bntghp', cb[..., None] * decay, xdt)

    def step(state, inp):
        c_c, b_c, xdt_c, cum_c, tot_c = inp
        y_off = jnp.einsum('btgk,bghpk->btghp', c_c, state) * jnp.exp(cum_c)[..., None]
        dec = jnp.exp(tot_c[:, None] - cum_c)
        new = jnp.exp(tot_c)[..., None, None] * state + jnp.einsum('bsgk,bsghp->bghpk', b_c, xdt_c * dec[..., None])
        return new, y_off

    state0 = jnp.zeros((bsz, SSD_GROUPS, SSD_HPG, SSD_HEADDIM, SSD_STATE), jnp.float32)
    xs = tuple(jnp.moveaxis(t, 1, 0) for t in (cc, bc, xdt, cum, tot))
    _, y_off = lax.scan(step, state0, xs)
    y = y_diag + jnp.moveaxis(y_off, 0, 1)
    return y, xc


def ssd_mixer(h, w_in, conv_w, conv_b, dt_bias, a_log, d_skip, norm_w, w_out):
    bsz, s, _ = h.shape
    proj = h @ w_in
    z, xbc, dt_raw = jnp.split(proj, [SSD_INNER, SSD_INNER + SSD_CONV_DIM], axis=-1)
    xbc = jax.nn.silu(causal_dwconv(xbc, conv_w, conv_b)).astype(jnp.float32)
    xs, bm, cm = jnp.split(xbc, [SSD_INNER, SSD_INNER + SSD_GROUPS * SSD_STATE], axis=-1)
    bm = bm.reshape(bsz, s, SSD_GROUPS, SSD_STATE)
    cm = cm.reshape(bsz, s, SSD_GROUPS, SSD_STATE)
    dt = jax.nn.softplus(dt_raw.astype(jnp.float32) + dt_bias.astype(jnp.float32))
    a_neg = -jnp.exp(a_log.astype(jnp.float32))
    y, xc = ssd_scan(xs, dt, a_neg, bm, cm)
    y = y + xc * d_skip.astype(jnp.float32).reshape(SSD_GROUPS, SSD_HPG)[..., None]
    y = y.reshape(bsz, s, SSD_INNER) * jax.nn.silu(z.astype(jnp.float32))
    y = y.reshape(bsz, s, SSD_GROUPS, SSD_INNER // SSD_GROUPS)
    y = y * lax.rsqrt(jnp.mean(y * y, axis=-1, keepdims=True) + EPS)
    y = y.reshape(bsz, s, SSD_INNER) * norm_w.astype(jnp.float32)
    return y.astype(h.dtype) @ w_out


def conv_ffn(h, w_up, conv_w, conv_b, w_down):
    u = causal_dwconv(h @ w_up, conv_w, conv_b)
    gate, val = jnp.split(u, 2, axis=-1)
    return (jax.nn.silu(gate) * val) @ w_down


def _fwd_setup_inputs(seed: int = 0) -> dict:
    key = jax.random.key(seed)
    ks = iter(jax.random.split(key, 48))

    def nrm(shape, scale):
        return scale * jax.random.normal(next(ks), shape, jnp.float32)

    def gain(shape):
        return 1.0 + 0.02 * jax.random.normal(next(ks), shape, jnp.float32)

    lam_u = jax.random.uniform(next(ks), (N_EVEN, LRU_WIDTH), jnp.float32, minval=0.9, maxval=0.999)
    dt0 = jnp.exp(jax.random.uniform(next(ks), (N_ODD, SSD_HEADS), jnp.float32,
                                     minval=math.log(1e-3), maxval=math.log(1e-1)))
    a0 = jax.random.uniform(next(ks), (N_ODD, SSD_HEADS), jnp.float32, minval=1.0, maxval=16.0)
    return {
        'x': nrm((BATCH, SEQ, D_MODEL), 1.0),
        'norm_mix_w': gain((DEPTH, D_MODEL)),
        'norm_ffn_w': gain((DEPTH, D_MODEL)),
        'norm_final_w': gain((D_MODEL,)),
        'ev_w_in': nrm((N_EVEN, D_MODEL, EVEN_IN), D_MODEL ** -0.5),
        'lru_conv_w': nrm((N_EVEN, LRU_CONV, LRU_WIDTH), LRU_CONV ** -0.5),
        'lru_conv_b': nrm((N_EVEN, LRU_WIDTH), 0.02),
        'lru_w_r': nrm((N_EVEN, LRU_BLOCKS, LRU_BLOCK, LRU_BLOCK), LRU_BLOCK ** -0.5),
        'lru_b_r': nrm((N_EVEN, LRU_WIDTH), 0.02),
        'lru_w_i': nrm((N_EVEN, LRU_BLOCKS, LRU_BLOCK, LRU_BLOCK), LRU_BLOCK ** -0.5),
        'lru_b_i': nrm((N_EVEN, LRU_WIDTH), 0.02),
        'lru_lambda': jnp.log(lam_u) - jnp.log1p(-lam_u),
        'hg_lower_bounds': nrm((N_EVEN, HG_WIDTH), 0.02),
        'hg_norm_w': gain((N_EVEN, HG_DV)),
        'ev_w_out': nrm((N_EVEN, MIX_WIDTH, D_MODEL), MIX_WIDTH ** -0.5),
        'ssd_w_in': nrm((N_ODD, D_MODEL, ODD_IN), D_MODEL ** -0.5),
        'ssd_conv_w': nrm((N_ODD, SSD_CONV, SSD_CONV_DIM), SSD_CONV ** -0.5),
        'ssd_conv_b': nrm((N_ODD, SSD_CONV_DIM), 0.02),
        'ssd_dt_bias': dt0 + jnp.log(-jnp.expm1(-dt0)),
        'ssd_a_log': jnp.log(a0),
        'ssd_d': gain((N_ODD, SSD_HEADS)),
        'ssd_norm_w': gain((N_ODD, SSD_INNER)),
        'ssd_w_out': nrm((N_ODD, SSD_INNER, D_MODEL), SSD_INNER ** -0.5),
        'ffn_w_up': nrm((DEPTH, D_MODEL, 2 * D_FF), D_MODEL ** -0.5),
        'ffn_conv_w': nrm((DEPTH, FFN_CONV, 2 * D_FF), FFN_CONV ** -0.5),
        'ffn_conv_b': nrm((DEPTH, 2 * D_FF), 0.02),
        'ffn_w_down': nrm((DEPTH, D_FF, D_MODEL), D_FF ** -0.5),
    }


def _fwd_reference(x, norm_mix_w, norm_ffn_w, norm_final_w, ev_w_in, lru_conv_w, lru_conv_b, lru_w_r,
              lru_b_r, lru_w_i, lru_b_i, lru_lambda, hg_lower_bounds, hg_norm_w, ev_w_out,
              ssd_w_in, ssd_conv_w, ssd_conv_b, ssd_dt_bias, ssd_a_log, ssd_d, ssd_norm_w, ssd_w_out,
              ffn_w_up, ffn_conv_w, ffn_conv_b, ffn_w_down):
    lbp = jax.nn.softmax(hg_lower_bounds.astype(jnp.float32), axis=0)
    lbs = jnp.cumsum(lbp, axis=0) - lbp[0]
    for l in range(DEPTH):
        h = rmsnorm(x, norm_mix_w[l])
        if l % 2 == 0:
            e = l // 2
            mix = even_mixer(h, ev_w_in[e], lru_conv_w[e], lru_conv_b[e], lru_w_r[e], lru_b_r[e],
                             lru_w_i[e], lru_b_i[e], lru_lambda[e], lbs[e], hg_norm_w[e], ev_w_out[e])
        else:
            o = l // 2
            mix = ssd_mixer(h, ssd_w_in[o], ssd_conv_w[o], ssd_conv_b[o], ssd_dt_bias[o], ssd_a_log[o],
                            ssd_d[o], ssd_norm_w[o], ssd_w_out[o])
        x = x + mix
        x = x + conv_ffn(rmsnorm(x, norm_ffn_w[l]), ffn_w_up[l], ffn_conv_w[l], ffn_conv_b[l], ffn_w_down[l])
    return rmsnorm(x, norm_final_w)


import jax as _jax
import jax.numpy as _jnp

TWIN_FORMAT = 'train_step'
FWD_PARAMS = ['x', 'norm_mix_w', 'norm_ffn_w', 'norm_final_w', 'ev_w_in', 'lru_conv_w', 'lru_conv_b', 'lru_w_r', 'lru_b_r', 'lru_w_i', 'lru_b_i', 'lru_lambda', 'hg_lower_bounds', 'hg_norm_w', 'ev_w_out', 'ssd_w_in', 'ssd_conv_w', 'ssd_conv_b', 'ssd_dt_bias', 'ssd_a_log', 'ssd_d', 'ssd_norm_w', 'ssd_w_out', 'ffn_w_up', 'ffn_conv_w', 'ffn_conv_b', 'ffn_w_down']
TWIN_WEIGHTS = ['norm_mix_w', 'norm_ffn_w', 'norm_final_w', 'ev_w_in', 'lru_conv_w', 'lru_conv_b', 'lru_w_r', 'lru_b_r', 'lru_w_i', 'lru_b_i', 'lru_lambda', 'hg_lower_bounds', 'hg_norm_w', 'ev_w_out', 'ssd_w_in', 'ssd_conv_w', 'ssd_conv_b', 'ssd_dt_bias', 'ssd_a_log', 'ssd_d', 'ssd_norm_w', 'ssd_w_out', 'ffn_w_up', 'ffn_conv_w', 'ffn_conv_b', 'ffn_w_down']
TWIN_DIFF_INPUT = 'x'
TWIN_INPUTS = ['x', 'norm_mix_w', 'norm_ffn_w', 'norm_final_w', 'ev_w_in', 'lru_conv_w', 'lru_conv_b', 'lru_w_r', 'lru_b_r', 'lru_w_i', 'lru_b_i', 'lru_lambda', 'hg_lower_bounds', 'hg_norm_w', 'ev_w_out', 'ssd_w_in', 'ssd_conv_w', 'ssd_conv_b', 'ssd_dt_bias', 'ssd_a_log', 'ssd_d', 'ssd_norm_w', 'ssd_w_out', 'ffn_w_up', 'ffn_conv_w', 'ffn_conv_b', 'ffn_w_down', 'loss_target', 'm_norm_mix_w', 'm_norm_ffn_w', 'm_norm_final_w', 'm_ev_w_in', 'm_lru_conv_w', 'm_lru_conv_b', 'm_lru_w_r', 'm_lru_b_r', 'm_lru_w_i', 'm_lru_b_i', 'm_lru_lambda', 'm_hg_lower_bounds', 'm_hg_norm_w', 'm_ev_w_out', 'm_ssd_w_in', 'm_ssd_conv_w', 'm_ssd_conv_b', 'm_ssd_dt_bias', 'm_ssd_a_log', 'm_ssd_d', 'm_ssd_norm_w', 'm_ssd_w_out', 'm_ffn_w_up', 'm_ffn_conv_w', 'm_ffn_conv_b', 'm_ffn_w_down', 'v_norm_mix_w', 'v_norm_ffn_w', 'v_norm_final_w', 'v_ev_w_in', 'v_lru_conv_w', 'v_lru_conv_b', 'v_lru_w_r', 'v_lru_b_r', 'v_lru_w_i', 'v_lru_b_i', 'v_lru_lambda', 'v_hg_lower_bounds', 'v_hg_norm_w', 'v_ev_w_out', 'v_ssd_w_in', 'v_ssd_conv_w', 'v_ssd_conv_b', 'v_ssd_dt_bias', 'v_ssd_a_log', 'v_ssd_d', 'v_ssd_norm_w', 'v_ssd_w_out', 'v_ffn_w_up', 'v_ffn_conv_w', 'v_ffn_conv_b', 'v_ffn_w_down']
TWIN_OUTPUTS = ['loss', 'grad_x', 'grad_norm_mix_w', 'grad_norm_ffn_w', 'grad_norm_final_w', 'grad_ev_w_in', 'grad_lru_conv_w', 'grad_lru_conv_b', 'grad_lru_w_r', 'grad_lru_b_r', 'grad_lru_w_i', 'grad_lru_b_i', 'grad_lru_lambda', 'grad_hg_lower_bounds', 'grad_hg_norm_w', 'grad_ev_w_out', 'grad_ssd_w_in', 'grad_ssd_conv_w', 'grad_ssd_conv_b', 'grad_ssd_dt_bias', 'grad_ssd_a_log', 'grad_ssd_d', 'grad_ssd_norm_w', 'grad_ssd_w_out', 'grad_ffn_w_up', 'grad_ffn_conv_w', 'grad_ffn_conv_b', 'grad_ffn_w_down', 'delta_norm_mix_w', 'delta_norm_ffn_w', 'delta_norm_final_w', 'delta_ev_w_in', 'delta_lru_conv_w', 'delta_lru_conv_b', 'delta_lru_w_r', 'delta_lru_b_r', 'delta_lru_w_i', 'delta_lru_b_i', 'delta_lru_lambda', 'delta_hg_lower_bounds', 'delta_hg_norm_w', 'delta_ev_w_out', 'delta_ssd_w_in', 'delta_ssd_conv_w', 'delta_ssd_conv_b', 'delta_ssd_dt_bias', 'delta_ssd_a_log', 'delta_ssd_d', 'delta_ssd_norm_w', 'delta_ssd_w_out', 'delta_ffn_w_up', 'delta_ffn_conv_w', 'delta_ffn_conv_b', 'delta_ffn_w_down', 'new_m_norm_mix_w', 'new_m_norm_ffn_w', 'new_m_norm_final_w', 'new_m_ev_w_in', 'new_m_lru_conv_w', 'new_m_lru_conv_b', 'new_m_lru_w_r', 'new_m_lru_b_r', 'new_m_lru_w_i', 'new_m_lru_b_i', 'new_m_lru_lambda', 'new_m_hg_lower_bounds', 'new_m_hg_norm_w', 'new_m_ev_w_out', 'new_m_ssd_w_in', 'new_m_ssd_conv_w', 'new_m_ssd_conv_b', 'new_m_ssd_dt_bias', 'new_m_ssd_a_log', 'new_m_ssd_d', 'new_m_ssd_norm_w', 'new_m_ssd_w_out', 'new_m_ffn_w_up', 'new_m_ffn_conv_w', 'new_m_ffn_conv_b', 'new_m_ffn_w_down', 'new_v_norm_mix_w', 'new_v_norm_ffn_w', 'new_v_norm_final_w', 'new_v_ev_w_in', 'new_v_lru_conv_w', 'new_v_lru_conv_b', 'new_v_lru_w_r', 'new_v_lru_b_r', 'new_v_lru_w_i', 'new_v_lru_b_i', 'new_v_lru_lambda', 'new_v_hg_lower_bounds', 'new_v_hg_norm_w', 'new_v_ev_w_out', 'new_v_ssd_w_in', 'new_v_ssd_conv_w', 'new_v_ssd_conv_b', 'new_v_ssd_dt_bias', 'new_v_ssd_a_log', 'new_v_ssd_d', 'new_v_ssd_norm_w', 'new_v_ssd_w_out', 'new_v_ffn_w_up', 'new_v_ffn_conv_w', 'new_v_ffn_conv_b', 'new_v_ffn_w_down']
TWIN_LEAF_KINDS = {'loss': 'loss', 'grad_x': 'grad_x', 'grad_norm_mix_w': 'grad_w', 'grad_norm_ffn_w': 'grad_w', 'grad_norm_final_w': 'grad_w', 'grad_ev_w_in': 'grad_w', 'grad_lru_conv_w': 'grad_w', 'grad_lru_conv_b': 'grad_w', 'grad_lru_w_r': 'grad_w', 'grad_lru_b_r': 'grad_w', 'grad_lru_w_i': 'grad_w', 'grad_lru_b_i': 'grad_w', 'grad_lru_lambda': 'grad_w', 'grad_hg_lower_bounds': 'grad_w', 'grad_hg_norm_w': 'grad_w', 'grad_ev_w_out': 'grad_w', 'grad_ssd_w_in': 'grad_w', 'grad_ssd_conv_w': 'grad_w', 'grad_ssd_conv_b': 'grad_w', 'grad_ssd_dt_bias': 'grad_w', 'grad_ssd_a_log': 'grad_w', 'grad_ssd_d': 'grad_w', 'grad_ssd_norm_w': 'grad_w', 'grad_ssd_w_out': 'grad_w', 'grad_ffn_w_up': 'grad_w', 'grad_ffn_conv_w': 'grad_w', 'grad_ffn_conv_b': 'grad_w', 'grad_ffn_w_down': 'grad_w', 'delta_norm_mix_w': 'delta_w', 'delta_norm_ffn_w': 'delta_w', 'delta_norm_final_w': 'delta_w', 'delta_ev_w_in': 'delta_w', 'delta_lru_conv_w': 'delta_w', 'delta_lru_conv_b': 'delta_w', 'delta_lru_w_r': 'delta_w', 'delta_lru_b_r': 'delta_w', 'delta_lru_w_i': 'delta_w', 'delta_lru_b_i': 'delta_w', 'delta_lru_lambda': 'delta_w', 'delta_hg_lower_bounds': 'delta_w', 'delta_hg_norm_w': 'delta_w', 'delta_ev_w_out': 'delta_w', 'delta_ssd_w_in': 'delta_w', 'delta_ssd_conv_w': 'delta_w', 'delta_ssd_conv_b': 'delta_w', 'delta_ssd_dt_bias': 'delta_w', 'delta_ssd_a_log': 'delta_w', 'delta_ssd_d': 'delta_w', 'delta_ssd_norm_w': 'delta_w', 'delta_ssd_w_out': 'delta_w', 'delta_ffn_w_up': 'delta_w', 'delta_ffn_conv_w': 'delta_w', 'delta_ffn_conv_b': 'delta_w', 'delta_ffn_w_down': 'delta_w', 'new_m_norm_mix_w': 'new_m', 'new_m_norm_ffn_w': 'new_m', 'new_m_norm_final_w': 'new_m', 'new_m_ev_w_in': 'new_m', 'new_m_lru_conv_w': 'new_m', 'new_m_lru_conv_b': 'new_m', 'new_m_lru_w_r': 'new_m', 'new_m_lru_b_r': 'new_m', 'new_m_lru_w_i': 'new_m', 'new_m_lru_b_i': 'new_m', 'new_m_lru_lambda': 'new_m', 'new_m_hg_lower_bounds': 'new_m', 'new_m_hg_norm_w': 'new_m', 'new_m_ev_w_out': 'new_m', 'new_m_ssd_w_in': 'new_m', 'new_m_ssd_conv_w': 'new_m', 'new_m_ssd_conv_b': 'new_m', 'new_m_ssd_dt_bias': 'new_m', 'new_m_ssd_a_log': 'new_m', 'new_m_ssd_d': 'new_m', 'new_m_ssd_norm_w': 'new_m', 'new_m_ssd_w_out': 'new_m', 'new_m_ffn_w_up': 'new_m', 'new_m_ffn_conv_w': 'new_m', 'new_m_ffn_conv_b': 'new_m', 'new_m_ffn_w_down': 'new_m', 'new_v_norm_mix_w': 'new_v', 'new_v_norm_ffn_w': 'new_v', 'new_v_norm_final_w': 'new_v', 'new_v_ev_w_in': 'new_v', 'new_v_lru_conv_w': 'new_v', 'new_v_lru_conv_b': 'new_v', 'new_v_lru_w_r': 'new_v', 'new_v_lru_b_r': 'new_v', 'new_v_lru_w_i': 'new_v', 'new_v_lru_b_i': 'new_v', 'new_v_lru_lambda': 'new_v', 'new_v_hg_lower_bounds': 'new_v', 'new_v_hg_norm_w': 'new_v', 'new_v_ev_w_out': 'new_v', 'new_v_ssd_w_in': 'new_v', 'new_v_ssd_conv_w': 'new_v', 'new_v_ssd_conv_b': 'new_v', 'new_v_ssd_dt_bias': 'new_v', 'new_v_ssd_a_log': 'new_v', 'new_v_ssd_d': 'new_v', 'new_v_ssd_norm_w': 'new_v', 'new_v_ssd_w_out': 'new_v', 'new_v_ffn_w_up': 'new_v', 'new_v_ffn_conv_w': 'new_v', 'new_v_ffn_conv_b': 'new_v', 'new_v_ffn_w_down': 'new_v'}


def _forward(args):
    return _fwd_reference(*[args[k] for k in FWD_PARAMS])


def _output_shape():
    def fwd():
        inp = _fwd_setup_inputs(0)
        return _fwd_reference(*[inp[k] for k in FWD_PARAMS])
    out = _jax.eval_shape(fwd)
    return out.shape, out.dtype

N_MICROBATCH = 1
ADAM_LR = 0.001
ADAM_B1 = 0.9
ADAM_B2 = 0.999
ADAM_EPS = 1e-08
ADAM_WD = 0.01
ADAM_STEP = 10
PER_EXAMPLE_BATCH_AXIS = {'x': 0, 'loss_target': 0}
SHARED_INPUTS = []
_WEIGHT_DTYPES = {'norm_mix_w': _jnp.float32, 'norm_ffn_w': _jnp.float32, 'norm_final_w': _jnp.float32, 'ev_w_in': _jnp.float32, 'lru_conv_w': _jnp.float32, 'lru_conv_b': _jnp.float32, 'lru_w_r': _jnp.float32, 'lru_b_r': _jnp.float32, 'lru_w_i': _jnp.float32, 'lru_b_i': _jnp.float32, 'lru_lambda': _jnp.float32, 'hg_lower_bounds': _jnp.float32, 'hg_norm_w': _jnp.float32, 'ev_w_out': _jnp.float32, 'ssd_w_in': _jnp.float32, 'ssd_conv_w': _jnp.float32, 'ssd_conv_b': _jnp.float32, 'ssd_dt_bias': _jnp.float32, 'ssd_a_log': _jnp.float32, 'ssd_d': _jnp.float32, 'ssd_norm_w': _jnp.float32, 'ssd_w_out': _jnp.float32, 'ffn_w_up': _jnp.float32, 'ffn_conv_w': _jnp.float32, 'ffn_conv_b': _jnp.float32, 'ffn_w_down': _jnp.float32}
MOMENT_SCALE = {'norm_mix_w': 1.228583e-01, 'norm_ffn_w': 9.341748e-02, 'norm_final_w': 3.198727e+01, 'ev_w_in': 6.587903e-02, 'lru_conv_w': 6.968262e-02, 'lru_conv_b': 3.117880e-01, 'lru_w_r': 1.512003e-02, 'lru_b_r': 1.581660e-02, 'lru_w_i': 2.656967e-02, 'lru_b_i': 2.207935e-02, 'lru_lambda': 3.200193e-02, 'hg_lower_bounds': 5.099308e-03, 'hg_norm_w': 2.702698e-01, 'ev_w_out': 7.843366e-02, 'ssd_w_in': 5.703515e-02, 'ssd_conv_w': 5.299756e-02, 'ssd_conv_b': 7.189211e-02, 'ssd_dt_bias': 1.082229e-01, 'ssd_a_log': 2.403265e-01, 'ssd_d': 3.504331e-01, 'ssd_norm_w': 6.185250e-02, 'ssd_w_out': 8.657569e-02, 'ffn_w_up': 3.939774e-02, 'ffn_conv_w': 3.954009e-02, 'ffn_conv_b': 3.908202e-02, 'ffn_w_down': 6.430130e-02}


def _to_microbatches(a, axis):
    t = _jnp.moveaxis(a, axis, 0)
    t = t.reshape((N_MICROBATCH, t.shape[0] // N_MICROBATCH) + t.shape[1:])
    return _jnp.moveaxis(t, 1, axis + 1)


def setup_inputs(seed: int = 0) -> dict:
    inp = _fwd_setup_inputs(seed)
    key = _jax.random.fold_in(_jax.random.key(seed), 7919)
    shape, _ = _output_shape()
    out = dict(inp)
    out["loss_target"] = _jax.random.normal(_jax.random.fold_in(key, 0), shape, _jnp.float32)
    for i, name in enumerate(TWIN_WEIGHTS):
        w = inp[name].astype(_jnp.float32)
        if MOMENT_SCALE is None:
            s = _jnp.sqrt(_jnp.mean(_jnp.square(w)) + 1e-30)
        else:
            s = MOMENT_SCALE[name]
        km, kv = _jax.random.split(_jax.random.fold_in(key, i + 1))
        out[name] = w
        out["m_" + name] = s * _jax.random.normal(km, w.shape, _jnp.float32)
        out["v_" + name] = (s * s) * _jax.random.uniform(kv, w.shape, _jnp.float32, 0.5, 1.5)
    if N_MICROBATCH > 1:
        for name, axis in PER_EXAMPLE_BATCH_AXIS.items():
            out[name] = _to_microbatches(out[name], axis)
    return {'x': out['x'], 'norm_mix_w': out['norm_mix_w'], 'norm_ffn_w': out['norm_ffn_w'], 'norm_final_w': out['norm_final_w'], 'ev_w_in': out['ev_w_in'], 'lru_conv_w': out['lru_conv_w'], 'lru_conv_b': out['lru_conv_b'], 'lru_w_r': out['lru_w_r'], 'lru_b_r': out['lru_b_r'], 'lru_w_i': out['lru_w_i'], 'lru_b_i': out['lru_b_i'], 'lru_lambda': out['lru_lambda'], 'hg_lower_bounds': out['hg_lower_bounds'], 'hg_norm_w': out['hg_norm_w'], 'ev_w_out': out['ev_w_out'], 'ssd_w_in': out['ssd_w_in'], 'ssd_conv_w': out['ssd_conv_w'], 'ssd_conv_b': out['ssd_conv_b'], 'ssd_dt_bias': out['ssd_dt_bias'], 'ssd_a_log': out['ssd_a_log'], 'ssd_d': out['ssd_d'], 'ssd_norm_w': out['ssd_norm_w'], 'ssd_w_out': out['ssd_w_out'], 'ffn_w_up': out['ffn_w_up'], 'ffn_conv_w': out['ffn_conv_w'], 'ffn_conv_b': out['ffn_conv_b'], 'ffn_w_down': out['ffn_w_down'], 'loss_target': out['loss_target'], 'm_norm_mix_w': out['m_norm_mix_w'], 'm_norm_ffn_w': out['m_norm_ffn_w'], 'm_norm_final_w': out['m_norm_final_w'], 'm_ev_w_in': out['m_ev_w_in'], 'm_lru_conv_w': out['m_lru_conv_w'], 'm_lru_conv_b': out['m_lru_conv_b'], 'm_lru_w_r': out['m_lru_w_r'], 'm_lru_b_r': out['m_lru_b_r'], 'm_lru_w_i': out['m_lru_w_i'], 'm_lru_b_i': out['m_lru_b_i'], 'm_lru_lambda': out['m_lru_lambda'], 'm_hg_lower_bounds': out['m_hg_lower_bounds'], 'm_hg_norm_w': out['m_hg_norm_w'], 'm_ev_w_out': out['m_ev_w_out'], 'm_ssd_w_in': out['m_ssd_w_in'], 'm_ssd_conv_w': out['m_ssd_conv_w'], 'm_ssd_conv_b': out['m_ssd_conv_b'], 'm_ssd_dt_bias': out['m_ssd_dt_bias'], 'm_ssd_a_log': out['m_ssd_a_log'], 'm_ssd_d': out['m_ssd_d'], 'm_ssd_norm_w': out['m_ssd_norm_w'], 'm_ssd_w_out': out['m_ssd_w_out'], 'm_ffn_w_up': out['m_ffn_w_up'], 'm_ffn_conv_w': out['m_ffn_conv_w'], 'm_ffn_conv_b': out['m_ffn_conv_b'], 'm_ffn_w_down': out['m_ffn_w_down'], 'v_norm_mix_w': out['v_norm_mix_w'], 'v_norm_ffn_w': out['v_norm_ffn_w'], 'v_norm_final_w': out['v_norm_final_w'], 'v_ev_w_in': out['v_ev_w_in'], 'v_lru_conv_w': out['v_lru_conv_w'], 'v_lru_conv_b': out['v_lru_conv_b'], 'v_lru_w_r': out['v_lru_w_r'], 'v_lru_b_r': out['v_lru_b_r'], 'v_lru_w_i': out['v_lru_w_i'], 'v_lru_b_i': out['v_lru_b_i'], 'v_lru_lambda': out['v_lru_lambda'], 'v_hg_lower_bounds': out['v_hg_lower_bounds'], 'v_hg_norm_w': out['v_hg_norm_w'], 'v_ev_w_out': out['v_ev_w_out'], 'v_ssd_w_in': out['v_ssd_w_in'], 'v_ssd_conv_w': out['v_ssd_conv_w'], 'v_ssd_conv_b': out['v_ssd_conv_b'], 'v_ssd_dt_bias': out['v_ssd_dt_bias'], 'v_ssd_a_log': out['v_ssd_a_log'], 'v_ssd_d': out['v_ssd_d'], 'v_ssd_norm_w': out['v_ssd_norm_w'], 'v_ssd_w_out': out['v_ssd_w_out'], 'v_ffn_w_up': out['v_ffn_w_up'], 'v_ffn_conv_w': out['v_ffn_conv_w'], 'v_ffn_conv_b': out['v_ffn_conv_b'], 'v_ffn_w_down': out['v_ffn_w_down']}


def _loss(weights, diff, rest, loss_target):
    with _jax.named_scope("forward"):
        args = {**rest, TWIN_DIFF_INPUT: diff, **{k: w.astype(_WEIGHT_DTYPES[k]) for k, w in weights.items()}}
        y = _forward(args)
    with _jax.named_scope("loss_head"):
        err = _jnp.square(y.astype(_jnp.float32) - loss_target)
        return 0.5 * _jnp.sum(_jnp.mean(err, axis=-1)) if err.ndim else 0.5 * err


def _adamw(w, g, m, v):
    m = ADAM_B1 * m + (1.0 - ADAM_B1) * g
    v = ADAM_B2 * v + (1.0 - ADAM_B2) * _jnp.square(g)
    m_hat = m / (1.0 - ADAM_B1 ** ADAM_STEP)
    v_hat = v / (1.0 - ADAM_B2 ** ADAM_STEP)
    delta = -ADAM_LR * (m_hat / (_jnp.sqrt(v_hat) + ADAM_EPS) + ADAM_WD * w)
    return delta, m, v


def reference(x, norm_mix_w, norm_ffn_w, norm_final_w, ev_w_in, lru_conv_w, lru_conv_b, lru_w_r, lru_b_r, lru_w_i, lru_b_i, lru_lambda, hg_lower_bounds, hg_norm_w, ev_w_out, ssd_w_in, ssd_conv_w, ssd_conv_b, ssd_dt_bias, ssd_a_log, ssd_d, ssd_norm_w, ssd_w_out, ffn_w_up, ffn_conv_w, ffn_conv_b, ffn_w_down, loss_target, m_norm_mix_w, m_norm_ffn_w, m_norm_final_w, m_ev_w_in, m_lru_conv_w, m_lru_conv_b, m_lru_w_r, m_lru_b_r, m_lru_w_i, m_lru_b_i, m_lru_lambda, m_hg_lower_bounds, m_hg_norm_w, m_ev_w_out, m_ssd_w_in, m_ssd_conv_w, m_ssd_conv_b, m_ssd_dt_bias, m_ssd_a_log, m_ssd_d, m_ssd_norm_w, m_ssd_w_out, m_ffn_w_up, m_ffn_conv_w, m_ffn_conv_b, m_ffn_w_down, v_norm_mix_w, v_norm_ffn_w, v_norm_final_w, v_ev_w_in, v_lru_conv_w, v_lru_conv_b, v_lru_w_r, v_lru_b_r, v_lru_w_i, v_lru_b_i, v_lru_lambda, v_hg_lower_bounds, v_hg_norm_w, v_ev_w_out, v_ssd_w_in, v_ssd_conv_w, v_ssd_conv_b, v_ssd_dt_bias, v_ssd_a_log, v_ssd_d, v_ssd_norm_w, v_ssd_w_out, v_ffn_w_up, v_ffn_conv_w, v_ffn_conv_b, v_ffn_w_down):
    given = dict(x=x, norm_mix_w=norm_mix_w, norm_ffn_w=norm_ffn_w, norm_final_w=norm_final_w, ev_w_in=ev_w_in, lru_conv_w=lru_conv_w, lru_conv_b=lru_conv_b, lru_w_r=lru_w_r, lru_b_r=lru_b_r, lru_w_i=lru_w_i, lru_b_i=lru_b_i, lru_lambda=lru_lambda, hg_lower_bounds=hg_lower_bounds, hg_norm_w=hg_norm_w, ev_w_out=ev_w_out, ssd_w_in=ssd_w_in, ssd_conv_w=ssd_conv_w, ssd_conv_b=ssd_conv_b, ssd_dt_bias=ssd_dt_bias, ssd_a_log=ssd_a_log, ssd_d=ssd_d, ssd_norm_w=ssd_norm_w, ssd_w_out=ssd_w_out, ffn_w_up=ffn_w_up, ffn_conv_w=ffn_conv_w, ffn_conv_b=ffn_conv_b, ffn_w_down=ffn_w_down, loss_target=loss_target, m_norm_mix_w=m_norm_mix_w, m_norm_ffn_w=m_norm_ffn_w, m_norm_final_w=m_norm_final_w, m_ev_w_in=m_ev_w_in, m_lru_conv_w=m_lru_conv_w, m_lru_conv_b=m_lru_conv_b, m_lru_w_r=m_lru_w_r, m_lru_b_r=m_lru_b_r, m_lru_w_i=m_lru_w_i, m_lru_b_i=m_lru_b_i, m_lru_lambda=m_lru_lambda, m_hg_lower_bounds=m_hg_lower_bounds, m_hg_norm_w=m_hg_norm_w, m_ev_w_out=m_ev_w_out, m_ssd_w_in=m_ssd_w_in, m_ssd_conv_w=m_ssd_conv_w, m_ssd_conv_b=m_ssd_conv_b, m_ssd_dt_bias=m_ssd_dt_bias, m_ssd_a_log=m_ssd_a_log, m_ssd_d=m_ssd_d, m_ssd_norm_w=m_ssd_norm_w, m_ssd_w_out=m_ssd_w_out, m_ffn_w_up=m_ffn_w_up, m_ffn_conv_w=m_ffn_conv_w, m_ffn_conv_b=m_ffn_conv_b, m_ffn_w_down=m_ffn_w_down, v_norm_mix_w=v_norm_mix_w, v_norm_ffn_w=v_norm_ffn_w, v_norm_final_w=v_norm_final_w, v_ev_w_in=v_ev_w_in, v_lru_conv_w=v_lru_conv_w, v_lru_conv_b=v_lru_conv_b, v_lru_w_r=v_lru_w_r, v_lru_b_r=v_lru_b_r, v_lru_w_i=v_lru_w_i, v_lru_b_i=v_lru_b_i, v_lru_lambda=v_lru_lambda, v_hg_lower_bounds=v_hg_lower_bounds, v_hg_norm_w=v_hg_norm_w, v_ev_w_out=v_ev_w_out, v_ssd_w_in=v_ssd_w_in, v_ssd_conv_w=v_ssd_conv_w, v_ssd_conv_b=v_ssd_conv_b, v_ssd_dt_bias=v_ssd_dt_bias, v_ssd_a_log=v_ssd_a_log, v_ssd_d=v_ssd_d, v_ssd_norm_w=v_ssd_norm_w, v_ssd_w_out=v_ssd_w_out, v_ffn_w_up=v_ffn_w_up, v_ffn_conv_w=v_ffn_conv_w, v_ffn_conv_b=v_ffn_conv_b, v_ffn_w_down=v_ffn_w_down)
    weights = {n: given[n] for n in TWIN_WEIGHTS}
    shared = {n: given[n] for n in SHARED_INPUTS}
    per_example = {n: given[n] for n in ['x']}
    grad_fn = _jax.value_and_grad(_loss, argnums=(0, 1))

    def one_microbatch(ex, loss_target):
        ex = dict(ex)
        diff = ex.pop(TWIN_DIFF_INPUT)
        return grad_fn(weights, diff, {**shared, **ex}, loss_target)

    if N_MICROBATCH == 1:
        loss, (grad_w, grad_x) = one_microbatch(per_example, given["loss_target"])
    else:
        def body(carry, xs):
            loss_sum, grad_sum = carry
            l_k, (gw_k, gx_k) = one_microbatch(xs[0], xs[1])
            with _jax.named_scope("update"):
                return (loss_sum + l_k, _jax.tree.map(_jnp.add, grad_sum, gw_k)), gx_k

        init = (_jnp.zeros((), _jnp.float32), _jax.tree.map(_jnp.zeros_like, weights))
        (loss, grad_w), grad_x = _jax.lax.scan(body, init, (per_example, given["loss_target"]))
    with _jax.named_scope("update"):
        delta_w, new_m, new_v = {}, {}, {}
        for n in TWIN_WEIGHTS:
            delta_w[n], new_m[n], new_v[n] = _adamw(weights[n], grad_w[n], given["m_" + n], given["v_" + n])
    return (loss, grad_x, *[grad_w[n] for n in TWIN_WEIGHTS], *[delta_w[n] for n in TWIN_WEIGHTS],
            *[new_m[n] for n in TWIN_WEIGHTS], *[new_v[n] for n in TWIN_WEIGHTS])
```

```python
import functools
import math

import jax
import jax.numpy as jnp
from jax import lax
from jax.experimental import pallas as pl
from jax.experimental.pallas import tpu as pltpu

F32 = jnp.float32
BF16 = jnp.bfloat16
MXU = jnp.bfloat16
HI = lax.Precision.HIGHEST

N_DEV = 8
EPS = 1e-6
LRU_C = 8.0
CHUNK = 64
STATE = 128
ADAM_LR, ADAM_B1, ADAM_B2, ADAM_EPS, ADAM_WD, ADAM_STEP = 0.001, 0.9, 0.999, 1e-08, 0.01, 10

SUBLANES = 8
VMEM_LIMIT = 56 * 1024 * 1024

NN = (((1,), (0,)), ((), ()))
NT = (((1,), (1,)), ((), ()))
TN = (((0,), (0,)), ((), ()))


def _params(sem):
    return pltpu.CompilerParams(dimension_semantics=sem, vmem_limit_bytes=VMEM_LIMIT)


def _divisor(n, target, align):
    if n <= target:
        return n
    best = None
    for d in range(align, target + 1, align):
        if n % d == 0:
            best = d
    assert best is not None, (n, target, align)
    return best


def _mm(a, b, dn):
    return lax.dot_general(a.astype(MXU), b.astype(MXU), dn, preferred_element_type=F32)


@jax.custom_vjp
def mm_nn(a, b):
    return _mm(a, b, NN)


@jax.custom_vjp
def mm_nt(a, b):
    return _mm(a, b, NT)


@jax.custom_vjp
def mm_tn(a, b):
    return _mm(a, b, TN)


mm_nn.defvjp(lambda a, b: (_mm(a, b, NN), (a, b)), lambda r, g: (mm_nt(g, r[1]), mm_tn(r[0], g)))
mm_nt.defvjp(lambda a, b: (_mm(a, b, NT), (a, b)), lambda r, g: (mm_nn(g, r[1]), mm_tn(g, r[0])))
mm_tn.defvjp(lambda a, b: (_mm(a, b, TN), (a, b)), lambda r, g: (mm_nt(r[1], g), mm_nn(r[0], g)))


def dot_hi(a, b, dn=NN):
    return lax.dot_general(a, b, dn, precision=HI, preferred_element_type=F32)


def _iota(shape, dim):
    return lax.broadcasted_iota(jnp.int32, shape, dim)


def _tril(n):
    return (_iota((n, n), 0) >= _iota((n, n), 1)).astype(F32)


def _softplus(x):
    return jnp.maximum(x, 0.0) + jnp.log1p(jnp.exp(-jnp.abs(x)))


def _neg_expm1(x):
    series = -x * (1.0 + x * (0.5 + x * (1.0 / 6.0 + x * (1.0 / 24.0))))
    return jnp.where(x > -0.03, series, 1.0 - jnp.exp(x))


def matmul(name, a, b, *, ta=False, tb=False, res=None, out_dtype=F32, tm=1024, tn=1024, tk=512):
    m, k = (a.shape[1], a.shape[0]) if ta else a.shape
    n = b.shape[0] if tb else b.shape[1]
    assert (b.shape[1] if tb else b.shape[0]) == k, (name, a.shape, b.shape)
    tm, tn, tk = _divisor(m, tm, 128), _divisor(n, tn, 128), _divisor(k, tk, 128)
    nk = k // tk
    dn = (((0 if ta else 1,), (1 if tb else 0,)), ((), ()))

    def body(*refs):
        a_ref, b_ref = refs[0], refs[1]
        r_ref = refs[2] if res is not None else None
        o_ref, acc_ref = refs[-2], refs[-1]
        kk = pl.program_id(2)

        @pl.when(kk == 0)
        def _():
            acc_ref[...] = jnp.zeros_like(acc_ref)

        acc_ref[...] += _mm(a_ref[...], b_ref[...], dn)

        @pl.when(kk == nk - 1)
        def _():
            r = acc_ref[...]
            if r_ref is not None:
                r = r + r_ref[...]
            o_ref[...] = r.astype(o_ref.dtype)

    a_spec = pl.BlockSpec((tk, tm), lambda i, j, kk: (kk, i)) if ta else pl.BlockSpec((tm, tk), lambda i, j, kk: (i, kk))
    b_spec = pl.BlockSpec((tn, tk), lambda i, j, kk: (j, kk)) if tb else pl.BlockSpec((tk, tn), lambda i, j, kk: (kk, j))
    o_spec = pl.BlockSpec((tm, tn), lambda i, j, kk: (i, j))
    ins, specs = [a, b], [a_spec, b_spec]
    if res is not None:
        ins.append(res)
        specs.append(o_spec)
    return pl.pallas_call(
        body, name=name, grid=(m // tm, n // tn, nk), in_specs=specs, out_specs=o_spec,
        out_shape=jax.ShapeDtypeStruct((m, n), out_dtype),
        scratch_shapes=[pltpu.VMEM((tm, tn), F32)],
        compiler_params=_params(("parallel", "parallel", "arbitrary")),
    )(*ins)


def Row(arr, width=None, off=0, var=True):
    return ("row", arr, arr.shape[1] if width is None else width, off, var)


def Prev(arr, width=None, off=0, var=True):
    return ("prev", arr, arr.shape[1] if width is None else width, off, var)


def Next(arr, width=None, off=0, var=True):
    return ("next", arr, arr.shape[1] if width is None else width, off, var)


def Full(arr, width=None, off=0, var=True):
    return ("full", arr, arr.shape[1] if width is None else width, off, var)


def rows_call(name, fn, rows, tile, ncol, ins, outs=(), accs=()):
    nrow = rows // tile
    assert rows % tile == 0 and tile % SUBLANES == 0, (name, rows, tile)
    last8 = rows // SUBLANES - 1
    per8 = tile // SUBLANES

    def spec(kind, arr, width, off, var):
        col = (lambda j: off + j) if var else (lambda j: off)
        if kind == "row":
            return pl.BlockSpec((tile, width), lambda j, i: (i, col(j)))
        if kind == "prev":
            return pl.BlockSpec((SUBLANES, width), lambda j, i: (jnp.maximum(i * per8 - 1, 0), col(j)))
        if kind == "next":
            return pl.BlockSpec((SUBLANES, width), lambda j, i: (jnp.minimum((i + 1) * per8, last8), col(j)))
        return pl.BlockSpec((arr.shape[0], width), lambda j, i: (0, col(j)))

    n_in, n_out = len(ins), len(outs)

    def body(*refs):
        j, i = pl.program_id(0), pl.program_id(1)
        o_tiles, a_tiles = fn(i, j, *[r[...] for r in refs[:n_in]])
        for r, o in zip(refs[n_in:n_in + n_out], o_tiles, strict=True):
            r[...] = o.astype(r.dtype)
        acc_refs = refs[n_in + n_out:]
        if acc_refs:
            @pl.when(i == 0)
            def _():
                for r in acc_refs:
                    r[...] = jnp.zeros_like(r)
            for r, a in zip(acc_refs, a_tiles, strict=True):
                r[...] += a

    out_shape = [jax.ShapeDtypeStruct((rows, w * ncol), dt) for dt, w in outs]
    out_shape += [jax.ShapeDtypeStruct((r, w * ncol), F32) for r, w in accs]
    out_specs = [pl.BlockSpec((tile, w), lambda j, i: (i, j)) for _, w in outs]
    out_specs += [pl.BlockSpec((r, w), lambda j, i: (0, j)) for r, w in accs]
    res = pl.pallas_call(
        body, name=name, grid=(ncol, nrow), in_specs=[spec(*s) for s in ins], out_specs=out_specs,
        out_shape=out_shape, compiler_params=_params(("arbitrary", "arbitrary")),
    )(*[s[1] for s in ins])
    return res


def _shift_down(tile, prev8, s, first):
    if s == 0:
        return tile
    rolled = pltpu.roll(tile, s, 0)
    pr = jnp.where(first, 0.0, pltpu.roll(prev8, s, 0))
    head = jnp.where(_iota(pr.shape, 0) < s, pr, rolled[:SUBLANES])
    return jnp.concatenate([head, rolled[SUBLANES:]], axis=0)


def _shift_up(tile, next8, s, last):
    if s == 0:
        return tile
    t = tile.shape[0]
    rolled = pltpu.roll(tile, t - s, 0)
    nx = jnp.where(last, 0.0, pltpu.roll(next8, SUBLANES - s, 0))
    tail = jnp.where(_iota(nx.shape, 0) >= SUBLANES - s, nx, rolled[t - SUBLANES:])
    return jnp.concatenate([rolled[:t - SUBLANES], tail], axis=0)


def _row(w, k):
    return jnp.sum(jnp.where(_iota(w.shape, 0) == k, w, 0.0), axis=0, keepdims=True)


def _conv(x, prev8, w, b, first):
    kk = w.shape[0]
    y = b + _row(w, kk - 1) * x
    for k in range(kk - 1):
        y = y + _row(w, k) * _shift_down(x, prev8, kk - 1 - k, first)
    return y


def _conv_wgrad(x, prev8, dy, kk, first):
    out = jnp.zeros((SUBLANES, x.shape[1]), F32)
    for k in range(kk):
        r = jnp.sum(dy * _shift_down(x, prev8, kk - 1 - k, first), axis=0, keepdims=True)
        out = out + jnp.where(_iota(out.shape, 0) == k, r, 0.0)
    return out


def _conv_t(dy, next8, w, last):
    kk = w.shape[0]
    dx = _row(w, kk - 1) * dy
    for k in range(kk - 1):
        dx = dx + _row(w, k) * _shift_up(dy, next8, kk - 1 - k, last)
    return dx


def f_rms(x, w):
    return x * lax.rsqrt(jnp.mean(x * x, axis=-1, keepdims=True) + EPS) * w


def rms_fwd(name, x, w):
    def fn(i, j, xt, wt):
        return [f_rms(xt, wt)], []
    return rows_call(name, fn, x.shape[0], _divisor(x.shape[0], 512, 8), 1, [Row(x), Full(w)],
                     outs=[(BF16, x.shape[1])])[0]


def rms_bwd(name, x, w, dh, dres):
    def fn(i, j, xt, wt, dht, drt):
        _, vjp = jax.vjp(f_rms, xt, wt)
        dx, dw = vjp(dht)
        return [drt + dx], [dw]
    d = x.shape[1]
    return rows_call(name, fn, x.shape[0], _divisor(x.shape[0], 256, 8), 1, [Row(x), Full(w), Row(dh), Row(dres)],
                     outs=[(F32, d)], accs=[(1, d)])


def loss_head(name, x, w, target):
    def fn(i, j, xt, wt, tt):
        def f(xx, ww):
            err = f_rms(xx, ww) - tt
            return 0.5 * jnp.mean(err * err, axis=-1, keepdims=True)
        rows, vjp = jax.vjp(f, xt, wt)
        dx, dw = vjp(jnp.ones_like(rows))
        return [dx], [dw, jnp.broadcast_to(jnp.sum(rows, axis=0, keepdims=True), (1, 128))]
    d = x.shape[1]
    return rows_call(name, fn, x.shape[0], _divisor(x.shape[0], 256, 8), 1, [Row(x), Full(w), Row(target)],
                     outs=[(F32, d)], accs=[(1, d), (1, 128)])


def ffn_act_fwd(name, u0, cw, cb):
    t, two_f = u0.shape
    wc = _divisor(two_f // 2, 512, 128)
    nc = two_f // 2 // wc

    def fn(i, j, ug, ugp, uv, uvp, wg, wv, bg, bv):
        first = i == 0
        g = _conv(ug, ugp, wg, bg, first)
        v = _conv(uv, uvp, wv, bv, first)
        return [jax.nn.silu(g) * v], []
    ins = [Row(u0, wc), Prev(u0, wc), Row(u0, wc, nc), Prev(u0, wc, nc),
           Full(cw, wc), Full(cw, wc, nc), Full(cb, wc), Full(cb, wc, nc)]
    return rows_call(name, fn, t, _divisor(t, 1024, 8), nc, ins, outs=[(BF16, wc)])[0]


def ffn_act_bwd(name, u0, cw, cb, dact):
    t, two_f = u0.shape
    wc = _divisor(two_f // 2, 512, 128)
    nc = two_f // 2 // wc
    kk = cw.shape[0]

    def fn(i, j, ug, ugp, uv, uvp, wg, wv, bg, bv, da):
        first = i == 0
        g = _conv(ug, ugp, wg, bg, first)
        v = _conv(uv, uvp, wv, bv, first)
        _, vjp = jax.vjp(lambda gg, vv: jax.nn.silu(gg) * vv, g, v)
        dg, dv = vjp(da)
        accs = [_conv_wgrad(ug, ugp, dg, kk, first), _conv_wgrad(uv, uvp, dv, kk, first),
                jnp.sum(dg, axis=0, keepdims=True), jnp.sum(dv, axis=0, keepdims=True)]
        return [dg, dv], accs
    ins = [Row(u0, wc), Prev(u0, wc), Row(u0, wc, nc), Prev(u0, wc, nc),
           Full(cw, wc), Full(cw, wc, nc), Full(cb, wc), Full(cb, wc, nc), Row(dact, wc)]
    return rows_call(name, fn, t, _divisor(t, 512, 8), nc, ins, outs=[(F32, wc), (F32, wc)],
                     accs=[(SUBLANES, wc), (SUBLANES, wc), (1, wc), (1, wc)])


def conv_t_call(name, dy, cw, col_off, width):
    out_dtype = BF16
    t, c = dy.shape
    nc = c // width
    nrow_tile = _divisor(t, 512, 8)
    last_i = t // nrow_tile - 1

    def fn(i, j, d, dn, w):
        return [_conv_t(d, dn, w, i == last_i)], []
    return rows_call(name, fn, t, nrow_tile, nc, [Row(dy, width), Next(dy, width), Full(cw, width, col_off)],
                     outs=[(out_dtype, width)])[0]


def f_lru_gates(xc, wr, br, wi, bi, lam):
    r = jax.nn.sigmoid(mm_nn(xc, wr) + br)
    gi = jax.nn.sigmoid(mm_nn(xc, wi) + bi)
    log_a = -LRU_C * r * _softplus(-lam)
    a = jnp.exp(log_a)
    u = jnp.sqrt(_neg_expm1(2.0 * log_a)) * (gi * xc)
    return a, u


def lru_pre_fwd(name, proj, cw, cb, wr, br, wi, bi, lam):
    t, w = proj.shape[0], lam.shape[1]

    def fn(i, j, xa, xap, cwt, cbt, wrt, brt, wit, bit, lamt):
        xc = _conv(xa, xap, cwt, cbt, i == 0)
        a, u = f_lru_gates(xc, wrt, brt, wit, bit, lamt)
        return [xc, a, u], []
    ins = [Row(proj, w), Prev(proj, w), Full(cw), Full(cb), Full(wr), Full(br), Full(wi), Full(bi), Full(lam)]
    return rows_call(name, fn, t, _divisor(t, 512, 8), 1, ins, outs=[(F32, w)] * 3)


def lru_pre_bwd(name, proj, cw, cb, wr, br, wi, bi, lam, xc, hseq, lamb):
    t, w = proj.shape[0], lam.shape[1]
    kk = cw.shape[0]

    def fn(i, j, xa, xap, xct, hs, hsp, lb, wrt, brt, wit, bit, lamt):
        first = i == 0
        da = lb * _shift_down(hs, hsp, 1, first)
        _, vjp = jax.vjp(f_lru_gates, xct, wrt.astype(F32), brt, wit.astype(F32), bit, lamt)
        dxc, dwr, dbr, dwi, dbi, dlam = vjp((da, lb))
        accs = [_conv_wgrad(xa, xap, dxc, kk, first), jnp.sum(dxc, axis=0, keepdims=True), dwr, dbr, dwi, dbi, dlam]
        return [dxc], accs
    ins = [Row(proj, w), Prev(proj, w), Row(xc), Row(hseq), Prev(hseq), Row(lamb),
           Full(wr), Full(br), Full(wi), Full(bi), Full(lam)]
    return rows_call(name, fn, t, _divisor(t, 256, 8), 1, ins, outs=[(F32, w)],
                     accs=[(SUBLANES, w), (1, w), (w, w), (1, w), (w, w), (1, w), (1, w)])


def lin_scan(name, a, x, reverse):
    t, c = a.shape
    tile = _divisor(t, 512, 8)
    n = t // tile

    def body(a_ref, x_ref, o_ref, c_ref):
        @pl.when(pl.program_id(0) == 0)
        def _():
            c_ref[...] = jnp.zeros_like(c_ref)

        def step(s, carry):
            r = (tile - 1 - s) if reverse else s
            at, xt = a_ref[pl.ds(r, 1), :], x_ref[pl.ds(r, 1), :]
            o = (xt + carry) if reverse else (at * carry + xt)
            o_ref[pl.ds(r, 1), :] = o
            return (at * o) if reverse else o
        c_ref[...] = lax.fori_loop(0, tile, step, c_ref[...], unroll=8)

    spec = pl.BlockSpec((tile, c), (lambda i: (n - 1 - i, 0)) if reverse else (lambda i: (i, 0)))
    return pl.pallas_call(
        body, name=name, grid=(n,), in_specs=[spec, spec], out_specs=spec,
        out_shape=jax.ShapeDtypeStruct((t, c), F32), scratch_shapes=[pltpu.VMEM((1, c), F32)],
        compiler_params=_params(("arbitrary",)),
    )(a, x)


def _hg_chunk(s, q, fr, v, lb):
    f = lb + (1.0 - lb) * jax.nn.sigmoid(fr)
    k = 1.0 - f
    g = jnp.log(f)
    qs = jax.nn.silu(q) * (STATE ** -0.5)
    cum = dot_hi(_tril(CHUNK), g)
    tot = jnp.sum(g, axis=0, keepdims=True)
    mid = jnp.sum(jnp.where(_iota(g.shape, 0) < CHUNK // 2, g, 0.0), axis=0, keepdims=True)
    scores = mm_nt(qs * jnp.exp(cum - mid), k * jnp.exp(mid - cum))
    scores = jnp.where(_tril(CHUNK) > 0, scores, 0.0)
    o = mm_nn(scores, v) + mm_nn(qs * jnp.exp(cum), s)
    decay = jnp.broadcast_to(jnp.exp(tot), s.shape).T
    s_new = decay * s + mm_tn(k * jnp.exp(tot - cum), v)
    return o, s_new


def _hg_specs(proj, heads, lru_w, hg_w, rows, rev):
    nblk = proj.shape[0] // rows
    blk = (lambda b: nblk - 1 - b) if rev else (lambda b: b)
    base = 2 * lru_w // STATE
    per = hg_w // STATE
    col = [pl.BlockSpec((rows, STATE), functools.partial(lambda h, b, o: (blk(b), o + h), o=base + k * per))
           for k in range(3)]
    return nblk, blk, col


def hg_fwd(name, proj, lbs, lru_w, cb=4):
    t, hg_w = proj.shape[0], lbs.shape[1]
    heads = hg_w // STATE
    cb = min(cb, t // CHUNK)
    rows = cb * CHUNK
    nblk, blk, col = _hg_specs(proj, heads, lru_w, hg_w, rows, False)

    def body(q_ref, f_ref, v_ref, lb_ref, o_ref, s_ref, st):
        @pl.when(pl.program_id(1) == 0)
        def _():
            st[...] = jnp.zeros_like(st)
        s = st[...]
        for c in range(cb):
            sl = slice(c * CHUNK, (c + 1) * CHUNK)
            s_ref[c] = s
            o, s = _hg_chunk(s, q_ref[sl, :], f_ref[sl, :], v_ref[sl, :], lb_ref[...])
            o_ref[sl, :] = o
        st[...] = s

    return pl.pallas_call(
        body, name=name, grid=(heads, nblk),
        in_specs=col + [pl.BlockSpec((1, STATE), lambda h, b: (0, h))],
        out_specs=[pl.BlockSpec((rows, STATE), lambda h, b: (b, h)),
                   pl.BlockSpec((cb, None, STATE, STATE), lambda h, b: (b, h, 0, 0))],
        out_shape=[jax.ShapeDtypeStruct((t, hg_w), F32),
                   jax.ShapeDtypeStruct((t // CHUNK, heads, STATE, STATE), F32)],
        scratch_shapes=[pltpu.VMEM((STATE, STATE), F32)],
        compiler_params=_params(("arbitrary", "arbitrary")),
    )(proj, proj, proj, lbs)


def hg_bwd(name, proj, lbs, states, do, lru_w, cb=4):
    t, hg_w = proj.shape[0], lbs.shape[1]
    heads = hg_w // STATE
    cb = min(cb, t // CHUNK)
    rows = cb * CHUNK
    nblk, blk, col = _hg_specs(proj, heads, lru_w, hg_w, rows, True)

    def body(q_ref, f_ref, v_ref, lb_ref, s_ref, do_ref, dq_ref, df_ref, dv_ref, dlb_ref, dst):
        @pl.when(pl.program_id(1) == 0)
        def _():
            dst[...] = jnp.zeros_like(dst)
            dlb_ref[...] = jnp.zeros_like(dlb_ref)
        ds = dst[...]
        dlb = jnp.zeros((1, STATE), F32)
        for c in reversed(range(cb)):
            sl = slice(c * CHUNK, (c + 1) * CHUNK)
            _, vjp = jax.vjp(_hg_chunk, s_ref[c], q_ref[sl, :], f_ref[sl, :], v_ref[sl, :], lb_ref[...])
            ds, dq, df, dv, dl = vjp((do_ref[sl, :], ds))
            dq_ref[sl, :] = dq.astype(dq_ref.dtype)
            df_ref[sl, :] = df.astype(df_ref.dtype)
            dv_ref[sl, :] = dv.astype(dv_ref.dtype)
            dlb = dlb + dl
        dst[...] = ds
        dlb_ref[...] += dlb

    rspec = pl.BlockSpec((rows, STATE), lambda h, b: (blk(b), h))
    return pl.pallas_call(
        body, name=name, grid=(heads, nblk),
        in_specs=col + [pl.BlockSpec((1, STATE), lambda h, b: (0, h)),
                        pl.BlockSpec((cb, None, STATE, STATE), lambda h, b: (blk(b), h, 0, 0)), rspec],
        out_specs=[rspec, rspec, rspec, pl.BlockSpec((1, STATE), lambda h, b: (0, h))],
        out_shape=[jax.ShapeDtypeStruct((t, hg_w), BF16)] * 3 + [jax.ShapeDtypeStruct((1, hg_w), F32)],
        scratch_shapes=[pltpu.VMEM((STATE, STATE), F32)],
        compiler_params=_params(("arbitrary", "arbitrary")),
    )(proj, proj, proj, lbs, states, do)


def f_even_post(hseq, ga, ob, gb, nw):
    parts = [hseq * jax.nn.gelu(ga)]
    for h in range(ob.shape[1] // STATE):
        o = ob[:, h * STATE:(h + 1) * STATE]
        on = o * lax.rsqrt(jnp.mean(o * o, axis=-1, keepdims=True) + EPS) * nw
        parts.append(on * jax.nn.silu(gb[:, h * STATE:(h + 1) * STATE]))
    return jnp.concatenate(parts, axis=-1)


def _even_post_ins(proj, hseq, ob, nw):
    w, v = hseq.shape[1], ob.shape[1]
    assert w == v
    return [Row(hseq), Row(proj, w, 1), Row(ob), Row(proj, v, (2 * w + 3 * v) // v), Full(nw)]


def even_post_fwd(name, proj, hseq, ob, nw):
    t = proj.shape[0]

    def fn(i, j, hs, ga, o, gb, nwt):
        return [f_even_post(hs, ga, o, gb, nwt)], []
    return rows_call(name, fn, t, _divisor(t, 256, 8), 1, _even_post_ins(proj, hseq, ob, nw),
                     outs=[(BF16, hseq.shape[1] + ob.shape[1])])[0]


def even_post_bwd(name, proj, hseq, ob, nw, dy):
    t, w, v = proj.shape[0], hseq.shape[1], ob.shape[1]

    def fn(i, j, hs, ga, o, gb, nwt, dyt):
        _, vjp = jax.vjp(f_even_post, hs, ga, o, gb, nwt)
        dhs, dga, dob, dgb, dnw = vjp(dyt)
        return [dhs, dga, dob, dgb], [dnw]
    return rows_call(name, fn, t, _divisor(t, 256, 8), 1, _even_post_ins(proj, hseq, ob, nw) + [Row(dy)],
                     outs=[(F32, w), (BF16, w), (F32, v), (BF16, v)], accs=[(1, STATE)])


def f_lbs(hb):
    e = jnp.exp(hb - jnp.max(hb, axis=0, keepdims=True))
    p = e / jnp.sum(e, axis=0, keepdims=True)
    out, run = jnp.zeros_like(p), jnp.zeros_like(p[:1])
    for r in range(hb.shape[0]):
        run = run + _row(p, r)
        out = out + jnp.where(_iota(p.shape, 0) == r, run - _row(p, 0), 0.0)
    return out


def whole_call(name, fn, ins, out_shapes):
    def body(*refs):
        outs = fn(*[r[...] for r in refs[:len(ins)]])
        for r, o in zip(refs[len(ins):], outs, strict=True):
            r[...] = o
    return pl.pallas_call(body, name=name, out_shape=[jax.ShapeDtypeStruct(s, F32) for s in out_shapes])(*ins)


HEADDIM = 64


def _ssd_pair(s, xp, bm, cm, dtr, dtb, alog, dsk, hd1):
    lanes = 2 * HEADDIM
    spread = (_iota((lanes, lanes), 0) == hd1 + (_iota((lanes, lanes), 1) >= HEADDIM)).astype(F32)
    dt_all = _softplus(dtr + dtb)
    da_all = dt_all * (-jnp.exp(alog))
    cum_all = dot_hi(_tril(CHUNK), da_all)
    dt = dot_hi(dt_all, spread)
    da = dot_hi(da_all, spread)
    cum = dot_hi(cum_all, spread)
    tot = jnp.sum(da, axis=0, keepdims=True)
    xdt = xp * dt
    cbm = mm_nt(cm, bm)
    causal = _tril(CHUNK) > 0
    yd = []
    for hh in range(2):
        rows = (_iota((CHUNK, lanes), 1) == hd1 + hh).astype(F32)
        cols = (_iota((lanes, CHUNK), 0) == hd1 + hh).astype(F32)
        seg = dot_hi(cum_all, cols) - dot_hi(rows, cum_all, NT)
        m = jnp.where(causal, cbm * jnp.exp(jnp.where(causal, seg, 0.0)), 0.0)
        yd.append(mm_nn(m, xdt))
    y = jnp.where(_iota(xp.shape, 1) < HEADDIM, yd[0], yd[1])
    y = y + mm_nt(cm, s) * jnp.exp(cum) + xp * dsk
    decay = jnp.broadcast_to(jnp.exp(tot), s.shape).T
    s_new = decay * s + mm_tn(xdt * jnp.exp(tot - cum), bm)
    return y, s_new


def _ssd_specs(act, inner, groups, rev):
    t = act.shape[0]
    nch = t // CHUNK
    ch = (lambda c: nch - 1 - c) if rev else (lambda c: c)
    gw = inner // groups
    specs = [pl.BlockSpec((CHUNK, gw), lambda c, g: (ch(c), g)),
             pl.BlockSpec((CHUNK, STATE), lambda c, g: (ch(c), inner // STATE + g)),
             pl.BlockSpec((CHUNK, STATE), lambda c, g: (ch(c), inner // STATE + groups + g)),
             pl.BlockSpec((CHUNK, STATE), lambda c, g: (ch(c), 0)),
             pl.BlockSpec((1, STATE), lambda c, g: (0, 0)),
             pl.BlockSpec((1, STATE), lambda c, g: (0, 0)),
             pl.BlockSpec((1, gw), lambda c, g: (0, g))]
    return nch, ch, gw, specs


def ssd_fwd(name, act, dtr, dtb, alog, dexp, inner, groups):
    t = act.shape[0]
    nch, ch, gw, specs = _ssd_specs(act, inner, groups, False)
    pairs = gw // (2 * HEADDIM)
    hpg = 2 * pairs

    def body(x_ref, b_ref, c_ref, dtr_ref, dtb_ref, alog_ref, dsk_ref, y_ref, sv_ref, st):
        c, g = pl.program_id(0), pl.program_id(1)

        @pl.when(c == 0)
        def _():
            for pp in range(pairs):
                st[g * pairs + pp] = jnp.zeros((STATE, STATE), F32)

        for pp in range(pairs):
            lanes = slice(pp * STATE, (pp + 1) * STATE)
            slot = g * pairs + pp
            s = st[slot]
            sv_ref[pp] = s
            y, s = _ssd_pair(s, x_ref[:, lanes], b_ref[...], c_ref[...], dtr_ref[...], dtb_ref[...], alog_ref[...],
                             dsk_ref[:, lanes], g * hpg + 2 * pp)
            y_ref[:, lanes] = y
            st[slot] = s

    return pl.pallas_call(
        body, name=name, grid=(nch, groups), in_specs=specs,
        out_specs=[pl.BlockSpec((CHUNK, gw), lambda c, g: (c, g)),
                   pl.BlockSpec((None, pairs, STATE, STATE), lambda c, g: (c, g, 0, 0))],
        out_shape=[jax.ShapeDtypeStruct((t, inner), F32),
                   jax.ShapeDtypeStruct((nch, groups * pairs, STATE, STATE), F32)],
        scratch_shapes=[pltpu.VMEM((groups * pairs, STATE, STATE), F32)],
        compiler_params=_params(("arbitrary", "arbitrary")),
    )(act, act, act, dtr, dtb, alog, dexp)


def ssd_bwd(name, act, dtr, dtb, alog, dexp, states, dy, inner, groups):
    t = act.shape[0]
    nch, ch, gw, specs = _ssd_specs(act, inner, groups, True)
    pairs = gw // (2 * HEADDIM)
    hpg = 2 * pairs

    def body(x_ref, b_ref, c_ref, dtr_ref, dtb_ref, alog_ref, dsk_ref, sv_ref, dy_ref,
             dx_ref, db_ref, dc_ref, ddtr_ref, ddtb_ref, dalog_ref, ddsk_ref, dst):
        c, g = pl.program_id(0), pl.program_id(1)

        @pl.when((c == 0) & (g == 0))
        def _():
            ddtb_ref[...] = jnp.zeros_like(ddtb_ref)
            dalog_ref[...] = jnp.zeros_like(dalog_ref)
            ddsk_ref[...] = jnp.zeros_like(ddsk_ref)

        @pl.when(c == 0)
        def _():
            for pp in range(pairs):
                dst[g * pairs + pp] = jnp.zeros((STATE, STATE), F32)

        @pl.when(g == 0)
        def _():
            ddtr_ref[...] = jnp.zeros_like(ddtr_ref)

        db = jnp.zeros((CHUNK, STATE), F32)
        dc = jnp.zeros((CHUNK, STATE), F32)
        ddtr = jnp.zeros((CHUNK, STATE), F32)
        ddtb = jnp.zeros((1, STATE), F32)
        dalog = jnp.zeros((1, STATE), F32)
        for pp in range(pairs):
            lanes = slice(pp * STATE, (pp + 1) * STATE)
            slot = g * pairs + pp
            f = functools.partial(_ssd_pair, hd1=g * hpg + 2 * pp)
            _, vjp = jax.vjp(f, sv_ref[pp], x_ref[:, lanes], b_ref[...], c_ref[...], dtr_ref[...], dtb_ref[...],
                             alog_ref[...], dsk_ref[:, lanes])
            ds, dxp, dbp, dcp, ddtrp, ddtbp, dalogp, ddskp = vjp((dy_ref[:, lanes], dst[slot]))
            dst[slot] = ds
            dx_ref[:, lanes] = dxp
            db, dc, ddtr, ddtb, dalog = db + dbp, dc + dcp, ddtr + ddtrp, ddtb + ddtbp, dalog + dalogp
            col = pl.ds(pl.multiple_of(g * gw + pp * STATE, STATE), STATE)
            ddsk_ref[:, col] = ddsk_ref[:, col] + ddskp
        db_ref[...] = db
        dc_ref[...] = dc
        ddtr_ref[...] += ddtr
        ddtb_ref[...] += ddtb
        dalog_ref[...] += dalog

    one = pl.BlockSpec((1, STATE), lambda c, g: (0, 0))
    return pl.pallas_call(
        body, name=name, grid=(nch, groups),
        in_specs=specs + [pl.BlockSpec((None, pairs, STATE, STATE), lambda c, g: (ch(c), g, 0, 0)),
                          pl.BlockSpec((CHUNK, gw), lambda c, g: (ch(c), g))],
        out_specs=[pl.BlockSpec((CHUNK, gw), lambda c, g: (ch(c), g)),
                   pl.BlockSpec((CHUNK, STATE), lambda c, g: (ch(c), g)),
                   pl.BlockSpec((CHUNK, STATE), lambda c, g: (ch(c), g)),
                   pl.BlockSpec((CHUNK, STATE), lambda c, g: (ch(c), 0)),
                   one, one, pl.BlockSpec((1, inner), lambda c, g: (0, 0))],
        out_shape=[jax.ShapeDtypeStruct((t, inner), F32), jax.ShapeDtypeStruct((t, groups * STATE), F32),
                   jax.ShapeDtypeStruct((t, groups * STATE), F32), jax.ShapeDtypeStruct((t, STATE), F32),
                   jax.ShapeDtypeStruct((1, STATE), F32), jax.ShapeDtypeStruct((1, STATE), F32),
                   jax.ShapeDtypeStruct((1, inner), F32)],
        scratch_shapes=[pltpu.VMEM((groups * pairs, STATE, STATE), F32)],
        compiler_params=_params(("arbitrary", "arbitrary")),
    )(act, act, act, dtr, dtb, alog, dexp, states, dy)


def _ssd_conv_width(inner, cdim):
    return _divisor(math.gcd(inner, cdim), 2048, 128)


def ssd_conv_fwd(name, zx, cw, cb, inner):
    t, cdim = zx.shape[0], cw.shape[1]
    wc = _ssd_conv_width(inner, cdim)

    def fn(i, j, xt, xp, w, b):
        return [jax.nn.silu(_conv(xt, xp, w, b, i == 0))], []
    ins = [Row(zx, wc, inner // wc), Prev(zx, wc, inner // wc), Full(cw, wc), Full(cb, wc)]
    return rows_call(name, fn, t, _divisor(t, 256, 8), cdim // wc, ins, outs=[(F32, wc)])[0]


def ssd_conv_bwd(name, zx, cw, cb, dact, inner):
    t, cdim = zx.shape[0], cw.shape[1]
    wc = _ssd_conv_width(inner, cdim)
    kk = cw.shape[0]

    def fn(i, j, xt, xp, w, b, da):
        first = i == 0
        pre = _conv(xt, xp, w, b, first)
        _, vjp = jax.vjp(jax.nn.silu, pre)
        dpre, = vjp(da)
        return [dpre], [_conv_wgrad(xt, xp, dpre, kk, first), jnp.sum(dpre, axis=0, keepdims=True)]
    ins = [Row(zx, wc, inner // wc), Prev(zx, wc, inner // wc), Full(cw, wc), Full(cb, wc), Row(dact, wc)]
    return rows_call(name, fn, t, _divisor(t, 256, 8), cdim // wc, ins, outs=[(F32, wc)],
                     accs=[(SUBLANES, wc), (1, wc)])


def f_ssd_post(y, z, nw):
    yz = y * jax.nn.silu(z)
    return yz * lax.rsqrt(jnp.mean(yz * yz, axis=-1, keepdims=True) + EPS) * nw


def ssd_post_fwd(name, y, zx, nw, groups):
    t, inner = y.shape
    gw = inner // groups

    def fn(i, j, yt, zt, nwt):
        return [f_ssd_post(yt, zt, nwt)], []
    return rows_call(name, fn, t, _divisor(t, 1024, 8), groups, [Row(y, gw), Row(zx, gw), Full(nw, gw)],
                     outs=[(BF16, gw)])[0]


def ssd_post_bwd(name, y, zx, nw, dyn, groups):
    t, inner = y.shape
    gw = inner // groups

    def fn(i, j, yt, zt, nwt, dt):
        _, vjp = jax.vjp(f_ssd_post, yt, zt, nwt)
        dy, dz, dnw = vjp(dt)
        return [dy, dz], [dnw]
    return rows_call(name, fn, t, _divisor(t, 512, 8), groups, [Row(y, gw), Row(zx, gw), Full(nw, gw), Row(dyn, gw)],
                     outs=[(F32, gw), (BF16, gw)], accs=[(1, gw)])


def adamw(name, slots, row0, w, m, v, tile_rows):
    n = w.shape[0]
    tr = _divisor(math.gcd(row0, n), tile_rows, 16)
    assert row0 % tr == 0 and n % tr == 0, (name, row0, n, tr)
    off = row0 // tr

    def body(s_ref, w_ref, m_ref, v_ref, g_ref, d_ref, mo_ref, vo_ref):
        g = s_ref[0].astype(F32)
        for k in range(1, N_DEV):
            g = g + s_ref[k].astype(F32)
        mn = ADAM_B1 * m_ref[...] + (1.0 - ADAM_B1) * g
        vn = ADAM_B2 * v_ref[...] + (1.0 - ADAM_B2) * (g * g)
        m_hat = mn / (1.0 - ADAM_B1 ** ADAM_STEP)
        v_hat = vn / (1.0 - ADAM_B2 ** ADAM_STEP)
        g_ref[...] = g
        d_ref[...] = -ADAM_LR * (m_hat / (jnp.sqrt(v_hat) + ADAM_EPS) + ADAM_WD * w_ref[...])
        mo_ref[...] = mn
        vo_ref[...] = vn

    spec = pl.BlockSpec((tr, 128), lambda i: (i, 0))
    return pl.pallas_call(
        body, name=name, grid=(n // tr,),
        in_specs=[pl.BlockSpec((N_DEV, tr, 128), lambda i: (0, off + i, 0)), spec, spec, spec],
        out_specs=[spec] * 4, out_shape=[jax.ShapeDtypeStruct((n, 128), F32)] * 4,
        compiler_params=_params(("parallel",)),
    )(slots, w, m, v)


_HBM = pl.BlockSpec(memory_space=pltpu.HBM)
_MESH = pl.DeviceIdType.MESH


def all_gather(name, shard):
    def body(x_ref, out_ref, send_sems, recv_sems, local_sem):
        x, y, c = lax.axis_index("x"), lax.axis_index("y"), lax.axis_index("c")
        me, sibling = (x, y, c), (x, y, 1 - c)
        chips = [(1 - x, y), (x, 1 - y), (1 - x, 1 - y)]

        def slab(px, py, pc):
            return out_ref.at[4 * px + 2 * py + pc]

        def copy(k, block, to, src=None):
            return pltpu.make_async_remote_copy(
                src_ref=slab(*block) if src is None else src, dst_ref=slab(*block),
                send_sem=send_sems.at[k], recv_sem=recv_sems.at[k], device_id=to, device_id_type=_MESH)

        mine = pltpu.make_async_copy(x_ref, slab(*me), local_sem)
        mine.start()
        first = [copy(0, me, sibling, src=x_ref)]
        first += [copy(1 + j, me, (*chip, c), src=x_ref) for j, chip in enumerate(chips)]
        for cp in first:
            cp.start()
        passed = [copy(4 + j, (*chip, c), sibling) for j, chip in enumerate(chips)]
        for j, chip in enumerate(chips):
            copy(1 + j, (*chip, c), me).wait_recv()
            passed[j].start()
        copy(0, sibling, me).wait_recv()
        for j, chip in enumerate(chips):
            copy(4 + j, (*chip, 1 - c), me).wait_recv()
        for cp in first + passed:
            cp.wait_send()
        mine.wait()

    return pl.pallas_call(
        body, name=name, out_shape=jax.ShapeDtypeStruct((N_DEV,) + shard.shape, shard.dtype),
        in_specs=[_HBM], out_specs=_HBM,
        scratch_shapes=[pltpu.SemaphoreType.DMA((7,)), pltpu.SemaphoreType.DMA((7,)), pltpu.SemaphoreType.DMA],
    )(shard)


def exchange(name, pieces):
    def body(p_ref, out_ref, send_sems, recv_sems, local_sem):
        x, y, c = lax.axis_index("x"), lax.axis_index("y"), lax.axis_index("c")
        me = 4 * x + 2 * y + c
        mine = pltpu.make_async_copy(p_ref.at[me], out_ref.at[me], local_sem)
        mine.start()
        copies = []
        for k in range(N_DEV - 1):
            bx, by, bc = ((k + 1) >> 2) & 1, ((k + 1) >> 1) & 1, (k + 1) & 1
            px, py, pc = (x + bx) % 2, (y + by) % 2, (c + bc) % 2
            copies.append(pltpu.make_async_remote_copy(
                src_ref=p_ref.at[4 * px + 2 * py + pc], dst_ref=out_ref.at[me],
                send_sem=send_sems.at[k], recv_sem=recv_sems.at[k], device_id=(px, py, pc), device_id_type=_MESH))
        for cp in copies:
            cp.start()
        for cp in copies:
            cp.wait_recv()
        for cp in copies:
            cp.wait_send()
        mine.wait()

    return pl.pallas_call(
        body, name=name, out_shape=jax.ShapeDtypeStruct(pieces.shape, pieces.dtype),
        in_specs=[_HBM], out_specs=_HBM,
        scratch_shapes=[pltpu.SemaphoreType.DMA((7,)), pltpu.SemaphoreType.DMA((7,)), pltpu.SemaphoreType.DMA],
    )(pieces)


WEIGHTS = ['norm_mix_w', 'norm_ffn_w', 'norm_final_w', 'ev_w_in', 'lru_conv_w', 'lru_conv_b', 'lru_w_r', 'lru_b_r',
           'lru_w_i', 'lru_b_i', 'lru_lambda', 'hg_lower_bounds', 'hg_norm_w', 'ev_w_out', 'ssd_w_in', 'ssd_conv_w',
           'ssd_conv_b', 'ssd_dt_bias', 'ssd_a_log', 'ssd_d', 'ssd_norm_w', 'ssd_w_out', 'ffn_w_up', 'ffn_conv_w',
           'ffn_conv_b', 'ffn_w_down']
COL_SHARDED = ['ev_w_in', 'ssd_w_in', 'ffn_w_up']
ROW_SHARDED = ['ev_w_out', 'ssd_w_out', 'ffn_w_down']
BIG = ['ev_w_in', 'ev_w_out', 'ssd_w_out', 'ffn_w_up', 'ffn_w_down', 'ssd_w_in']
SMALL_SHARDED = ['lru_conv_w', 'ssd_conv_w', 'ssd_conv_b', 'ssd_norm_w', 'ffn_conv_w']
SMALL = [n for n in WEIGHTS if n not in BIG]
SEG = 16 * 128


def _pad_to(flat, mult):
    extra = (-flat.shape[-1]) % mult
    if extra == 0:
        return flat
    return jnp.pad(flat, [(0, 0)] * (flat.ndim - 1) + [(0, extra)])


def _as_pairs(a):
    return lax.bitcast_convert_type(a, BF16).reshape(a.shape[:-1] + (2 * a.shape[-1],))


def _from_pairs(a):
    return lax.bitcast_convert_type(a.reshape(a.shape[:-1] + (a.shape[-1] // 2, 2)), F32)


def _pack(segments):
    offs, parts, at = [], [], 0
    for s in segments:
        s = _pad_to(s, SEG)
        offs.append(at)
        at += s.shape[-1]
        parts.append(s)
    buf = jnp.concatenate(parts, axis=-1)
    return buf.reshape(buf.shape[:-1] + (at // 128, 128)), offs


def _unshard_cols(g):
    return jnp.transpose(g, (1, 2, 0, 3)).reshape(g.shape[1], g.shape[2], N_DEV * g.shape[3])


def _unshard_rows(g):
    return jnp.transpose(g, (1, 0, 2, 3)).reshape(g.shape[1], N_DEV * g.shape[2], g.shape[3])


def _unshard_last(g):
    g = jnp.moveaxis(g, 0, -2)
    return g.reshape(g.shape[:-2] + (N_DEV * g.shape[-1],))


def _pieces_cols(w):
    return jnp.transpose(w.reshape(w.shape[0], w.shape[1], N_DEV, w.shape[2] // N_DEV), (2, 0, 1, 3))


def _pieces_rows(w):
    return jnp.transpose(w.reshape(w.shape[0], N_DEV, w.shape[1] // N_DEV, w.shape[2]), (1, 0, 2, 3))


def _block_diag(w):
    nb, b, _ = w.shape
    return (w[:, :, None, :] * jnp.eye(nb, dtype=w.dtype)[:, None, :, None]).reshape(nb * b, nb * b)


def _diag_blocks(dense, nb):
    b = dense.shape[0] // nb
    d4 = dense.reshape(nb, b, nb, b)
    return jnp.stack([d4[h, :, h, :] for h in range(nb)])


def kernel(x, norm_mix_w, norm_ffn_w, norm_final_w, ev_w_in, lru_conv_w, lru_conv_b, lru_w_r, lru_b_r, lru_w_i, lru_b_i, lru_lambda, hg_lower_bounds, hg_norm_w, ev_w_out, ssd_w_in, ssd_conv_w, ssd_conv_b, ssd_dt_bias, ssd_a_log, ssd_d, ssd_norm_w, ssd_w_out, ffn_w_up, ffn_conv_w, ffn_conv_b, ffn_w_down, loss_target, m_norm_mix_w, m_norm_ffn_w, m_norm_final_w, m_ev_w_in, m_lru_conv_w, m_lru_conv_b, m_lru_w_r, m_lru_b_r, m_lru_w_i, m_lru_b_i, m_lru_lambda, m_hg_lower_bounds, m_hg_norm_w, m_ev_w_out, m_ssd_w_in, m_ssd_conv_w, m_ssd_conv_b, m_ssd_dt_bias, m_ssd_a_log, m_ssd_d, m_ssd_norm_w, m_ssd_w_out, m_ffn_w_up, m_ffn_conv_w, m_ffn_conv_b, m_ffn_w_down, v_norm_mix_w, v_norm_ffn_w, v_norm_final_w, v_ev_w_in, v_lru_conv_w, v_lru_conv_b, v_lru_w_r, v_lru_b_r, v_lru_w_i, v_lru_b_i, v_lru_lambda, v_hg_lower_bounds, v_hg_norm_w, v_ev_w_out, v_ssd_w_in, v_ssd_conv_w, v_ssd_conv_b, v_ssd_dt_bias, v_ssd_a_log, v_ssd_d, v_ssd_norm_w, v_ssd_w_out, v_ffn_w_up, v_ffn_conv_w, v_ffn_conv_b, v_ffn_w_down):
    given = dict(locals())
    wts = {n: given[n] for n in WEIGHTS}
    mom1 = {n: given["m_" + n] for n in WEIGHTS}
    mom2 = {n: given["v_" + n] for n in WEIGHTS}
    me = 4 * lax.axis_index("x") + 2 * lax.axis_index("y") + lax.axis_index("c")

    depth, d = norm_mix_w.shape
    t = x.shape[1]
    x0 = x.reshape(t, d)
    target = loss_target.reshape(t, d)
    n_even, lru_w = lru_lambda.shape
    hg_w = hg_lower_bounds.shape[1]
    n_odd, heads = ssd_dt_bias.shape
    inner = N_DEV * ssd_norm_w.shape[1]
    cdim = N_DEV * ssd_conv_b.shape[1]
    groups = (cdim - inner) // (2 * STATE)
    assert inner == heads * HEADDIM and heads <= STATE and (inner // groups) % (2 * HEADDIM) == 0

    segs = [wts[n].astype(BF16).reshape(-1) for n in BIG] + [_as_pairs(wts[n].reshape(-1)) for n in SMALL_SHARDED]
    shard_buf, offs = _pack(segs)
    gathered = all_gather("ag_weights", shard_buf).reshape(N_DEV, -1)
    full = {}
    for n, off, mine in zip(BIG + SMALL_SHARDED, offs, segs, strict=True):
        seg = gathered[:, off:off + mine.shape[0]]
        if n in BIG:
            seg = seg.reshape((N_DEV,) + wts[n].shape)
            full[n] = _unshard_cols(seg) if n in COL_SHARDED else _unshard_rows(seg)
        else:
            full[n] = _unshard_last(_from_pairs(seg).reshape((N_DEV,) + wts[n].shape))
    w_zx = full['ssd_w_in'][:, :, :inner + cdim]
    w_dt = jnp.pad(full['ssd_w_in'][:, :, inner + cdim:], ((0, 0), (0, 0), (0, STATE - heads)))
    pad_h = lambda a: jnp.pad(a.reshape(1, heads), ((0, 0), (0, STATE - heads)))
    row = lambda a: a.reshape(1, -1)

    lbs = whole_call("lbs_fwd", lambda hb: [f_lbs(hb)], [hg_lower_bounds], [hg_lower_bounds.shape])[0]

    saved = []
    xcur = x0
    for l in range(depth):
        sv = {'x0': xcur}
        h = rms_fwd(f"rms_mix_fwd", xcur, row(norm_mix_w[l]))
        sv['h'] = h
        if l % 2 == 0:
            e = l // 2
            wr = _block_diag(lru_w_r[e]).astype(BF16)
            wi = _block_diag(lru_w_i[e]).astype(BF16)
            lru_p = (full['lru_conv_w'][e], row(lru_conv_b[e]), wr, row(lru_b_r[e]), wi, row(lru_b_i[e]), row(lru_lambda[e]))
            proj = matmul(f"ev_in", h, full['ev_w_in'][e])
            xc, a, u = lru_pre_fwd(f"lru_pre_fwd", proj, *lru_p)
            hseq = lin_scan(f"lru_scan_fwd", a, u, False)
            ob, states = hg_fwd(f"hg_fwd", proj, row(lbs[e]), lru_w)
            y = even_post_fwd(f"even_post_fwd", proj, hseq, ob, row(hg_norm_w[e]))
            xmid = matmul(f"ev_out", y, full['ev_w_out'][e], res=xcur)
            sv.update(proj=proj, xc=xc, a=a, hseq=hseq, ob=ob, states=states, y=y, lru_p=lru_p)
        else:
            o = l // 2
            ssd_p = (pad_h(ssd_dt_bias[o]), pad_h(ssd_a_log[o]), row(jnp.repeat(ssd_d[o], HEADDIM)))
            zx = matmul(f"ssd_in", h, w_zx[o])
            dtr = matmul(f"ssd_dt", h, w_dt[o])
            act = ssd_conv_fwd(f"ssd_conv_fwd", zx, full['ssd_conv_w'][o], row(full['ssd_conv_b'][o]), inner)
            ys, states = ssd_fwd(f"ssd_fwd", act, dtr, *ssd_p, inner, groups)
            yn = ssd_post_fwd(f"ssd_post_fwd", ys, zx, row(full['ssd_norm_w'][o]), groups)
            xmid = matmul(f"ssd_out", yn, full['ssd_w_out'][o], res=xcur)
            sv.update(zx=zx, dtr=dtr, act=act, ys=ys, states=states, yn=yn, ssd_p=ssd_p)
        h2 = rms_fwd(f"rms_ffn_fwd", xmid, row(norm_ffn_w[l]))
        u0 = matmul(f"ffn_up", h2, full['ffn_w_up'][l])
        actf = ffn_act_fwd(f"ffn_act_fwd", u0, full['ffn_conv_w'][l], row(ffn_conv_b[l]))
        xcur = matmul(f"ffn_down", actf, full['ffn_w_down'][l], res=xmid)
        sv.update(x1=xmid, h2=h2, u0=u0, actf=actf)
        saved.append(sv)

    gcur, d_nfw, loss_row = loss_head("loss_head", xcur, row(norm_final_w), target)
    loss = lax.psum(loss_row[0, 0], ("x", "y", "c"))

    gl = {n: [None] * wts[n].shape[0] for n in WEIGHTS if n != 'norm_final_w'}
    d_lbs = [None] * n_even
    for l in reversed(range(depth)):
        sv = saved[l]
        half = sv['u0'].shape[1] // 2
        fcw = full['ffn_conv_w'][l]
        dact = matmul(f"ffn_down_dx", gcur, full['ffn_w_down'][l], tb=True)
        gl['ffn_w_down'][l] = matmul(f"ffn_down_dw", sv['actf'], gcur, ta=True, out_dtype=BF16)
        dg, dv, dwg, dwv, dbg, dbv = ffn_act_bwd(f"ffn_act_bwd", sv['u0'], fcw, row(ffn_conv_b[l]), dact)
        kf = fcw.shape[0]
        gl['ffn_conv_w'][l] = jnp.concatenate([dwg[:kf], dwv[:kf]], axis=1)
        gl['ffn_conv_b'][l] = jnp.concatenate([dbg, dbv], axis=1)[0]
        wcol = _divisor(half, 512, 128)
        du0 = jnp.concatenate([conv_t_call(f"ffn_convt_g", dg, fcw, 0, wcol),
                               conv_t_call(f"ffn_convt_v", dv, fcw, half // wcol, wcol)], axis=1)
        dh2 = matmul(f"ffn_up_dx", du0, full['ffn_w_up'][l], tb=True)
        gl['ffn_w_up'][l] = matmul(f"ffn_up_dw", sv['h2'], du0, ta=True, out_dtype=BF16)
        gmid, dnw = rms_bwd(f"rms_ffn_bwd", sv['x1'], row(norm_ffn_w[l]), dh2, gcur)
        gl['norm_ffn_w'][l] = dnw[0]
        if l % 2 == 0:
            e = l // 2
            proj, lru_p = sv['proj'], sv['lru_p']
            dy = matmul(f"ev_out_dx", gmid, full['ev_w_out'][e], tb=True)
            gl['ev_w_out'][e] = matmul(f"ev_out_dw", sv['y'], gmid, ta=True, out_dtype=BF16)
            dhs, dga, dob, dgb, dhn = even_post_bwd(f"even_post_bwd", proj, sv['hseq'], sv['ob'], row(hg_norm_w[e]), dy)
            gl['hg_norm_w'][e] = dhn[0]
            dq, df, di, dlb = hg_bwd(f"hg_bwd", proj, row(lbs[e]), sv['states'], dob, lru_w)
            d_lbs[e] = dlb
            lamb = lin_scan(f"lru_scan_bwd", sv['a'], dhs, True)
            dxc, dcw, dcb, dwr, dbr, dwi, dbi, dlam = lru_pre_bwd(f"lru_pre_bwd", proj, *lru_p, sv['xc'], sv['hseq'], lamb)
            nb = lru_w_r.shape[1]
            gl['lru_conv_w'][e], gl['lru_conv_b'][e] = dcw[:lru_p[0].shape[0]], dcb[0]
            gl['lru_w_r'][e], gl['lru_b_r'][e] = _diag_blocks(dwr, nb), dbr[0]
            gl['lru_w_i'][e], gl['lru_b_i'][e] = _diag_blocks(dwi, nb), dbi[0]
            gl['lru_lambda'][e] = dlam[0]
            dxa = conv_t_call(f"lru_convt", dxc, lru_p[0], 0, lru_w)
            dproj = jnp.concatenate([dxa, dga, dq, df, di, dgb], axis=1)
            dh = matmul(f"ev_in_dx", dproj, full['ev_w_in'][e], tb=True)
            gl['ev_w_in'][e] = matmul(f"ev_in_dw", sv['h'], dproj, ta=True, out_dtype=BF16)
        else:
            o = l // 2
            zx, scw = sv['zx'], full['ssd_conv_w'][o]
            dyn = matmul(f"ssd_out_dx", gmid, full['ssd_w_out'][o], tb=True)
            gl['ssd_w_out'][o] = matmul(f"ssd_out_dw", sv['yn'], gmid, ta=True, out_dtype=BF16)
            dys, dz, dnw = ssd_post_bwd(f"ssd_post_bwd", sv['ys'], zx, row(full['ssd_norm_w'][o]), dyn, groups)
            gl['ssd_norm_w'][o] = dnw[0]
            dxs, dbm, dcm, ddtr, ddtb, dalog, ddexp = ssd_bwd(f"ssd_bwd", sv['act'], sv['dtr'], *sv['ssd_p'],
                                                             sv['states'], dys, inner, groups)
            gl['ssd_dt_bias'][o], gl['ssd_a_log'][o] = ddtb[0, :heads], dalog[0, :heads]
            gl['ssd_d'][o] = jnp.sum(ddexp.reshape(heads, HEADDIM), axis=1)
            dact = jnp.concatenate([dxs, dbm, dcm], axis=1)
            dpre, dcw, dcb = ssd_conv_bwd(f"ssd_conv_bwd", zx, scw, row(full['ssd_conv_b'][o]), dact, inner)
            gl['ssd_conv_w'][o], gl['ssd_conv_b'][o] = dcw[:scw.shape[0]], dcb[0]
            dxbc = conv_t_call(f"ssd_convt", dpre, scw, 0, _ssd_conv_width(inner, cdim))
            dzx = jnp.concatenate([dz, dxbc], axis=1)
            dh = matmul(f"ssd_in_dx", dzx, w_zx[o], tb=True)
            dh = matmul(f"ssd_dt_dx", ddtr, w_dt[o], tb=True, res=dh)
            dwzx = matmul(f"ssd_in_dw", sv['h'], dzx, ta=True, out_dtype=BF16)
            dwdt = matmul(f"ssd_dt_dw", sv['h'], ddtr, ta=True, out_dtype=BF16)
            gl['ssd_w_in'][o] = jnp.concatenate([dwzx, dwdt[:, :heads]], axis=1)
        gcur, dnw = rms_bwd(f"rms_mix_bwd", sv['x0'], row(norm_mix_w[l]), dh, gmid)
        gl['norm_mix_w'][l] = dnw[0]

    def lbs_bwd(hb, dl):
        _, vjp = jax.vjp(f_lbs, hb)
        return [vjp(dl)[0]]
    d_hlb = whole_call("lbs_bwd", lbs_bwd, [hg_lower_bounds, jnp.concatenate(d_lbs, axis=0)], [hg_lower_bounds.shape])[0]

    part = {n: jnp.stack(v) for n, v in gl.items() if n != 'hg_lower_bounds'}
    part['hg_lower_bounds'] = d_hlb
    part['norm_final_w'] = d_nfw[0]

    big_pieces = [(_pieces_cols(part[n]) if n in COL_SHARDED else _pieces_rows(part[n])).reshape(N_DEV, -1) for n in BIG]
    small_flat = _as_pairs(jnp.concatenate([part[n].astype(F32).reshape(-1) for n in SMALL]))
    pieces, goffs = _pack(big_pieces + [jnp.broadcast_to(small_flat, (N_DEV,) + small_flat.shape)])
    slots = exchange("rs_grads", pieces)

    out = {}
    for n, off in zip(BIG, goffs[:-1], strict=True):
        shp = wts[n].shape
        flat = lambda a: _pad_to(a.reshape(-1), SEG).reshape(-1, 128)
        res = adamw(f"adamw_{n}", slots, off // 128, flat(wts[n]), flat(mom1[n]), flat(mom2[n]), 1024)
        out[n] = [r.reshape(-1)[:wts[n].size].reshape(shp) for r in res]

    sm = _from_pairs(slots.reshape(N_DEV, -1)[:, goffs[-1]:goffs[-1] + small_flat.shape[0]])
    own, at = [], 0
    for n in SMALL:
        size = part[n].size
        g8 = sm[:, at:at + size].reshape((N_DEV,) + part[n].shape)
        at += size
        if n in SMALL_SHARDED:
            g8 = lax.dynamic_slice_in_dim(g8, me * wts[n].shape[-1], wts[n].shape[-1], axis=g8.ndim - 1)
        own.append(g8.reshape(N_DEV, -1))
    sslots, _ = _pack([jnp.concatenate(own, axis=1)])
    cat = lambda dct: _pad_to(jnp.concatenate([dct[n].reshape(-1) for n in SMALL]), SEG).reshape(-1, 128)
    res = adamw("adamw_small", sslots, 0, cat(wts), cat(mom1), cat(mom2), 1024)
    at = 0
    for n in SMALL:
        out[n] = [r.reshape(-1)[at:at + wts[n].size].reshape(wts[n].shape) for r in res]
        at += wts[n].size

    grad_x = gcur.reshape(x.shape)
    return (loss, grad_x, *[out[n][0] for n in WEIGHTS], *[out[n][1] for n in WEIGHTS],
            *[out[n][2] for n in WEIGHTS], *[out[n][3] for n in WEIGHTS])
```

```python
import functools
import math

import jax
import jax.numpy as jnp
from jax import lax
from jax.experimental import pallas as pl
from jax.experimental.pallas import tpu as pltpu

F32 = jnp.float32
BF16 = jnp.bfloat16
MXU = jnp.bfloat16
HI = lax.Precision.HIGHEST

N_DEV = 8
EPS = 1e-6
LRU_C = 8.0
CHUNK = 64
STATE = 128
ADAM_LR, ADAM_B1, ADAM_B2, ADAM_EPS, ADAM_WD, ADAM_STEP = 0.001, 0.9, 0.999, 1e-08, 0.01, 10

SUBLANES = 8
VMEM_LIMIT = 56 * 1024 * 1024

NN = (((1,), (0,)), ((), ()))
NT = (((1,), (1,)), ((), ()))
TN = (((0,), (0,)), ((), ()))


def _params(sem):
    return pltpu.CompilerParams(dimension_semantics=sem, vmem_limit_bytes=VMEM_LIMIT)


def _divisor(n, target, align):
    if n <= target:
        return n
    best = None
    for d in range(align, target + 1, align):
        if n % d == 0:
            best = d
    assert best is not None, (n, target, align)
    return best


def _mm(a, b, dn):
    return lax.dot_general(a.astype(MXU), b.astype(MXU), dn, preferred_element_type=F32)


@jax.custom_vjp
def mm_nn(a, b):
    return _mm(a, b, NN)


@jax.custom_vjp
def mm_nt(a, b):
    return _mm(a, b, NT)


@jax.custom_vjp
def mm_tn(a, b):
    return _mm(a, b, TN)


mm_nn.defvjp(lambda a, b: (_mm(a, b, NN), (a, b)), lambda r, g: (mm_nt(g, r[1]), mm_tn(r[0], g)))
mm_nt.defvjp(lambda a, b: (_mm(a, b, NT), (a, b)), lambda r, g: (mm_nn(g, r[1]), mm_tn(g, r[0])))
mm_tn.defvjp(lambda a, b: (_mm(a, b, TN), (a, b)), lambda r, g: (mm_nt(r[1], g), mm_nn(r[0], g)))


def dot_hi(a, b, dn=NN):
    return lax.dot_general(a, b, dn, precision=HI, preferred_element_type=F32)


def _iota(shape, dim):
    return lax.broadcasted_iota(jnp.int32, shape, dim)


def _tril(n):
    return (_iota((n, n), 0) >= _iota((n, n), 1)).astype(F32)


def _softplus(x):
    return jnp.maximum(x, 0.0) + jnp.log1p(jnp.exp(-jnp.abs(x)))


def _neg_expm1(x):
    series = -x * (1.0 + x * (0.5 + x * (1.0 / 6.0 + x * (1.0 / 24.0))))
    return jnp.where(x > -0.03, series, 1.0 - jnp.exp(x))


def matmul(name, a, b, *, ta=False, tb=False, res=None, out_dtype=F32, tm=1024, tn=1024, tk=512):
    m, k = (a.shape[1], a.shape[0]) if ta else a.shape
    n = b.shape[0] if tb else b.shape[1]
    assert (b.shape[1] if tb else b.shape[0]) == k, (name, a.shape, b.shape)
    tm, tn, tk = _divisor(m, tm, 128), _divisor(n, tn, 128), _divisor(k, tk, 128)
    nk = k // tk
    dn = (((0 if ta else 1,), (1 if tb else 0,)), ((), ()))

    def body(*refs):
        a_ref, b_ref = refs[0], refs[1]
        r_ref = refs[2] if res is not None else None
        o_ref, acc_ref = refs[-2], refs[-1]
        kk = pl.program_id(2)

        @pl.when(kk == 0)
        def _():
            acc_ref[...] = jnp.zeros_like(acc_ref)

        acc_ref[...] += _mm(a_ref[...], b_ref[...], dn)

        @pl.when(kk == nk - 1)
        def _():
            r = acc_ref[...]
            if r_ref is not None:
                r = r + r_ref[...]
            o_ref[...] = r.astype(o_ref.dtype)

    a_spec = pl.BlockSpec((tk, tm), lambda i, j, kk: (kk, i)) if ta else pl.BlockSpec((tm, tk), lambda i, j, kk: (i, kk))
    b_spec = pl.BlockSpec((tn, tk), lambda i, j, kk: (j, kk)) if tb else pl.BlockSpec((tk, tn), lambda i, j, kk: (kk, j))
    o_spec = pl.BlockSpec((tm, tn), lambda i, j, kk: (i, j))
    ins, specs = [a, b], [a_spec, b_spec]
    if res is not None:
        ins.append(res)
        specs.append(o_spec)
    return pl.pallas_call(
        body, name=name, grid=(m // tm, n // tn, nk), in_specs=specs, out_specs=o_spec,
        out_shape=jax.ShapeDtypeStruct((m, n), out_dtype),
        scratch_shapes=[pltpu.VMEM((tm, tn), F32)],
        compiler_params=_params(("parallel", "parallel", "arbitrary")),
    )(*ins)


def Row(arr, width=None, off=0, var=True):
    return ("row", arr, arr.shape[1] if width is None else width, off, var)


def Prev(arr, width=None, off=0, var=True):
    return ("prev", arr, arr.shape[1] if width is None else width, off, var)


def Next(arr, width=None, off=0, var=True):
    return ("next", arr, arr.shape[1] if width is None else width, off, var)


def Full(arr, width=None, off=0, var=True):
    return ("full", arr, arr.shape[1] if width is None else width, off, var)


def rows_call(name, fn, rows, tile, ncol, ins, outs=(), accs=()):
    nrow = rows // tile
    assert rows % tile == 0 and tile % SUBLANES == 0, (name, rows, tile)
    last8 = rows // SUBLANES - 1
    per8 = tile // SUBLANES

    def spec(kind, arr, width, off, var):
        col = (lambda j: off + j) if var else (lambda j: off)
        if kind == "row":
            return pl.BlockSpec((tile, width), lambda j, i: (i, col(j)))
        if kind == "prev":
            return pl.BlockSpec((SUBLANES, width), lambda j, i: (jnp.maximum(i * per8 - 1, 0), col(j)))
        if kind == "next":
            return pl.BlockSpec((SUBLANES, width), lambda j, i: (jnp.minimum((i + 1) * per8, last8), col(j)))
        return pl.BlockSpec((arr.shape[0], width), lambda j, i: (0, col(j)))

    n_in, n_out = len(ins), len(outs)

    def body(*refs):
        j, i = pl.program_id(0), pl.program_id(1)
        o_tiles, a_tiles = fn(i, j, *[r[...] for r in refs[:n_in]])
        for r, o in zip(refs[n_in:n_in + n_out], o_tiles, strict=True):
            r[...] = o.astype(r.dtype)
        acc_refs = refs[n_in + n_out:]
        if acc_refs:
            @pl.when(i == 0)
            def _():
                for r in acc_refs:
                    r[...] = jnp.zeros_like(r)
            for r, a in zip(acc_refs, a_tiles, strict=True):
                r[...] += a

    out_shape = [jax.ShapeDtypeStruct((rows, w * ncol), dt) for dt, w in outs]
    out_shape += [jax.ShapeDtypeStruct((r, w * ncol), F32) for r, w in accs]
    out_specs = [pl.BlockSpec((tile, w), lambda j, i: (i, j)) for _, w in outs]
    out_specs += [pl.BlockSpec((r, w), lambda j, i: (0, j)) for r, w in accs]
    res = pl.pallas_call(
        body, name=name, grid=(ncol, nrow), in_specs=[spec(*s) for s in ins], out_specs=out_specs,
        out_shape=out_shape, compiler_params=_params(("arbitrary", "arbitrary")),
    )(*[s[1] for s in ins])
    return res


def _shift_down(tile, prev8, s, first):
    if s == 0:
        return tile
    rolled = pltpu.roll(tile, s, 0)
    pr = jnp.where(first, 0.0, pltpu.roll(prev8, s, 0))
    head = jnp.where(_iota(pr.shape, 0) < s, pr, rolled[:SUBLANES])
    return jnp.concatenate([head, rolled[SUBLANES:]], axis=0)


def _shift_up(tile, next8, s, last):
    if s == 0:
        return tile
    t = tile.shape[0]
    rolled = pltpu.roll(tile, t - s, 0)
    nx = jnp.where(last, 0.0, pltpu.roll(next8, SUBLANES - s, 0))
    tail = jnp.where(_iota(nx.shape, 0) >= SUBLANES - s, nx, rolled[t - SUBLANES:])
    return jnp.concatenate([rolled[:t - SUBLANES], tail], axis=0)


def _row(w, k):
    return jnp.sum(jnp.where(_iota(w.shape, 0) == k, w, 0.0), axis=0, keepdims=True)


def _conv(x, prev8, w, b, first):
    kk = w.shape[0]
    y = b + _row(w, kk - 1) * x
    for k in range(kk - 1):
        y = y + _row(w, k) * _shift_down(x, prev8, kk - 1 - k, first)
    return y


def _conv_wgrad(x, prev8, dy, kk, first):
    out = jnp.zeros((SUBLANES, x.shape[1]), F32)
    for k in range(kk):
        r = jnp.sum(dy * _shift_down(x, prev8, kk - 1 - k, first), axis=0, keepdims=True)
        out = out + jnp.where(_iota(out.shape, 0) == k, r, 0.0)
    return out


def _conv_t(dy, next8, w, last):
    kk = w.shape[0]
    dx = _row(w, kk - 1) * dy
    for k in range(kk - 1):
        dx = dx + _row(w, k) * _shift_up(dy, next8, kk - 1 - k, last)
    return dx


def f_rms(x, w):
    return x * lax.rsqrt(jnp.mean(x * x, axis=-1, keepdims=True) + EPS) * w


def rms_fwd(name, x, w):
    def fn(i, j, xt, wt):
        return [f_rms(xt, wt)], []
    return rows_call(name, fn, x.shape[0], _divisor(x.shape[0], 512, 8), 1, [Row(x), Full(w)],
                     outs=[(BF16, x.shape[1])])[0]


def rms_bwd(name, x, w, dh, dres):
    def fn(i, j, xt, wt, dht, drt):
        _, vjp = jax.vjp(f_rms, xt, wt)
        dx, dw = vjp(dht)
        return [drt + dx], [dw]
    d = x.shape[1]
    return rows_call(name, fn, x.shape[0], _divisor(x.shape[0], 256, 8), 1, [Row(x), Full(w), Row(dh), Row(dres)],
                     outs=[(F32, d)], accs=[(1, d)])


def loss_head(name, x, w, target):
    def fn(i, j, xt, wt, tt):
        def f(xx, ww):
            err = f_rms(xx, ww) - tt
            return 0.5 * jnp.mean(err * err, axis=-1, keepdims=True)
        rows, vjp = jax.vjp(f, xt, wt)
        dx, dw = vjp(jnp.ones_like(rows))
        return [dx], [dw, jnp.broadcast_to(jnp.sum(rows, axis=0, keepdims=True), (1, 128))]
    d = x.shape[1]
    return rows_call(name, fn, x.shape[0], _divisor(x.shape[0], 256, 8), 1, [Row(x), Full(w), Row(target)],
                     outs=[(F32, d)], accs=[(1, d), (1, 128)])


def ffn_act_fwd(name, u0, cw, cb):
    t, two_f = u0.shape
    wc = _divisor(two_f // 2, 512, 128)
    nc = two_f // 2 // wc

    def fn(i, j, ug, ugp, uv, uvp, wg, wv, bg, bv):
        first = i == 0
        g = _conv(ug, ugp, wg, bg, first)
        v = _conv(uv, uvp, wv, bv, first)
        return [jax.nn.silu(g) * v], []
    ins = [Row(u0, wc), Prev(u0, wc), Row(u0, wc, nc), Prev(u0, wc, nc),
           Full(cw, wc), Full(cw, wc, nc), Full(cb, wc), Full(cb, wc, nc)]
    return rows_call(name, fn, t, _divisor(t, 1024, 8), nc, ins, outs=[(BF16, wc)])[0]


def ffn_act_bwd(name, u0, cw, cb, dact):
    t, two_f = u0.shape
    wc = _divisor(two_f // 2, 512, 128)
    nc = two_f // 2 // wc
    kk = cw.shape[0]

    def fn(i, j, ug, ugp, uv, uvp, wg, wv, bg, bv, da):
        first = i == 0
        g = _conv(ug, ugp, wg, bg, first)
        v = _conv(uv, uvp, wv, bv, first)
        _, vjp = jax.vjp(lambda gg, vv: jax.nn.silu(gg) * vv, g, v)
        dg, dv = vjp(da)
        accs = [_conv_wgrad(ug, ugp, dg, kk, first), _conv_wgrad(uv, uvp, dv, kk, first),
                jnp.sum(dg, axis=0, keepdims=True), jnp.sum(dv, axis=0, keepdims=True)]
        return [dg, dv], accs
    ins = [Row(u0, wc), Prev(u0, wc), Row(u0, wc, nc), Prev(u0, wc, nc),
           Full(cw, wc), Full(cw, wc, nc), Full(cb, wc), Full(cb, wc, nc), Row(dact, wc)]
    return rows_call(name, fn, t, _divisor(t, 512, 8), nc, ins, outs=[(F32, wc), (F32, wc)],
                     accs=[(SUBLANES, wc), (SUBLANES, wc), (1, wc), (1, wc)])


def conv_t_call(name, dy, cw, col_off, width):
    out_dtype = BF16
    t, c = dy.shape
    nc = c // width
    nrow_tile = _divisor(t, 512, 8)
    last_i = t // nrow_tile - 1

    def fn(i, j, d, dn, w):
        return [_conv_t(d, dn, w, i == last_i)], []
    return rows_call(name, fn, t, nrow_tile, nc, [Row(dy, width), Next(dy, width), Full(cw, width, col_off)],
                     outs=[(out_dtype, width)])[0]


def f_lru_gates(xc, wr, br, wi, bi, lam):
    r = jax.nn.sigmoid(mm_nn(xc, wr) + br)
    gi = jax.nn.sigmoid(mm_nn(xc, wi) + bi)
    log_a = -LRU_C * r * _softplus(-lam)
    a = jnp.exp(log_a)
    u = jnp.sqrt(_neg_expm1(2.0 * log_a)) * (gi * xc)
    return a, u


def lru_pre_fwd(name, proj, cw, cb, wr, br, wi, bi, lam):
    t, w = proj.shape[0], lam.shape[1]

    def fn(i, j, xa, xap, cwt, cbt, wrt, brt, wit, bit, lamt):
        xc = _conv(xa, xap, cwt, cbt, i == 0)
        a, u = f_lru_gates(xc, wrt, brt, wit, bit, lamt)
        return [xc, a, u], []
    ins = [Row(proj, w), Prev(proj, w), Full(cw), Full(cb), Full(wr), Full(br), Full(wi), Full(bi), Full(lam)]
    return rows_call(name, fn, t, _divisor(t, 512, 8), 1, ins, outs=[(F32, w)] * 3)


def lru_pre_bwd(name, proj, cw, cb, wr, br, wi, bi, lam, xc, hseq, lamb):
    t, w = proj.shape[0], lam.shape[1]
    kk = cw.shape[0]

    def fn(i, j, xa, xap, xct, hs, hsp, lb, wrt, brt, wit, bit, lamt):
        first = i == 0
        da = lb * _shift_down(hs, hsp, 1, first)
        _, vjp = jax.vjp(f_lru_gates, xct, wrt.astype(F32), brt, wit.astype(F32), bit, lamt)
        dxc, dwr, dbr, dwi, dbi, dlam = vjp((da, lb))
        accs = [_conv_wgrad(xa, xap, dxc, kk, first), jnp.sum(dxc, axis=0, keepdims=True), dwr, dbr, dwi, dbi, dlam]
        return [dxc], accs
    ins = [Row(proj, w), Prev(proj, w), Row(xc), Row(hseq), Prev(hseq), Row(lamb),
           Full(wr), Full(br), Full(wi), Full(bi), Full(lam)]
    return rows_call(name, fn, t, _divisor(t, 256, 8), 1, ins, outs=[(F32, w)],
                     accs=[(SUBLANES, w), (1, w), (w, w), (1, w), (w, w), (1, w), (1, w)])


def lin_scan(name, a, x, reverse):
    t, c = a.shape
    tile = _divisor(t, 512, 8)
    n = t // tile

    def body(a_ref, x_ref, o_ref, c_ref):
        @pl.when(pl.program_id(0) == 0)
        def _():
            c_ref[...] = jnp.zeros_like(c_ref)

        def step(s, carry):
            r = (tile - 1 - s) if reverse else s
            at, xt = a_ref[pl.ds(r, 1), :], x_ref[pl.ds(r, 1), :]
            o = (xt + carry) if reverse else (at * carry + xt)
            o_ref[pl.ds(r, 1), :] = o
            return (at * o) if reverse else o
        c_ref[...] = lax.fori_loop(0, tile, step, c_ref[...], unroll=8)

    spec = pl.BlockSpec((tile, c), (lambda i: (n - 1 - i, 0)) if reverse else (lambda i: (i, 0)))
    return pl.pallas_call(
        body, name=name, grid=(n,), in_specs=[spec, spec], out_specs=spec,
        out_shape=jax.ShapeDtypeStruct((t, c), F32), scratch_shapes=[pltpu.VMEM((1, c), F32)],
        compiler_params=_params(("arbitrary",)),
    )(a, x)


def _hg_chunk(s, q, fr, v, lb):
    f = lb + (1.0 - lb) * jax.nn.sigmoid(fr)
    k = 1.0 - f
    g = jnp.log(f)
    qs = jax.nn.silu(q) * (STATE ** -0.5)
    cum = dot_hi(_tril(CHUNK), g)
    tot = jnp.sum(g, axis=0, keepdims=True)
    mid = jnp.sum(jnp.where(_iota(g.shape, 0) < CHUNK // 2, g, 0.0), axis=0, keepdims=True)
    scores = mm_nt(qs * jnp.exp(cum - mid), k * jnp.exp(mid - cum))
    scores = jnp.where(_tril(CHUNK) > 0, scores, 0.0)
    o = mm_nn(scores, v) + mm_nn(qs * jnp.exp(cum), s)
    decay = jnp.broadcast_to(jnp.exp(tot), s.shape).T
    s_new = decay * s + mm_tn(k * jnp.exp(tot - cum), v)
    return o, s_new


def _hg_specs(proj, heads, lru_w, hg_w, rows, rev):
    nblk = proj.shape[0] // rows
    blk = (lambda b: nblk - 1 - b) if rev else (lambda b: b)
    base = 2 * lru_w // STATE
    per = hg_w // STATE
    col = [pl.BlockSpec((rows, STATE), functools.partial(lambda h, b, o: (blk(b), o + h), o=base + k * per))
           for k in range(3)]
    return nblk, blk, col


def hg_fwd(name, proj, lbs, lru_w, cb=4):
    t, hg_w = proj.shape[0], lbs.shape[1]
    heads = hg_w // STATE
    cb = min(cb, t // CHUNK)
    rows = cb * CHUNK
    nblk, blk, col = _hg_specs(proj, heads, lru_w, hg_w, rows, False)

    def body(q_ref, f_ref, v_ref, lb_ref, o_ref, s_ref, st):
        @pl.when(pl.program_id(1) == 0)
        def _():
            st[...] = jnp.zeros_like(st)
        s = st[...]
        for c in range(cb):
            sl = slice(c * CHUNK, (c + 1) * CHUNK)
            s_ref[c] = s
            o, s = _hg_chunk(s, q_ref[sl, :], f_ref[sl, :], v_ref[sl, :], lb_ref[...])
            o_ref[sl, :] = o
        st[...] = s

    return pl.pallas_call(
        body, name=name, grid=(heads, nblk),
        in_specs=col + [pl.BlockSpec((1, STATE), lambda h, b: (0, h))],
        out_specs=[pl.BlockSpec((rows, STATE), lambda h, b: (b, h)),
                   pl.BlockSpec((cb, None, STATE, STATE), lambda h, b: (b, h, 0, 0))],
        out_shape=[jax.ShapeDtypeStruct((t, hg_w), F32),
                   jax.ShapeDtypeStruct((t // CHUNK, heads, STATE, STATE), F32)],
        scratch_shapes=[pltpu.VMEM((STATE, STATE), F32)],
        compiler_params=_params(("arbitrary", "arbitrary")),
    )(proj, proj, proj, lbs)


def hg_bwd(name, proj, lbs, states, do, lru_w, cb=4):
    t, hg_w = proj.shape[0], lbs.shape[1]
    heads = hg_w // STATE
    cb = min(cb, t // CHUNK)
    rows = cb * CHUNK
    nblk, blk, col = _hg_specs(proj, heads, lru_w, hg_w, rows, True)

    def body(q_ref, f_ref, v_ref, lb_ref, s_ref, do_ref, dq_ref, df_ref, dv_ref, dlb_ref, dst):
        @pl.when(pl.program_id(1) == 0)
        def _():
            dst[...] = jnp.zeros_like(dst)
            dlb_ref[...] = jnp.zeros_like(dlb_ref)
        ds = dst[...]
        dlb = jnp.zeros((1, STATE), F32)
        for c in reversed(range(cb)):
            sl = slice(c * CHUNK, (c + 1) * CHUNK)
            _, vjp = jax.vjp(_hg_chunk, s_ref[c], q_ref[sl, :], f_ref[sl, :], v_ref[sl, :], lb_ref[...])
            ds, dq, df, dv, dl = vjp((do_ref[sl, :], ds))
            dq_ref[sl, :] = dq.astype(dq_ref.dtype)
            df_ref[sl, :] = df.astype(df_ref.dtype)
            dv_ref[sl, :] = dv.astype(dv_ref.dtype)
            dlb = dlb + dl
        dst[...] = ds
        dlb_ref[...] += dlb

    rspec = pl.BlockSpec((rows, STATE), lambda h, b: (blk(b), h))
    return pl.pallas_call(
        body, name=name, grid=(heads, nblk),
        in_specs=col + [pl.BlockSpec((1, STATE), lambda h, b: (0, h)),
                        pl.BlockSpec((cb, None, STATE, STATE), lambda h, b: (blk(b), h, 0, 0)), rspec],
        out_specs=[rspec, rspec, rspec, pl.BlockSpec((1, STATE), lambda h, b: (0, h))],
        out_shape=[jax.ShapeDtypeStruct((t, hg_w), BF16)] * 3 + [jax.ShapeDtypeStruct((1, hg_w), F32)],
        scratch_shapes=[pltpu.VMEM((STATE, STATE), F32)],
        compiler_params=_params(("arbitrary", "arbitrary")),
    )(proj, proj, proj, lbs, states, do)


def f_even_post(hseq, ga, ob, gb, nw):
    parts = [hseq * jax.nn.gelu(ga)]
    for h in range(ob.shape[1] // STATE):
        o = ob[:, h * STATE:(h + 1) * STATE]
        on = o * lax.rsqrt(jnp.mean(o * o, axis=-1, keepdims=True) + EPS) * nw
        parts.append(on * jax.nn.silu(gb[:, h * STATE:(h + 1) * STATE]))
    return jnp.concatenate(parts, axis=-1)


def _even_post_ins(proj, hseq, ob, nw):
    w, v = hseq.shape[1], ob.shape[1]
    assert w == v
    return [Row(hseq), Row(proj, w, 1), Row(ob), Row(proj, v, (2 * w + 3 * v) // v), Full(nw)]


def even_post_fwd(name, proj, hseq, ob, nw):
    t = proj.shape[0]

    def fn(i, j, hs, ga, o, gb, nwt):
        return [f_even_post(hs, ga, o, gb, nwt)], []
    return rows_call(name, fn, t, _divisor(t, 256, 8), 1, _even_post_ins(proj, hseq, ob, nw),
                     outs=[(BF16, hseq.shape[1] + ob.shape[1])])[0]


def even_post_bwd(name, proj, hseq, ob, nw, dy):
    t, w, v = proj.shape[0], hseq.shape[1], ob.shape[1]

    def fn(i, j, hs, ga, o, gb, nwt, dyt):
        _, vjp = jax.vjp(f_even_post, hs, ga, o, gb, nwt)
        dhs, dga, dob, dgb, dnw = vjp(dyt)
        return [dhs, dga, dob, dgb], [dnw]
    return rows_call(name, fn, t, _divisor(t, 256, 8), 1, _even_post_ins(proj, hseq, ob, nw) + [Row(dy)],
                     outs=[(F32, w), (BF16, w), (F32, v), (BF16, v)], accs=[(1, STATE)])


def f_lbs(hb):
    e = jnp.exp(hb - jnp.max(hb, axis=0, keepdims=True))
    p = e / jnp.sum(e, axis=0, keepdims=True)
    out, run = jnp.zeros_like(p), jnp.zeros_like(p[:1])
    for r in range(hb.shape[0]):
        run = run + _row(p, r)
        out = out + jnp.where(_iota(p.shape, 0) == r, run - _row(p, 0), 0.0)
    return out


def whole_call(name, fn, ins, out_shapes):
    def body(*refs):
        outs = fn(*[r[...] for r in refs[:len(ins)]])
        for r, o in zip(refs[len(ins):], outs, strict=True):
            r[...] = o
    return pl.pallas_call(body, name=name, out_shape=[jax.ShapeDtypeStruct(s, F32) for s in out_shapes])(*ins)


HEADDIM = 64


def f_ssd_prep(dtr, dtb, alog, inner):
    rows = dtr.shape[0]
    dt_all = _softplus(dtr + dtb)
    da_all = dt_all * (-jnp.exp(alog))
    tril = _tril(CHUNK)
    cums = [dot_hi(tril, da_all[c * CHUNK:(c + 1) * CHUNK]) for c in range(rows // CHUNK)]
    cum_all = jnp.concatenate(cums, axis=0) if len(cums) > 1 else cums[0]
    head_of = _iota((STATE, inner), 1) - HEADDIM * _iota((STATE, inner), 0)
    spread = ((head_of >= 0) & (head_of < HEADDIM)).astype(F32)
    even = (_iota((CHUNK, STATE), 1) == 2 * _iota((CHUNK, STATE), 0)).astype(F32)
    odd = (_iota((CHUNK, STATE), 1) == 2 * _iota((CHUNK, STATE), 0) + 1).astype(F32)
    left = _iota((CHUNK, STATE), 1) < HEADDIM
    cumrows = []
    for cm in cums:
        twice = jnp.concatenate([cm, cm], axis=0)
        cumrows.append(jnp.where(left, dot_hi(even, twice, NT), dot_hi(odd, twice, NT)))
    return dot_hi(dt_all, spread), dot_hi(cum_all, spread), tuple(cumrows)


def ssd_prep_fwd(name, dtr, dtb, alog, inner, k=4):
    t = dtr.shape[0]
    k = min(k, t // CHUNK)
    rows = k * CHUNK

    def body(dtr_ref, dtb_ref, alog_ref, dt_ref, cum_ref, cr_ref):
        dt, cum, crs = f_ssd_prep(dtr_ref[...], dtb_ref[...], alog_ref[...], inner)
        dt_ref[...] = dt
        cum_ref[...] = cum
        for c, cr in enumerate(crs):
            cr_ref[c] = cr

    one = pl.BlockSpec((1, STATE), lambda i: (0, 0))
    wide = pl.BlockSpec((rows, inner), lambda i: (i, 0))
    return pl.pallas_call(
        body, name=name, grid=(t // rows,),
        in_specs=[pl.BlockSpec((rows, STATE), lambda i: (i, 0)), one, one],
        out_specs=[wide, wide, pl.BlockSpec((k, CHUNK, STATE), lambda i: (i, 0, 0))],
        out_shape=[jax.ShapeDtypeStruct((t, inner), F32), jax.ShapeDtypeStruct((t, inner), F32),
                   jax.ShapeDtypeStruct((t // CHUNK, CHUNK, STATE), F32)],
        compiler_params=_params(("arbitrary",)),
    )(dtr, dtb, alog)


def ssd_prep_bwd(name, dtr, dtb, alog, ddt, dcum, dcr, inner, k=4):
    t = dtr.shape[0]
    k = min(k, t // CHUNK)
    rows = k * CHUNK

    def body(dtr_ref, dtb_ref, alog_ref, ddt_ref, dcum_ref, dcr_ref, ddtr_ref, ddtb_ref, dalog_ref):
        @pl.when(pl.program_id(0) == 0)
        def _():
            ddtb_ref[...] = jnp.zeros_like(ddtb_ref)
            dalog_ref[...] = jnp.zeros_like(dalog_ref)
        _, vjp = jax.vjp(functools.partial(f_ssd_prep, inner=inner), dtr_ref[...], dtb_ref[...], alog_ref[...])
        ddtr, ddtb, dalog = vjp((ddt_ref[...], dcum_ref[...], tuple(dcr_ref[c] for c in range(k))))
        ddtr_ref[...] = ddtr
        ddtb_ref[...] += ddtb
        dalog_ref[...] += dalog

    one = pl.BlockSpec((1, STATE), lambda i: (0, 0))
    wide = pl.BlockSpec((rows, inner), lambda i: (i, 0))
    tall = pl.BlockSpec((rows, STATE), lambda i: (i, 0))
    return pl.pallas_call(
        body, name=name, grid=(t // rows,),
        in_specs=[tall, one, one, wide, wide, pl.BlockSpec((k, CHUNK, STATE), lambda i: (i, 0, 0))],
        out_specs=[tall, one, one],
        out_shape=[jax.ShapeDtypeStruct((t, STATE), F32), jax.ShapeDtypeStruct((1, STATE), F32),
                   jax.ShapeDtypeStruct((1, STATE), F32)],
        compiler_params=_params(("arbitrary",)),
    )(dtr, dtb, alog, ddt, dcum, dcr)


def _ssd_group(states, x, bm, cm, dt, cum, cumrs, dsk):
    half = _iota((CHUNK, STATE), 1) >= HEADDIM
    pos = _iota((CHUNK, STATE), 1) - jnp.where(half, HEADDIM, 0)
    row = _iota((CHUNK, STATE), 0)
    causal = row >= pos
    cb2 = mm_nt(cm, jnp.concatenate([bm, bm], axis=0))
    ys, new = [], []
    for pp, (s, cumr) in enumerate(zip(states, cumrs, strict=True)):
        lanes = slice(pp * STATE, (pp + 1) * STATE)
        cu, xdt = cum[:, lanes], x[:, lanes] * dt[:, lanes]
        tot = jnp.sum(jnp.where(row == CHUNK - 1, cu, 0.0), axis=0, keepdims=True)
        m = jnp.where(causal, cb2 * jnp.exp(jnp.where(causal, cu - cumr, 0.0)), 0.0)
        x2 = jnp.concatenate([jnp.where(half, 0.0, xdt), jnp.where(half, xdt, 0.0)], axis=0)
        ys.append(mm_nn(m, x2) + mm_nt(cm, s) * jnp.exp(cu) + x[:, lanes] * dsk[:, lanes])
        decay = jnp.broadcast_to(jnp.exp(tot), s.shape).T
        new.append(decay * s + mm_tn(xdt * jnp.exp(tot - cu), bm))
    return jnp.concatenate(ys, axis=1), tuple(new)


def _ssd_specs(act, inner, groups, rev):
    t = act.shape[0]
    nch = t // CHUNK
    ch = (lambda c: nch - 1 - c) if rev else (lambda c: c)
    gw = inner // groups
    wide = pl.BlockSpec((CHUNK, gw), lambda c, g: (ch(c), g))
    specs = [wide,
             pl.BlockSpec((CHUNK, STATE), lambda c, g: (ch(c), inner // STATE + g)),
             pl.BlockSpec((CHUNK, STATE), lambda c, g: (ch(c), inner // STATE + groups + g)),
             wide, wide,
             pl.BlockSpec((None, CHUNK, STATE), lambda c, g: (ch(c), 0, 0)),
             pl.BlockSpec((1, gw), lambda c, g: (0, g))]
    return nch, ch, gw, specs


def ssd_fwd(name, act, dt, cum, cumrow, dexp, inner, groups):
    t = act.shape[0]
    nch, ch, gw, specs = _ssd_specs(act, inner, groups, False)
    pairs = gw // (2 * HEADDIM)

    def body(x_ref, b_ref, c_ref, dt_ref, cum_ref, cr_ref, dsk_ref, y_ref, sv_ref, st):
        c, g = pl.program_id(0), pl.program_id(1)

        @pl.when(c == 0)
        def _():
            for pp in range(pairs):
                st[g * pairs + pp] = jnp.zeros((STATE, STATE), F32)

        states = tuple(st[g * pairs + pp] for pp in range(pairs))
        cumrs = tuple(cr_ref[pl.ds(g * pairs + pp, 1), :] for pp in range(pairs))
        for pp in range(pairs):
            sv_ref[pp] = states[pp]
        y, new = _ssd_group(states, x_ref[...], b_ref[...], c_ref[...], dt_ref[...], cum_ref[...], cumrs, dsk_ref[...])
        y_ref[...] = y
        for pp in range(pairs):
            st[g * pairs + pp] = new[pp]

    return pl.pallas_call(
        body, name=name, grid=(nch, groups), in_specs=specs,
        out_specs=[pl.BlockSpec((CHUNK, gw), lambda c, g: (c, g)),
                   pl.BlockSpec((None, pairs, STATE, STATE), lambda c, g: (c, g, 0, 0))],
        out_shape=[jax.ShapeDtypeStruct((t, inner), F32),
                   jax.ShapeDtypeStruct((nch, groups * pairs, STATE, STATE), F32)],
        scratch_shapes=[pltpu.VMEM((groups * pairs, STATE, STATE), F32)],
        compiler_params=_params(("arbitrary", "arbitrary")),
    )(act, act, act, dt, cum, cumrow, dexp)


def ssd_bwd(name, act, dt, cum, cumrow, dexp, states, dy, inner, groups):
    t = act.shape[0]
    nch, ch, gw, specs = _ssd_specs(act, inner, groups, True)
    pairs = gw // (2 * HEADDIM)

    def body(x_ref, b_ref, c_ref, dt_ref, cum_ref, cr_ref, dsk_ref, sv_ref, dy_ref,
             dx_ref, db_ref, dc_ref, ddt_ref, dcum_ref, dcr_ref, ddsk_ref, dst):
        c, g = pl.program_id(0), pl.program_id(1)

        @pl.when((c == 0) & (g == 0))
        def _():
            ddsk_ref[...] = jnp.zeros_like(ddsk_ref)

        @pl.when(c == 0)
        def _():
            for pp in range(pairs):
                dst[g * pairs + pp] = jnp.zeros((STATE, STATE), F32)

        @pl.when(g == 0)
        def _():
            dcr_ref[...] = jnp.zeros_like(dcr_ref)

        cumrs = tuple(cr_ref[pl.ds(g * pairs + pp, 1), :] for pp in range(pairs))
        _, vjp = jax.vjp(_ssd_group, tuple(sv_ref[pp] for pp in range(pairs)), x_ref[...], b_ref[...], c_ref[...],
                         dt_ref[...], cum_ref[...], cumrs, dsk_ref[...])
        ds, dx, db, dc, ddt, dcum, dcrs, ddsk = vjp((dy_ref[...], tuple(dst[g * pairs + pp] for pp in range(pairs))))
        for pp in range(pairs):
            dst[g * pairs + pp] = ds[pp]
            dcr_ref[pl.ds(g * pairs + pp, 1), :] = dcrs[pp]
        dx_ref[...] = dx
        db_ref[...] = db
        dc_ref[...] = dc
        ddt_ref[...] = ddt
        dcum_ref[...] = dcum
        col = pl.ds(pl.multiple_of(g * gw, STATE), gw)
        ddsk_ref[:, col] = ddsk_ref[:, col] + ddsk

    wide = pl.BlockSpec((CHUNK, gw), lambda c, g: (ch(c), g))
    grp = pl.BlockSpec((CHUNK, STATE), lambda c, g: (ch(c), g))
    return pl.pallas_call(
        body, name=name, grid=(nch, groups),
        in_specs=specs + [pl.BlockSpec((None, pairs, STATE, STATE), lambda c, g: (ch(c), g, 0, 0)), wide],
        out_specs=[wide, grp, grp, wide, wide, pl.BlockSpec((None, CHUNK, STATE), lambda c, g: (ch(c), 0, 0)),
                   pl.BlockSpec((1, inner), lambda c, g: (0, 0))],
        out_shape=[jax.ShapeDtypeStruct((t, inner), F32), jax.ShapeDtypeStruct((t, groups * STATE), F32),
                   jax.ShapeDtypeStruct((t, groups * STATE), F32), jax.ShapeDtypeStruct((t, inner), F32),
                   jax.ShapeDtypeStruct((t, inner), F32), jax.ShapeDtypeStruct((nch, CHUNK, STATE), F32),
                   jax.ShapeDtypeStruct((1, inner), F32)],
        scratch_shapes=[pltpu.VMEM((groups * pairs, STATE, STATE), F32)],
        compiler_params=_params(("arbitrary", "arbitrary")),
    )(act, act, act, dt, cum, cumrow, dexp, states, dy)


def _ssd_conv_width(inner, cdim):
    return _divisor(math.gcd(inner, cdim), 2048, 128)


def ssd_conv_fwd(name, zx, cw, cb, inner):
    t, cdim = zx.shape[0], cw.shape[1]
    wc = _ssd_conv_width(inner, cdim)

    def fn(i, j, xt, xp, w, b):
        return [jax.nn.silu(_conv(xt, xp, w, b, i == 0))], []
    ins = [Row(zx, wc, inner // wc), Prev(zx, wc, inner // wc), Full(cw, wc), Full(cb, wc)]
    return rows_call(name, fn, t, _divisor(t, 256, 8), cdim // wc, ins, outs=[(F32, wc)])[0]


def ssd_conv_bwd(name, zx, cw, cb, dact, inner):
    t, cdim = zx.shape[0], cw.shape[1]
    wc = _ssd_conv_width(inner, cdim)
    kk = cw.shape[0]

    def fn(i, j, xt, xp, w, b, da):
        first = i == 0
        pre = _conv(xt, xp, w, b, first)
        _, vjp = jax.vjp(jax.nn.silu, pre)
        dpre, = vjp(da)
        return [dpre], [_conv_wgrad(xt, xp, dpre, kk, first), jnp.sum(dpre, axis=0, keepdims=True)]
    ins = [Row(zx, wc, inner // wc), Prev(zx, wc, inner // wc), Full(cw, wc), Full(cb, wc), Row(dact, wc)]
    return rows_call(name, fn, t, _divisor(t, 256, 8), cdim // wc, ins, outs=[(F32, wc)],
                     accs=[(SUBLANES, wc), (1, wc)])


def f_ssd_post(y, z, nw):
    yz = y * jax.nn.silu(z)
    return yz * lax.rsqrt(jnp.mean(yz * yz, axis=-1, keepdims=True) + EPS) * nw


def ssd_post_fwd(name, y, zx, nw, groups):
    t, inner = y.shape
    gw = inner // groups

    def fn(i, j, yt, zt, nwt):
        return [f_ssd_post(yt, zt, nwt)], []
    return rows_call(name, fn, t, _divisor(t, 1024, 8), groups, [Row(y, gw), Row(zx, gw), Full(nw, gw)],
                     outs=[(BF16, gw)])[0]


def ssd_post_bwd(name, y, zx, nw, dyn, groups):
    t, inner = y.shape
    gw = inner // groups

    def fn(i, j, yt, zt, nwt, dt):
        _, vjp = jax.vjp(f_ssd_post, yt, zt, nwt)
        dy, dz, dnw = vjp(dt)
        return [dy, dz], [dnw]
    return rows_call(name, fn, t, _divisor(t, 512, 8), groups, [Row(y, gw), Row(zx, gw), Full(nw, gw), Row(dyn, gw)],
                     outs=[(F32, gw), (BF16, gw)], accs=[(1, gw)])


ADAMW_TILE = 256 * 1024


def adamw(name, slots, w, m, v):
    n, c = w.shape
    tr = _divisor(n, max(16, ADAMW_TILE // c // 16 * 16), 16)

    def body(s_ref, w_ref, m_ref, v_ref, g_ref, d_ref, mo_ref, vo_ref):
        g = s_ref[0].astype(F32)
        for k in range(1, N_DEV):
            g = g + s_ref[k].astype(F32)
        mn = ADAM_B1 * m_ref[...] + (1.0 - ADAM_B1) * g
        vn = ADAM_B2 * v_ref[...] + (1.0 - ADAM_B2) * (g * g)
        m_hat = mn / (1.0 - ADAM_B1 ** ADAM_STEP)
        v_hat = vn / (1.0 - ADAM_B2 ** ADAM_STEP)
        g_ref[...] = g
        d_ref[...] = -ADAM_LR * (m_hat / (jnp.sqrt(v_hat) + ADAM_EPS) + ADAM_WD * w_ref[...])
        mo_ref[...] = mn
        vo_ref[...] = vn

    spec = pl.BlockSpec((tr, c), lambda i: (i, 0))
    return pl.pallas_call(
        body, name=name, grid=(n // tr,),
        in_specs=[pl.BlockSpec((N_DEV, tr, c), lambda i: (0, i, 0)), spec, spec, spec],
        out_specs=[spec] * 4, out_shape=[jax.ShapeDtypeStruct((n, c), F32)] * 4,
        compiler_params=_params(("parallel",)),
    )(slots, w, m, v)


_HBM = pl.BlockSpec(memory_space=pltpu.HBM)
_MESH = pl.DeviceIdType.MESH


def all_gather(name, shard):
    def body(x_ref, out_ref, send_sems, recv_sems, local_sem):
        x, y, c = lax.axis_index("x"), lax.axis_index("y"), lax.axis_index("c")
        me, sibling = (x, y, c), (x, y, 1 - c)
        chips = [(1 - x, y), (x, 1 - y), (1 - x, 1 - y)]

        def slab(px, py, pc):
            return out_ref.at[4 * px + 2 * py + pc]

        def copy(k, block, to, src=None):
            return pltpu.make_async_remote_copy(
                src_ref=slab(*block) if src is None else src, dst_ref=slab(*block),
                send_sem=send_sems.at[k], recv_sem=recv_sems.at[k], device_id=to, device_id_type=_MESH)

        mine = pltpu.make_async_copy(x_ref, slab(*me), local_sem)
        mine.start()
        first = [copy(0, me, sibling, src=x_ref)]
        first += [copy(1 + j, me, (*chip, c), src=x_ref) for j, chip in enumerate(chips)]
        for cp in first:
            cp.start()
        passed = [copy(4 + j, (*chip, c), sibling) for j, chip in enumerate(chips)]
        for j, chip in enumerate(chips):
            copy(1 + j, (*chip, c), me).wait_recv()
            passed[j].start()
        copy(0, sibling, me).wait_recv()
        for j, chip in enumerate(chips):
            copy(4 + j, (*chip, 1 - c), me).wait_recv()
        for cp in first + passed:
            cp.wait_send()
        mine.wait()

    return pl.pallas_call(
        body, name=name, out_shape=jax.ShapeDtypeStruct((N_DEV,) + shard.shape, shard.dtype),
        in_specs=[_HBM], out_specs=_HBM,
        scratch_shapes=[pltpu.SemaphoreType.DMA((7,)), pltpu.SemaphoreType.DMA((7,)), pltpu.SemaphoreType.DMA],
    )(shard)


def exchange(name, pieces):
    def body(p_ref, out_ref, send_sems, recv_sems, local_sem):
        x, y, c = lax.axis_index("x"), lax.axis_index("y"), lax.axis_index("c")
        me = 4 * x + 2 * y + c
        mine = pltpu.make_async_copy(p_ref.at[me], out_ref.at[me], local_sem)
        mine.start()
        copies = []
        for k in range(N_DEV - 1):
            bx, by, bc = ((k + 1) >> 2) & 1, ((k + 1) >> 1) & 1, (k + 1) & 1
            px, py, pc = (x + bx) % 2, (y + by) % 2, (c + bc) % 2
            copies.append(pltpu.make_async_remote_copy(
                src_ref=p_ref.at[4 * px + 2 * py + pc], dst_ref=out_ref.at[me],
                send_sem=send_sems.at[k], recv_sem=recv_sems.at[k], device_id=(px, py, pc), device_id_type=_MESH))
        for cp in copies:
            cp.start()
        for cp in copies:
            cp.wait_recv()
        for cp in copies:
            cp.wait_send()
        mine.wait()

    return pl.pallas_call(
        body, name=name, out_shape=jax.ShapeDtypeStruct(pieces.shape, pieces.dtype),
        in_specs=[_HBM], out_specs=_HBM,
        scratch_shapes=[pltpu.SemaphoreType.DMA((7,)), pltpu.SemaphoreType.DMA((7,)), pltpu.SemaphoreType.DMA],
    )(pieces)


WEIGHTS = ['norm_mix_w', 'norm_ffn_w', 'norm_final_w', 'ev_w_in', 'lru_conv_w', 'lru_conv_b', 'lru_w_r', 'lru_b_r',
           'lru_w_i', 'lru_b_i', 'lru_lambda', 'hg_lower_bounds', 'hg_norm_w', 'ev_w_out', 'ssd_w_in', 'ssd_conv_w',
           'ssd_conv_b', 'ssd_dt_bias', 'ssd_a_log', 'ssd_d', 'ssd_norm_w', 'ssd_w_out', 'ffn_w_up', 'ffn_conv_w',
           'ffn_conv_b', 'ffn_w_down']
COL_SHARDED = ['ev_w_in', 'ssd_w_in', 'ffn_w_up']
ROW_SHARDED = ['ev_w_out', 'ssd_w_out', 'ffn_w_down']
BIG = ['ev_w_in', 'ev_w_out', 'ssd_w_out', 'ffn_w_up', 'ffn_w_down', 'ssd_w_in']
SMALL_SHARDED = ['lru_conv_w', 'ssd_conv_w', 'ssd_conv_b', 'ssd_norm_w', 'ffn_conv_w']
SMALL = [n for n in WEIGHTS if n not in BIG]
SEG = 16 * 128


def _pad_to(flat, mult):
    extra = (-flat.shape[-1]) % mult
    if extra == 0:
        return flat
    return jnp.pad(flat, [(0, 0)] * (flat.ndim - 1) + [(0, extra)])


def _pack(segments):
    offs, parts, at = [], [], 0
    for s in segments:
        s = _pad_to(s, SEG)
        offs.append(at)
        at += s.shape[-1]
        parts.append(s)
    buf = jnp.concatenate(parts, axis=-1)
    return buf.reshape(buf.shape[:-1] + (at // 128, 128)), offs


def _unshard_cols(g):
    return jnp.transpose(g, (1, 2, 0, 3)).reshape(g.shape[1], g.shape[2], N_DEV * g.shape[3])


def _unshard_rows(g):
    return jnp.transpose(g, (1, 0, 2, 3)).reshape(g.shape[1], N_DEV * g.shape[2], g.shape[3])


def _unshard_last(g):
    g = jnp.moveaxis(g, 0, -2)
    return g.reshape(g.shape[:-2] + (N_DEV * g.shape[-1],))


def _pieces_cols(w):
    return jnp.transpose(w.reshape(w.shape[0], w.shape[1], N_DEV, w.shape[2] // N_DEV), (2, 0, 1, 3))


def _pieces_rows(w):
    return jnp.transpose(w.reshape(w.shape[0], N_DEV, w.shape[1] // N_DEV, w.shape[2]), (1, 0, 2, 3))


def _block_diag(w):
    nb, b, _ = w.shape
    return (w[:, :, None, :] * jnp.eye(nb, dtype=w.dtype)[:, None, :, None]).reshape(nb * b, nb * b)


def _diag_blocks(dense, nb):
    b = dense.shape[0] // nb
    d4 = dense.reshape(nb, b, nb, b)
    return jnp.stack([d4[h, :, h, :] for h in range(nb)])


def kernel(x, norm_mix_w, norm_ffn_w, norm_final_w, ev_w_in, lru_conv_w, lru_conv_b, lru_w_r, lru_b_r, lru_w_i, lru_b_i, lru_lambda, hg_lower_bounds, hg_norm_w, ev_w_out, ssd_w_in, ssd_conv_w, ssd_conv_b, ssd_dt_bias, ssd_a_log, ssd_d, ssd_norm_w, ssd_w_out, ffn_w_up, ffn_conv_w, ffn_conv_b, ffn_w_down, loss_target, m_norm_mix_w, m_norm_ffn_w, m_norm_final_w, m_ev_w_in, m_lru_conv_w, m_lru_conv_b, m_lru_w_r, m_lru_b_r, m_lru_w_i, m_lru_b_i, m_lru_lambda, m_hg_lower_bounds, m_hg_norm_w, m_ev_w_out, m_ssd_w_in, m_ssd_conv_w, m_ssd_conv_b, m_ssd_dt_bias, m_ssd_a_log, m_ssd_d, m_ssd_norm_w, m_ssd_w_out, m_ffn_w_up, m_ffn_conv_w, m_ffn_conv_b, m_ffn_w_down, v_norm_mix_w, v_norm_ffn_w, v_norm_final_w, v_ev_w_in, v_lru_conv_w, v_lru_conv_b, v_lru_w_r, v_lru_b_r, v_lru_w_i, v_lru_b_i, v_lru_lambda, v_hg_lower_bounds, v_hg_norm_w, v_ev_w_out, v_ssd_w_in, v_ssd_conv_w, v_ssd_conv_b, v_ssd_dt_bias, v_ssd_a_log, v_ssd_d, v_ssd_norm_w, v_ssd_w_out, v_ffn_w_up, v_ffn_conv_w, v_ffn_conv_b, v_ffn_w_down):
    given = dict(locals())
    wts = {n: given[n] for n in WEIGHTS}
    mom1 = {n: given["m_" + n] for n in WEIGHTS}
    mom2 = {n: given["v_" + n] for n in WEIGHTS}
    me = 4 * lax.axis_index("x") + 2 * lax.axis_index("y") + lax.axis_index("c")

    depth, d = norm_mix_w.shape
    t = x.shape[1]
    x0 = x.reshape(t, d)
    target = loss_target.reshape(t, d)
    n_even, lru_w = lru_lambda.shape
    hg_w = hg_lower_bounds.shape[1]
    n_odd, heads = ssd_dt_bias.shape
    inner = N_DEV * ssd_norm_w.shape[1]
    cdim = N_DEV * ssd_conv_b.shape[1]
    groups = (cdim - inner) // (2 * STATE)
    assert inner == heads * HEADDIM and heads <= STATE and (inner // groups) % (2 * HEADDIM) == 0

    full = {}
    for n in BIG:
        g8 = all_gather(f"ag_{n}", wts[n].astype(BF16))
        full[n] = _unshard_cols(g8) if n in COL_SHARDED else _unshard_rows(g8)
    shard_buf, offs = _pack([wts[n].reshape(-1) for n in SMALL_SHARDED])
    gathered = all_gather("ag_small", shard_buf).reshape(N_DEV, -1)
    for n, off in zip(SMALL_SHARDED, offs, strict=True):
        full[n] = _unshard_last(gathered[:, off:off + wts[n].size].reshape((N_DEV,) + wts[n].shape))
    w_zx = full['ssd_w_in'][:, :, :inner + cdim]
    w_dt = jnp.pad(full['ssd_w_in'][:, :, inner + cdim:], ((0, 0), (0, 0), (0, STATE - heads)))
    pad_h = lambda a: jnp.pad(a.reshape(1, heads), ((0, 0), (0, STATE - heads)))
    row = lambda a: a.reshape(1, -1)

    lbs = whole_call("lbs_fwd", lambda hb: [f_lbs(hb)], [hg_lower_bounds], [hg_lower_bounds.shape])[0]

    saved = []
    xcur = x0
    for l in range(depth):
        sv = {'x0': xcur}
        h = rms_fwd(f"rms_mix_fwd", xcur, row(norm_mix_w[l]))
        sv['h'] = h
        if l % 2 == 0:
            e = l // 2
            wr = _block_diag(lru_w_r[e]).astype(BF16)
            wi = _block_diag(lru_w_i[e]).astype(BF16)
            lru_p = (full['lru_conv_w'][e], row(lru_conv_b[e]), wr, row(lru_b_r[e]), wi, row(lru_b_i[e]), row(lru_lambda[e]))
            proj = matmul(f"ev_in", h, full['ev_w_in'][e])
            xc, a, u = lru_pre_fwd(f"lru_pre_fwd", proj, *lru_p)
            hseq = lin_scan(f"lru_scan_fwd", a, u, False)
            ob, states = hg_fwd(f"hg_fwd", proj, row(lbs[e]), lru_w)
            y = even_post_fwd(f"even_post_fwd", proj, hseq, ob, row(hg_norm_w[e]))
            xmid = matmul(f"ev_out", y, full['ev_w_out'][e], res=xcur)
            sv.update(proj=proj, xc=xc, a=a, hseq=hseq, ob=ob, states=states, y=y, lru_p=lru_p)
        else:
            o = l // 2
            ssd_p = (pad_h(ssd_dt_bias[o]), pad_h(ssd_a_log[o]), row(jnp.repeat(ssd_d[o], HEADDIM)))
            zx = matmul(f"ssd_in", h, w_zx[o])
            dtr = matmul(f"ssd_dt", h, w_dt[o])
            act = ssd_conv_fwd(f"ssd_conv_fwd", zx, full['ssd_conv_w'][o], row(full['ssd_conv_b'][o]), inner)
            prep = ssd_prep_fwd("ssd_prep_fwd", dtr, ssd_p[0], ssd_p[1], inner)
            ys, states = ssd_fwd(f"ssd_fwd", act, *prep, ssd_p[2], inner, groups)
            sv['prep'] = prep
            yn = ssd_post_fwd(f"ssd_post_fwd", ys, zx, row(full['ssd_norm_w'][o]), groups)
            xmid = matmul(f"ssd_out", yn, full['ssd_w_out'][o], res=xcur)
            sv.update(zx=zx, dtr=dtr, act=act, ys=ys, states=states, yn=yn, ssd_p=ssd_p)
        h2 = rms_fwd(f"rms_ffn_fwd", xmid, row(norm_ffn_w[l]))
        u0 = matmul(f"ffn_up", h2, full['ffn_w_up'][l])
        actf = ffn_act_fwd(f"ffn_act_fwd", u0, full['ffn_conv_w'][l], row(ffn_conv_b[l]))
        xcur = matmul(f"ffn_down", actf, full['ffn_w_down'][l], res=xmid)
        sv.update(x1=xmid, h2=h2, u0=u0, actf=actf)
        saved.append(sv)

    gcur, d_nfw, loss_row = loss_head("loss_head", xcur, row(norm_final_w), target)
    loss = lax.psum(loss_row[0, 0], ("x", "y", "c"))

    gl = {n: [None] * wts[n].shape[0] for n in WEIGHTS if n != 'norm_final_w'}
    d_lbs = [None] * n_even
    for l in reversed(range(depth)):
        sv = saved[l]
        half = sv['u0'].shape[1] // 2
        fcw = full['ffn_conv_w'][l]
        dact = matmul(f"ffn_down_dx", gcur, full['ffn_w_down'][l], tb=True)
        gl['ffn_w_down'][l] = matmul(f"ffn_down_dw", sv['actf'], gcur, ta=True, out_dtype=BF16)
        dg, dv, dwg, dwv, dbg, dbv = ffn_act_bwd(f"ffn_act_bwd", sv['u0'], fcw, row(ffn_conv_b[l]), dact)
        kf = fcw.shape[0]
        gl['ffn_conv_w'][l] = jnp.concatenate([dwg[:kf], dwv[:kf]], axis=1)
        gl['ffn_conv_b'][l] = jnp.concatenate([dbg, dbv], axis=1)[0]
        wcol = _divisor(half, 512, 128)
        du0 = jnp.concatenate([conv_t_call(f"ffn_convt_g", dg, fcw, 0, wcol),
                               conv_t_call(f"ffn_convt_v", dv, fcw, half // wcol, wcol)], axis=1)
        dh2 = matmul(f"ffn_up_dx", du0, full['ffn_w_up'][l], tb=True)
        gl['ffn_w_up'][l] = matmul(f"ffn_up_dw", sv['h2'], du0, ta=True, out_dtype=BF16)
        gmid, dnw = rms_bwd(f"rms_ffn_bwd", sv['x1'], row(norm_ffn_w[l]), dh2, gcur)
        gl['norm_ffn_w'][l] = dnw[0]
        if l % 2 == 0:
            e = l // 2
            proj, lru_p = sv['proj'], sv['lru_p']
            dy = matmul(f"ev_out_dx", gmid, full['ev_w_out'][e], tb=True)
            gl['ev_w_out'][e] = matmul(f"ev_out_dw", sv['y'], gmid, ta=True, out_dtype=BF16)
            dhs, dga, dob, dgb, dhn = even_post_bwd(f"even_post_bwd", proj, sv['hseq'], sv['ob'], row(hg_norm_w[e]), dy)
            gl['hg_norm_w'][e] = dhn[0]
            dq, df, di, dlb = hg_bwd(f"hg_bwd", proj, row(lbs[e]), sv['states'], dob, lru_w)
            d_lbs[e] = dlb
            lamb = lin_scan(f"lru_scan_bwd", sv['a'], dhs, True)
            dxc, dcw, dcb, dwr, dbr, dwi, dbi, dlam = lru_pre_bwd(f"lru_pre_bwd", proj, *lru_p, sv['xc'], sv['hseq'], lamb)
            nb = lru_w_r.shape[1]
            gl['lru_conv_w'][e], gl['lru_conv_b'][e] = dcw[:lru_p[0].shape[0]], dcb[0]
            gl['lru_w_r'][e], gl['lru_b_r'][e] = _diag_blocks(dwr, nb), dbr[0]
            gl['lru_w_i'][e], gl['lru_b_i'][e] = _diag_blocks(dwi, nb), dbi[0]
            gl['lru_lambda'][e] = dlam[0]
            dxa = conv_t_call(f"lru_convt", dxc, lru_p[0], 0, lru_w)
            dproj = jnp.concatenate([dxa, dga, dq, df, di, dgb], axis=1)
            dh = matmul(f"ev_in_dx", dproj, full['ev_w_in'][e], tb=True)
            gl['ev_w_in'][e] = matmul(f"ev_in_dw", sv['h'], dproj, ta=True, out_dtype=BF16)
        else:
            o = l // 2
            zx, scw = sv['zx'], full['ssd_conv_w'][o]
            dyn = matmul(f"ssd_out_dx", gmid, full['ssd_w_out'][o], tb=True)
            gl['ssd_w_out'][o] = matmul(f"ssd_out_dw", sv['yn'], gmid, ta=True, out_dtype=BF16)
            dys, dz, dnw = ssd_post_bwd(f"ssd_post_bwd", sv['ys'], zx, row(full['ssd_norm_w'][o]), dyn, groups)
            gl['ssd_norm_w'][o] = dnw[0]
            dxs, dbm, dcm, ddt, dcum, dcr, ddexp = ssd_bwd(f"ssd_bwd", sv['act'], *sv['prep'], sv['ssd_p'][2],
                                                          sv['states'], dys, inner, groups)
            ddtr, ddtb, dalog = ssd_prep_bwd("ssd_prep_bwd", sv['dtr'], sv['ssd_p'][0], sv['ssd_p'][1], ddt, dcum, dcr, inner)
            gl['ssd_dt_bias'][o], gl['ssd_a_log'][o] = ddtb[0, :heads], dalog[0, :heads]
            gl['ssd_d'][o] = jnp.sum(ddexp.reshape(heads, HEADDIM), axis=1)
            dact = jnp.concatenate([dxs, dbm, dcm], axis=1)
            dpre, dcw, dcb = ssd_conv_bwd(f"ssd_conv_bwd", zx, scw, row(full['ssd_conv_b'][o]), dact, inner)
            gl['ssd_conv_w'][o], gl['ssd_conv_b'][o] = dcw[:scw.shape[0]], dcb[0]
            dxbc = conv_t_call(f"ssd_convt", dpre, scw, 0, _ssd_conv_width(inner, cdim))
            dzx = jnp.concatenate([dz, dxbc], axis=1)
            dh = matmul(f"ssd_in_dx", dzx, w_zx[o], tb=True)
            dh = matmul(f"ssd_dt_dx", ddtr, w_dt[o], tb=True, res=dh)
            dwzx = matmul(f"ssd_in_dw", sv['h'], dzx, ta=True, out_dtype=BF16)
            dwdt = matmul(f"ssd_dt_dw", sv['h'], ddtr, ta=True, out_dtype=BF16)
            gl['ssd_w_in'][o] = jnp.concatenate([dwzx, dwdt[:, :heads]], axis=1)
        gcur, dnw = rms_bwd(f"rms_mix_bwd", sv['x0'], row(norm_mix_w[l]), dh, gmid)
        gl['norm_mix_w'][l] = dnw[0]

    def lbs_bwd(hb, dl):
        _, vjp = jax.vjp(f_lbs, hb)
        return [vjp(dl)[0]]
    d_hlb = whole_call("lbs_bwd", lbs_bwd, [hg_lower_bounds, jnp.concatenate(d_lbs, axis=0)], [hg_lower_bounds.shape])[0]

    part = {n: jnp.stack(v) for n, v in gl.items() if n != 'hg_lower_bounds'}
    part['hg_lower_bounds'] = d_hlb
    part['norm_final_w'] = d_nfw[0]

    out = {}
    for n in BIG:
        shp = wts[n].shape
        slots = exchange(f"rs_{n}", _pieces_cols(part[n]) if n in COL_SHARDED else _pieces_rows(part[n]))
        two_d = lambda a: a.reshape(shp[0] * shp[1], shp[2])
        res = adamw(f"adamw_{n}", slots.reshape(N_DEV, shp[0] * shp[1], shp[2]), two_d(wts[n]), two_d(mom1[n]), two_d(mom2[n]))
        out[n] = [r.reshape(shp) for r in res]

    small_buf, _ = _pack([jnp.concatenate([part[n].astype(F32).reshape(-1) for n in SMALL])])
    sm = all_gather("ag_small_grads", small_buf).reshape(N_DEV, -1)
    own, at = [], 0
    for n in SMALL:
        size = part[n].size
        g8 = sm[:, at:at + size].reshape((N_DEV,) + part[n].shape)
        at += size
        if n in SMALL_SHARDED:
            g8 = lax.dynamic_slice_in_dim(g8, me * wts[n].shape[-1], wts[n].shape[-1], axis=g8.ndim - 1)
        own.append(g8.reshape(N_DEV, -1))
    sslots, _ = _pack([jnp.concatenate(own, axis=1)])
    cat = lambda dct: _pad_to(jnp.concatenate([dct[n].reshape(-1) for n in SMALL]), SEG).reshape(-1, 128)
    res = adamw("adamw_small", sslots, cat(wts), cat(mom1), cat(mom2))
    at = 0
    for n in SMALL:
        out[n] = [r.reshape(-1)[at:at + wts[n].size].reshape(wts[n].shape) for r in res]
        at += wts[n].size

    grad_x = gcur.reshape(x.shape)
    return (loss, grad_x, *[out[n][0] for n in WEIGHTS], *[out[n][1] for n in WEIGHTS],
            *[out[n][2] for n in WEIGHTS], *[out[n][3] for n in WEIGHTS])
```

```python
import functools
import math

import jax
import jax.numpy as jnp
from jax import lax
from jax.experimental import pallas as pl
from jax.experimental.pallas import tpu as pltpu

F32 = jnp.float32
BF16 = jnp.bfloat16
MXU = jnp.bfloat16
HI = lax.Precision.HIGHEST

N_DEV = 8
EPS = 1e-6
LRU_C = 8.0
CHUNK = 64
STATE = 128
ADAM_LR, ADAM_B1, ADAM_B2, ADAM_EPS, ADAM_WD, ADAM_STEP = 0.001, 0.9, 0.999, 1e-08, 0.01, 10

SUBLANES = 8
VMEM_LIMIT = 56 * 1024 * 1024

NN = (((1,), (0,)), ((), ()))
NT = (((1,), (1,)), ((), ()))
TN = (((0,), (0,)), ((), ()))


def _params(sem):
    return pltpu.CompilerParams(dimension_semantics=sem, vmem_limit_bytes=VMEM_LIMIT)


def _divisor(n, target, align):
    if n <= target:
        return n
    best = None
    for d in range(align, target + 1, align):
        if n % d == 0:
            best = d
    assert best is not None, (n, target, align)
    return best


def _mm(a, b, dn):
    return lax.dot_general(a.astype(MXU), b.astype(MXU), dn, preferred_element_type=F32)


@jax.custom_vjp
def mm_nn(a, b):
    return _mm(a, b, NN)


@jax.custom_vjp
def mm_nt(a, b):
    return _mm(a, b, NT)


@jax.custom_vjp
def mm_tn(a, b):
    return _mm(a, b, TN)


mm_nn.defvjp(lambda a, b: (_mm(a, b, NN), (a, b)), lambda r, g: (mm_nt(g, r[1]), mm_tn(r[0], g)))
mm_nt.defvjp(lambda a, b: (_mm(a, b, NT), (a, b)), lambda r, g: (mm_nn(g, r[1]), mm_tn(g, r[0])))
mm_tn.defvjp(lambda a, b: (_mm(a, b, TN), (a, b)), lambda r, g: (mm_nt(r[1], g), mm_nn(r[0], g)))


def dot_hi(a, b, dn=NN):
    return lax.dot_general(a, b, dn, precision=HI, preferred_element_type=F32)


def _iota(shape, dim):
    return lax.broadcasted_iota(jnp.int32, shape, dim)


def _tril(n):
    return (_iota((n, n), 0) >= _iota((n, n), 1)).astype(F32)


def _softplus(x):
    return jnp.maximum(x, 0.0) + jnp.log1p(jnp.exp(-jnp.abs(x)))


def _neg_expm1(x):
    series = -x * (1.0 + x * (0.5 + x * (1.0 / 6.0 + x * (1.0 / 24.0))))
    return jnp.where(x > -0.03, series, 1.0 - jnp.exp(x))


_HBM = pl.BlockSpec(memory_space=pltpu.HBM)
_MESH = pl.DeviceIdType.MESH


def _peer_copies(kind, src_ref, dst_ref, send_sems, recv_sems, local_sem):
    x, y, c = lax.axis_index("x"), lax.axis_index("y"), lax.axis_index("c")
    me = 4 * x + 2 * y + c
    pick = (lambda p: src_ref) if kind == "gather" else (lambda p: src_ref.at[p])
    local = pltpu.make_async_copy(pick(me), dst_ref.at[me], local_sem)
    remote = []
    for k in range(N_DEV - 1):
        px, py, pc = (x + ((k + 1) >> 2 & 1)) % 2, (y + ((k + 1) >> 1 & 1)) % 2, (c + ((k + 1) & 1)) % 2
        remote.append(pltpu.make_async_remote_copy(
            src_ref=pick(4 * px + 2 * py + pc), dst_ref=dst_ref.at[me], send_sem=send_sems.at[k],
            recv_sem=recv_sems.at[k], device_id=(px, py, pc), device_id_type=_MESH))
    return local, remote


_PEER_SEMS = [pltpu.SemaphoreType.DMA((N_DEV - 1,)), pltpu.SemaphoreType.DMA((N_DEV - 1,)), pltpu.SemaphoreType.DMA]


def matmul(name, a, b, *, ta=False, tb=False, res=None, out_dtype=F32, tm=1024, tn=1024, tk=2048, ride=None):
    m, k = (a.shape[1], a.shape[0]) if ta else a.shape
    n = b.shape[0] if tb else b.shape[1]
    assert (b.shape[1] if tb else b.shape[0]) == k, (name, a.shape, b.shape)
    tm, tn, tk = _divisor(m, tm, 128), _divisor(n, tn, 128), _divisor(k, tk, 128)
    ni, nj, nk = m // tm, n // tn, k // tk
    dn = (((0 if ta else 1,), (1 if tb else 0,)), ((), ()))
    n_in = 2 + (res is not None) + (ride is not None)
    n_out = 1 + (ride is not None)

    def body(*refs):
        a_ref, b_ref = refs[0], refs[1]
        r_ref = refs[2] if res is not None else None
        o_ref = refs[n_in]
        scratch = refs[n_in + n_out:]
        i, j, kk = pl.program_id(0), pl.program_id(1), pl.program_id(2)

        if ride is not None:
            peers = functools.partial(_peer_copies, ride[0], refs[n_in - 1], refs[n_in + 1], *scratch[-3:])

            @pl.when((i == 0) & (j == 0) & (kk == 0))
            def _():
                local, remote = peers()
                local.start()
                for cp in remote:
                    cp.start()

        def finish(r):
            if r_ref is not None:
                r = r + r_ref[...]
            o_ref[...] = r.astype(o_ref.dtype)

        if nk == 1:
            finish(_mm(a_ref[...], b_ref[...], dn))
        else:
            acc_ref = scratch[0]

            @pl.when(kk == 0)
            def _():
                acc_ref[...] = jnp.zeros_like(acc_ref)

            acc_ref[...] += _mm(a_ref[...], b_ref[...], dn)

            @pl.when(kk == nk - 1)
            def _():
                finish(acc_ref[...])

        if ride is not None:
            @pl.when((i == ni - 1) & (j == nj - 1) & (kk == nk - 1))
            def _():
                local, remote = peers()
                for cp in remote:
                    cp.wait_recv()
                for cp in remote:
                    cp.wait_send()
                local.wait()

    a_spec = pl.BlockSpec((tk, tm), lambda i, j, kk: (kk, i)) if ta else pl.BlockSpec((tm, tk), lambda i, j, kk: (i, kk))
    b_spec = pl.BlockSpec((tn, tk), lambda i, j, kk: (j, kk)) if tb else pl.BlockSpec((tk, tn), lambda i, j, kk: (kk, j))
    o_spec = pl.BlockSpec((tm, tn), lambda i, j, kk: (i, j))
    ins, specs = [a, b], [a_spec, b_spec]
    out_specs, out_shape = [o_spec], [jax.ShapeDtypeStruct((m, n), out_dtype)]
    scratch_shapes = [pltpu.VMEM((tm, tn), F32)] if nk > 1 else []
    if res is not None:
        ins.append(res)
        specs.append(o_spec)
    if ride is not None:
        kind, src = ride
        ins.append(src)
        specs.append(_HBM)
        out_specs.append(_HBM)
        out_shape.append(jax.ShapeDtypeStruct(src.shape if kind == "exchange" else (N_DEV,) + src.shape, src.dtype))
        scratch_shapes += _PEER_SEMS
    outs = pl.pallas_call(
        body, name=name, grid=(ni, nj, nk), in_specs=specs, out_specs=out_specs, out_shape=out_shape,
        scratch_shapes=scratch_shapes, compiler_params=_params(("arbitrary", "arbitrary", "arbitrary")),
    )(*ins)
    return outs[0] if ride is None else (outs[0], outs[1])


def Row(arr, width=None, off=0, var=True):
    return ("row", arr, arr.shape[1] if width is None else width, off, var)


def Prev(arr, width=None, off=0, var=True):
    return ("prev", arr, arr.shape[1] if width is None else width, off, var)


def Next(arr, width=None, off=0, var=True):
    return ("next", arr, arr.shape[1] if width is None else width, off, var)


def Full(arr, width=None, off=0, var=True):
    return ("full", arr, arr.shape[1] if width is None else width, off, var)


def rows_call(name, fn, rows, tile, ncol, ins, outs=(), accs=()):
    nrow = rows // tile
    assert rows % tile == 0 and tile % SUBLANES == 0, (name, rows, tile)
    last8 = rows // SUBLANES - 1
    per8 = tile // SUBLANES

    def spec(kind, arr, width, off, var):
        col = (lambda j: off + j) if var else (lambda j: off)
        if kind == "row":
            return pl.BlockSpec((tile, width), lambda j, i: (i, col(j)))
        if kind == "prev":
            return pl.BlockSpec((SUBLANES, width), lambda j, i: (jnp.maximum(i * per8 - 1, 0), col(j)))
        if kind == "next":
            return pl.BlockSpec((SUBLANES, width), lambda j, i: (jnp.minimum((i + 1) * per8, last8), col(j)))
        return pl.BlockSpec((arr.shape[0], width), lambda j, i: (0, col(j)))

    n_in, n_out = len(ins), len(outs)

    def body(*refs):
        j, i = pl.program_id(0), pl.program_id(1)
        o_tiles, a_tiles = fn(i, j, *[r[...] for r in refs[:n_in]])
        for r, o in zip(refs[n_in:n_in + n_out], o_tiles, strict=True):
            r[...] = o.astype(r.dtype)
        acc_refs = refs[n_in + n_out:]
        if acc_refs:
            @pl.when(i == 0)
            def _():
                for r in acc_refs:
                    r[...] = jnp.zeros_like(r)
            for r, a in zip(acc_refs, a_tiles, strict=True):
                r[...] += a

    out_shape = [jax.ShapeDtypeStruct((rows, w * ncol), dt) for dt, w in outs]
    out_shape += [jax.ShapeDtypeStruct((r, w * ncol), F32) for r, w in accs]
    out_specs = [pl.BlockSpec((tile, w), lambda j, i: (i, j)) for _, w in outs]
    out_specs += [pl.BlockSpec((r, w), lambda j, i: (0, j)) for r, w in accs]
    res = pl.pallas_call(
        body, name=name, grid=(ncol, nrow), in_specs=[spec(*s) for s in ins], out_specs=out_specs,
        out_shape=out_shape, compiler_params=_params(("arbitrary", "arbitrary")),
    )(*[s[1] for s in ins])
    return res


def _shift_down(tile, prev8, s, first):
    if s == 0:
        return tile
    rolled = pltpu.roll(tile, s, 0)
    pr = jnp.where(first, 0.0, pltpu.roll(prev8, s, 0))
    head = jnp.where(_iota(pr.shape, 0) < s, pr, rolled[:SUBLANES])
    return jnp.concatenate([head, rolled[SUBLANES:]], axis=0)


def _shift_up(tile, next8, s, last):
    if s == 0:
        return tile
    t = tile.shape[0]
    rolled = pltpu.roll(tile, t - s, 0)
    nx = jnp.where(last, 0.0, pltpu.roll(next8, SUBLANES - s, 0))
    tail = jnp.where(_iota(nx.shape, 0) >= SUBLANES - s, nx, rolled[t - SUBLANES:])
    return jnp.concatenate([rolled[:t - SUBLANES], tail], axis=0)


def _row(w, k):
    return jnp.sum(jnp.where(_iota(w.shape, 0) == k, w, 0.0), axis=0, keepdims=True)


def _conv(x, prev8, w, b, first):
    kk = w.shape[0]
    y = b + _row(w, kk - 1) * x
    for k in range(kk - 1):
        y = y + _row(w, k) * _shift_down(x, prev8, kk - 1 - k, first)
    return y


def _conv_wgrad(x, prev8, dy, kk, first):
    out = jnp.zeros((SUBLANES, x.shape[1]), F32)
    for k in range(kk):
        r = jnp.sum(dy * _shift_down(x, prev8, kk - 1 - k, first), axis=0, keepdims=True)
        out = out + jnp.where(_iota(out.shape, 0) == k, r, 0.0)
    return out


def _conv_t(dy, next8, w, last):
    kk = w.shape[0]
    dx = _row(w, kk - 1) * dy
    for k in range(kk - 1):
        dx = dx + _row(w, k) * _shift_up(dy, next8, kk - 1 - k, last)
    return dx


def f_rms(x, w):
    return x * lax.rsqrt(jnp.mean(x * x, axis=-1, keepdims=True) + EPS) * w


def rms_fwd(name, x, w):
    def fn(i, j, xt, wt):
        return [f_rms(xt, wt)], []
    return rows_call(name, fn, x.shape[0], _divisor(x.shape[0], 512, 8), 1, [Row(x), Full(w)],
                     outs=[(BF16, x.shape[1])])[0]


def rms_bwd(name, x, w, dh, dres):
    def fn(i, j, xt, wt, dht, drt):
        _, vjp = jax.vjp(f_rms, xt, wt)
        dx, dw = vjp(dht)
        return [drt + dx], [dw]
    d = x.shape[1]
    return rows_call(name, fn, x.shape[0], _divisor(x.shape[0], 256, 8), 1, [Row(x), Full(w), Row(dh), Row(dres)],
                     outs=[(F32, d)], accs=[(1, d)])


def loss_head(name, x, w, target):
    def fn(i, j, xt, wt, tt):
        def f(xx, ww):
            err = f_rms(xx, ww) - tt
            return 0.5 * jnp.mean(err * err, axis=-1, keepdims=True)
        rows, vjp = jax.vjp(f, xt, wt)
        dx, dw = vjp(jnp.ones_like(rows))
        return [dx], [dw, jnp.broadcast_to(jnp.sum(rows, axis=0, keepdims=True), (1, 128))]
    d = x.shape[1]
    return rows_call(name, fn, x.shape[0], _divisor(x.shape[0], 256, 8), 1, [Row(x), Full(w), Row(target)],
                     outs=[(F32, d)], accs=[(1, d), (1, 128)])


def ffn_act_fwd(name, u0, cw, cb):
    t, two_f = u0.shape
    wc = _divisor(two_f // 2, 512, 128)
    nc = two_f // 2 // wc

    def fn(i, j, ug, ugp, uv, uvp, wg, wv, bg, bv):
        first = i == 0
        g = _conv(ug, ugp, wg, bg, first)
        v = _conv(uv, uvp, wv, bv, first)
        return [jax.nn.silu(g) * v], []
    ins = [Row(u0, wc), Prev(u0, wc), Row(u0, wc, nc), Prev(u0, wc, nc),
           Full(cw, wc), Full(cw, wc, nc), Full(cb, wc), Full(cb, wc, nc)]
    return rows_call(name, fn, t, _divisor(t, 1024, 8), nc, ins, outs=[(BF16, wc)])[0]


def ffn_act_bwd(name, u0, cw, cb, dact):
    t, two_f = u0.shape
    wc = _divisor(two_f // 2, 512, 128)
    nc = two_f // 2 // wc
    kk = cw.shape[0]

    def fn(i, j, ug, ugp, uv, uvp, wg, wv, bg, bv, da):
        first = i == 0
        g = _conv(ug, ugp, wg, bg, first)
        v = _conv(uv, uvp, wv, bv, first)
        _, vjp = jax.vjp(lambda gg, vv: jax.nn.silu(gg) * vv, g, v)
        dg, dv = vjp(da)
        accs = [_conv_wgrad(ug, ugp, dg, kk, first), _conv_wgrad(uv, uvp, dv, kk, first),
                jnp.sum(dg, axis=0, keepdims=True), jnp.sum(dv, axis=0, keepdims=True)]
        return [dg, dv], accs
    ins = [Row(u0, wc), Prev(u0, wc), Row(u0, wc, nc), Prev(u0, wc, nc),
           Full(cw, wc), Full(cw, wc, nc), Full(cb, wc), Full(cb, wc, nc), Row(dact, wc)]
    return rows_call(name, fn, t, _divisor(t, 512, 8), nc, ins, outs=[(F32, wc), (F32, wc)],
                     accs=[(SUBLANES, wc), (SUBLANES, wc), (1, wc), (1, wc)])


def conv_t_call(name, dy, cw, col_off, width):
    out_dtype = BF16
    t, c = dy.shape
    nc = c // width
    nrow_tile = _divisor(t, 512, 8)
    last_i = t // nrow_tile - 1

    def fn(i, j, d, dn, w):
        return [_conv_t(d, dn, w, i == last_i)], []
    return rows_call(name, fn, t, nrow_tile, nc, [Row(dy, width), Next(dy, width), Full(cw, width, col_off)],
                     outs=[(out_dtype, width)])[0]


def f_lru_gates(xc, wr, br, wi, bi, lam):
    r = jax.nn.sigmoid(mm_nn(xc, wr) + br)
    gi = jax.nn.sigmoid(mm_nn(xc, wi) + bi)
    log_a = -LRU_C * r * _softplus(-lam)
    a = jnp.exp(log_a)
    u = jnp.sqrt(_neg_expm1(2.0 * log_a)) * (gi * xc)
    return a, u


def lru_pre_fwd(name, proj, cw, cb, wr, br, wi, bi, lam):
    t, w = proj.shape[0], lam.shape[1]

    def fn(i, j, xa, xap, cwt, cbt, wrt, brt, wit, bit, lamt):
        xc = _conv(xa, xap, cwt, cbt, i == 0)
        a, u = f_lru_gates(xc, wrt, brt, wit, bit, lamt)
        return [xc, a, u], []
    ins = [Row(proj, w), Prev(proj, w), Full(cw), Full(cb), Full(wr), Full(br), Full(wi), Full(bi), Full(lam)]
    return rows_call(name, fn, t, _divisor(t, 512, 8), 1, ins, outs=[(F32, w)] * 3)


def lru_pre_bwd(name, proj, cw, cb, wr, br, wi, bi, lam, xc, hseq, lamb):
    t, w = proj.shape[0], lam.shape[1]
    kk = cw.shape[0]

    def fn(i, j, xa, xap, xct, hs, hsp, lb, wrt, brt, wit, bit, lamt):
        first = i == 0
        da = lb * _shift_down(hs, hsp, 1, first)
        _, vjp = jax.vjp(f_lru_gates, xct, wrt.astype(F32), brt, wit.astype(F32), bit, lamt)
        dxc, dwr, dbr, dwi, dbi, dlam = vjp((da, lb))
        accs = [_conv_wgrad(xa, xap, dxc, kk, first), jnp.sum(dxc, axis=0, keepdims=True), dwr, dbr, dwi, dbi, dlam]
        return [dxc], accs
    ins = [Row(proj, w), Prev(proj, w), Row(xc), Row(hseq), Prev(hseq), Row(lamb),
           Full(wr), Full(br), Full(wi), Full(bi), Full(lam)]
    return rows_call(name, fn, t, _divisor(t, 256, 8), 1, ins, outs=[(F32, w)],
                     accs=[(SUBLANES, w), (1, w), (w, w), (1, w), (w, w), (1, w), (1, w)])


def lin_scan(name, a, x, reverse):
    t, c = a.shape
    tile = _divisor(t, 512, 8)
    n = t // tile

    def body(a_ref, x_ref, o_ref, c_ref):
        @pl.when(pl.program_id(0) == 0)
        def _():
            c_ref[...] = jnp.zeros_like(c_ref)

        def step(s, carry):
            r = (tile - 1 - s) if reverse else s
            at, xt = a_ref[pl.ds(r, 1), :], x_ref[pl.ds(r, 1), :]
            o = (xt + carry) if reverse else (at * carry + xt)
            o_ref[pl.ds(r, 1), :] = o
            return (at * o) if reverse else o
        c_ref[...] = lax.fori_loop(0, tile, step, c_ref[...], unroll=8)

    spec = pl.BlockSpec((tile, c), (lambda i: (n - 1 - i, 0)) if reverse else (lambda i: (i, 0)))
    return pl.pallas_call(
        body, name=name, grid=(n,), in_specs=[spec, spec], out_specs=spec,
        out_shape=jax.ShapeDtypeStruct((t, c), F32), scratch_shapes=[pltpu.VMEM((1, c), F32)],
        compiler_params=_params(("arbitrary",)),
    )(a, x)


def _hg_chunk(s, q, fr, v, lb):
    f = lb + (1.0 - lb) * jax.nn.sigmoid(fr)
    k = 1.0 - f
    g = jnp.log(f)
    qs = jax.nn.silu(q) * (STATE ** -0.5)
    cum = dot_hi(_tril(CHUNK), g)
    tot = jnp.sum(g, axis=0, keepdims=True)
    mid = jnp.sum(jnp.where(_iota(g.shape, 0) < CHUNK // 2, g, 0.0), axis=0, keepdims=True)
    scores = mm_nt(qs * jnp.exp(cum - mid), k * jnp.exp(mid - cum))
    scores = jnp.where(_tril(CHUNK) > 0, scores, 0.0)
    o = mm_nn(scores, v) + mm_nn(qs * jnp.exp(cum), s)
    decay = jnp.broadcast_to(jnp.exp(tot), s.shape).T
    s_new = decay * s + mm_tn(k * jnp.exp(tot - cum), v)
    return o, s_new


def _hg_specs(proj, heads, lru_w, hg_w, rows, rev):
    nblk = proj.shape[0] // rows
    blk = (lambda b: nblk - 1 - b) if rev else (lambda b: b)
    base = 2 * lru_w // STATE
    per = hg_w // STATE
    col = [pl.BlockSpec((rows, STATE), functools.partial(lambda h, b, o: (blk(b), o + h), o=base + k * per))
           for k in range(3)]
    return nblk, blk, col


def hg_fwd(name, proj, lbs, lru_w, cb=4):
    t, hg_w = proj.shape[0], lbs.shape[1]
    heads = hg_w // STATE
    cb = min(cb, t // CHUNK)
    rows = cb * CHUNK
    nblk, blk, col = _hg_specs(proj, heads, lru_w, hg_w, rows, False)

    def body(q_ref, f_ref, v_ref, lb_ref, o_ref, s_ref, st):
        @pl.when(pl.program_id(1) == 0)
        def _():
            st[...] = jnp.zeros_like(st)
        s = st[...]
        for c in range(cb):
            sl = slice(c * CHUNK, (c + 1) * CHUNK)
            s_ref[c] = s
            o, s = _hg_chunk(s, q_ref[sl, :], f_ref[sl, :], v_ref[sl, :], lb_ref[...])
            o_ref[sl, :] = o
        st[...] = s

    return pl.pallas_call(
        body, name=name, grid=(heads, nblk),
        in_specs=col + [pl.BlockSpec((1, STATE), lambda h, b: (0, h))],
        out_specs=[pl.BlockSpec((rows, STATE), lambda h, b: (b, h)),
                   pl.BlockSpec((cb, None, STATE, STATE), lambda h, b: (b, h, 0, 0))],
        out_shape=[jax.ShapeDtypeStruct((t, hg_w), F32),
                   jax.ShapeDtypeStruct((t // CHUNK, heads, STATE, STATE), F32)],
        scratch_shapes=[pltpu.VMEM((STATE, STATE), F32)],
        compiler_params=_params(("arbitrary", "arbitrary")),
    )(proj, proj, proj, lbs)


def hg_bwd(name, proj, lbs, states, do, lru_w, cb=4):
    t, hg_w = proj.shape[0], lbs.shape[1]
    heads = hg_w // STATE
    cb = min(cb, t // CHUNK)
    rows = cb * CHUNK
    nblk, blk, col = _hg_specs(proj, heads, lru_w, hg_w, rows, True)

    def body(q_ref, f_ref, v_ref, lb_ref, s_ref, do_ref, dq_ref, df_ref, dv_ref, dlb_ref, dst):
        @pl.when(pl.program_id(1) == 0)
        def _():
            dst[...] = jnp.zeros_like(dst)
            dlb_ref[...] = jnp.zeros_like(dlb_ref)
        ds = dst[...]
        dlb = jnp.zeros((1, STATE), F32)
        for c in reversed(range(cb)):
            sl = slice(c * CHUNK, (c + 1) * CHUNK)
            _, vjp = jax.vjp(_hg_chunk, s_ref[c], q_ref[sl, :], f_ref[sl, :], v_ref[sl, :], lb_ref[...])
            ds, dq, df, dv, dl = vjp((do_ref[sl, :], ds))
            dq_ref[sl, :] = dq.astype(dq_ref.dtype)
            df_ref[sl, :] = df.astype(df_ref.dtype)
            dv_ref[sl, :] = dv.astype(dv_ref.dtype)
            dlb = dlb + dl
        dst[...] = ds
        dlb_ref[...] += dlb

    rspec = pl.BlockSpec((rows, STATE), lambda h, b: (blk(b), h))
    return pl.pallas_call(
        body, name=name, grid=(heads, nblk),
        in_specs=col + [pl.BlockSpec((1, STATE), lambda h, b: (0, h)),
                        pl.BlockSpec((cb, None, STATE, STATE), lambda h, b: (blk(b), h, 0, 0)), rspec],
        out_specs=[rspec, rspec, rspec, pl.BlockSpec((1, STATE), lambda h, b: (0, h))],
        out_shape=[jax.ShapeDtypeStruct((t, hg_w), BF16)] * 3 + [jax.ShapeDtypeStruct((1, hg_w), F32)],
        scratch_shapes=[pltpu.VMEM((STATE, STATE), F32)],
        compiler_params=_params(("arbitrary", "arbitrary")),
    )(proj, proj, proj, lbs, states, do)


def f_even_post(hseq, ga, ob, gb, nw):
    parts = [hseq * jax.nn.gelu(ga)]
    for h in range(ob.shape[1] // STATE):
        o = ob[:, h * STATE:(h + 1) * STATE]
        on = o * lax.rsqrt(jnp.mean(o * o, axis=-1, keepdims=True) + EPS) * nw
        parts.append(on * jax.nn.silu(gb[:, h * STATE:(h + 1) * STATE]))
    return jnp.concatenate(parts, axis=-1)


def _even_post_ins(proj, hseq, ob, nw):
    w, v = hseq.shape[1], ob.shape[1]
    assert w == v
    return [Row(hseq), Row(proj, w, 1), Row(ob), Row(proj, v, (2 * w + 3 * v) // v), Full(nw)]


def even_post_fwd(name, proj, hseq, ob, nw):
    t = proj.shape[0]

    def fn(i, j, hs, ga, o, gb, nwt):
        return [f_even_post(hs, ga, o, gb, nwt)], []
    return rows_call(name, fn, t, _divisor(t, 256, 8), 1, _even_post_ins(proj, hseq, ob, nw),
                     outs=[(BF16, hseq.shape[1] + ob.shape[1])])[0]


def even_post_bwd(name, proj, hseq, ob, nw, dy):
    t, w, v = proj.shape[0], hseq.shape[1], ob.shape[1]

    def fn(i, j, hs, ga, o, gb, nwt, dyt):
        _, vjp = jax.vjp(f_even_post, hs, ga, o, gb, nwt)
        dhs, dga, dob, dgb, dnw = vjp(dyt)
        return [dhs, dga, dob, dgb], [dnw]
    return rows_call(name, fn, t, _divisor(t, 256, 8), 1, _even_post_ins(proj, hseq, ob, nw) + [Row(dy)],
                     outs=[(F32, w), (BF16, w), (F32, v), (BF16, v)], accs=[(1, STATE)])


def f_lbs(hb):
    e = jnp.exp(hb - jnp.max(hb, axis=0, keepdims=True))
    p = e / jnp.sum(e, axis=0, keepdims=True)
    out, run = jnp.zeros_like(p), jnp.zeros_like(p[:1])
    for r in range(hb.shape[0]):
        run = run + _row(p, r)
        out = out + jnp.where(_iota(p.shape, 0) == r, run - _row(p, 0), 0.0)
    return out


def whole_call(name, fn, ins, out_shapes):
    def body(*refs):
        outs = fn(*[r[...] for r in refs[:len(ins)]])
        for r, o in zip(refs[len(ins):], outs, strict=True):
            r[...] = o
    return pl.pallas_call(body, name=name, out_shape=[jax.ShapeDtypeStruct(s, F32) for s in out_shapes])(*ins)


HEADDIM = 64


def f_ssd_prep(dtr, dtb, alog, inner):
    rows = dtr.shape[0]
    dt_all = _softplus(dtr + dtb)
    da_all = dt_all * (-jnp.exp(alog))
    tril = _tril(CHUNK)
    cums = [dot_hi(tril, da_all[c * CHUNK:(c + 1) * CHUNK]) for c in range(rows // CHUNK)]
    cum_all = jnp.concatenate(cums, axis=0) if len(cums) > 1 else cums[0]
    head_of = _iota((STATE, inner), 1) - HEADDIM * _iota((STATE, inner), 0)
    spread = ((head_of >= 0) & (head_of < HEADDIM)).astype(F32)
    even = (_iota((CHUNK, STATE), 1) == 2 * _iota((CHUNK, STATE), 0)).astype(F32)
    odd = (_iota((CHUNK, STATE), 1) == 2 * _iota((CHUNK, STATE), 0) + 1).astype(F32)
    left = _iota((CHUNK, STATE), 1) < HEADDIM
    cumrows = []
    for cm in cums:
        twice = jnp.concatenate([cm, cm], axis=0)
        cumrows.append(jnp.where(left, dot_hi(even, twice, NT), dot_hi(odd, twice, NT)))
    return dot_hi(dt_all, spread), dot_hi(cum_all, spread), tuple(cumrows)


def ssd_prep_fwd(name, dtr, dtb, alog, inner, k=4):
    t = dtr.shape[0]
    k = min(k, t // CHUNK)
    rows = k * CHUNK

    def body(dtr_ref, dtb_ref, alog_ref, dt_ref, cum_ref, cr_ref):
        dt, cum, crs = f_ssd_prep(dtr_ref[...], dtb_ref[...], alog_ref[...], inner)
        dt_ref[...] = dt
        cum_ref[...] = cum
        for c, cr in enumerate(crs):
            cr_ref[c] = cr

    one = pl.BlockSpec((1, STATE), lambda i: (0, 0))
    wide = pl.BlockSpec((rows, inner), lambda i: (i, 0))
    return pl.pallas_call(
        body, name=name, grid=(t // rows,),
        in_specs=[pl.BlockSpec((rows, STATE), lambda i: (i, 0)), one, one],
        out_specs=[wide, wide, pl.BlockSpec((k, CHUNK, STATE), lambda i: (i, 0, 0))],
        out_shape=[jax.ShapeDtypeStruct((t, inner), F32), jax.ShapeDtypeStruct((t, inner), F32),
                   jax.ShapeDtypeStruct((t // CHUNK, CHUNK, STATE), F32)],
        compiler_params=_params(("arbitrary",)),
    )(dtr, dtb, alog)


def ssd_prep_bwd(name, dtr, dtb, alog, ddt, dcum, dcr, inner, k=4):
    t = dtr.shape[0]
    k = min(k, t // CHUNK)
    rows = k * CHUNK

    def body(dtr_ref, dtb_ref, alog_ref, ddt_ref, dcum_ref, dcr_ref, ddtr_ref, ddtb_ref, dalog_ref):
        @pl.when(pl.program_id(0) == 0)
        def _():
            ddtb_ref[...] = jnp.zeros_like(ddtb_ref)
            dalog_ref[...] = jnp.zeros_like(dalog_ref)
        _, vjp = jax.vjp(functools.partial(f_ssd_prep, inner=inner), dtr_ref[...], dtb_ref[...], alog_ref[...])
        ddtr, ddtb, dalog = vjp((ddt_ref[...], dcum_ref[...], tuple(dcr_ref[c] for c in range(k))))
        ddtr_ref[...] = ddtr
        ddtb_ref[...] += ddtb
        dalog_ref[...] += dalog

    one = pl.BlockSpec((1, STATE), lambda i: (0, 0))
    wide = pl.BlockSpec((rows, inner), lambda i: (i, 0))
    tall = pl.BlockSpec((rows, STATE), lambda i: (i, 0))
    return pl.pallas_call(
        body, name=name, grid=(t // rows,),
        in_specs=[tall, one, one, wide, wide, pl.BlockSpec((k, CHUNK, STATE), lambda i: (i, 0, 0))],
        out_specs=[tall, one, one],
        out_shape=[jax.ShapeDtypeStruct((t, STATE), F32), jax.ShapeDtypeStruct((1, STATE), F32),
                   jax.ShapeDtypeStruct((1, STATE), F32)],
        compiler_params=_params(("arbitrary",)),
    )(dtr, dtb, alog, ddt, dcum, dcr)


def _ssd_group(states, x, bm, cm, dt, cum, cumrs, dsk):
    half = _iota((CHUNK, STATE), 1) >= HEADDIM
    pos = _iota((CHUNK, STATE), 1) - jnp.where(half, HEADDIM, 0)
    row = _iota((CHUNK, STATE), 0)
    causal = row >= pos
    cb2 = mm_nt(cm, jnp.concatenate([bm, bm], axis=0))
    ys, new = [], []
    for pp, (s, cumr) in enumerate(zip(states, cumrs, strict=True)):
        lanes = slice(pp * STATE, (pp + 1) * STATE)
        cu, xdt = cum[:, lanes], x[:, lanes] * dt[:, lanes]
        tot = jnp.sum(jnp.where(row == CHUNK - 1, cu, 0.0), axis=0, keepdims=True)
        m = jnp.where(causal, cb2 * jnp.exp(jnp.where(causal, cu - cumr, 0.0)), 0.0)
        x2 = jnp.concatenate([jnp.where(half, 0.0, xdt), jnp.where(half, xdt, 0.0)], axis=0)
        ys.append(mm_nn(m, x2) + mm_nt(cm, s) * jnp.exp(cu) + x[:, lanes] * dsk[:, lanes])
        decay = jnp.broadcast_to(jnp.exp(tot), s.shape).T
        new.append(decay * s + mm_tn(xdt * jnp.exp(tot - cu), bm))
    return jnp.concatenate(ys, axis=1), tuple(new)


def _ssd_specs(act, inner, groups, rev):
    t = act.shape[0]
    nch = t // CHUNK
    ch = (lambda c: nch - 1 - c) if rev else (lambda c: c)
    gw = inner // groups
    wide = pl.BlockSpec((CHUNK, gw), lambda c, g: (ch(c), g))
    specs = [wide,
             pl.BlockSpec((CHUNK, STATE), lambda c, g: (ch(c), inner // STATE + g)),
             pl.BlockSpec((CHUNK, STATE), lambda c, g: (ch(c), inner // STATE + groups + g)),
             wide, wide,
             pl.BlockSpec((None, CHUNK, STATE), lambda c, g: (ch(c), 0, 0)),
             pl.BlockSpec((1, gw), lambda c, g: (0, g))]
    return nch, ch, gw, specs


def ssd_fwd(name, act, dt, cum, cumrow, dexp, inner, groups):
    t = act.shape[0]
    nch, ch, gw, specs = _ssd_specs(act, inner, groups, False)
    pairs = gw // (2 * HEADDIM)

    def body(x_ref, b_ref, c_ref, dt_ref, cum_ref, cr_ref, dsk_ref, y_ref, sv_ref, st):
        c, g = pl.program_id(0), pl.program_id(1)

        @pl.when(c == 0)
        def _():
            for pp in range(pairs):
                st[g * pairs + pp] = jnp.zeros((STATE, STATE), F32)

        states = tuple(st[g * pairs + pp] for pp in range(pairs))
        cumrs = tuple(cr_ref[pl.ds(g * pairs + pp, 1), :] for pp in range(pairs))
        for pp in range(pairs):
            sv_ref[pp] = states[pp]
        y, new = _ssd_group(states, x_ref[...], b_ref[...], c_ref[...], dt_ref[...], cum_ref[...], cumrs, dsk_ref[...])
        y_ref[...] = y
        for pp in range(pairs):
            st[g * pairs + pp] = new[pp]

    return pl.pallas_call(
        body, name=name, grid=(nch, groups), in_specs=specs,
        out_specs=[pl.BlockSpec((CHUNK, gw), lambda c, g: (c, g)),
                   pl.BlockSpec((None, pairs, STATE, STATE), lambda c, g: (c, g, 0, 0))],
        out_shape=[jax.ShapeDtypeStruct((t, inner), F32),
                   jax.ShapeDtypeStruct((nch, groups * pairs, STATE, STATE), F32)],
        scratch_shapes=[pltpu.VMEM((groups * pairs, STATE, STATE), F32)],
        compiler_params=_params(("arbitrary", "arbitrary")),
    )(act, act, act, dt, cum, cumrow, dexp)


def ssd_bwd(name, act, dt, cum, cumrow, dexp, states, dy, inner, groups):
    t = act.shape[0]
    nch, ch, gw, specs = _ssd_specs(act, inner, groups, True)
    pairs = gw // (2 * HEADDIM)

    def body(x_ref, b_ref, c_ref, dt_ref, cum_ref, cr_ref, dsk_ref, sv_ref, dy_ref,
             dx_ref, db_ref, dc_ref, ddt_ref, dcum_ref, dcr_ref, ddsk_ref, dst):
        c, g = pl.program_id(0), pl.program_id(1)

        @pl.when((c == 0) & (g == 0))
        def _():
            ddsk_ref[...] = jnp.zeros_like(ddsk_ref)

        @pl.when(c == 0)
        def _():
            for pp in range(pairs):
                dst[g * pairs + pp] = jnp.zeros((STATE, STATE), F32)

        @pl.when(g == 0)
        def _():
            dcr_ref[...] = jnp.zeros_like(dcr_ref)

        cumrs = tuple(cr_ref[pl.ds(g * pairs + pp, 1), :] for pp in range(pairs))
        _, vjp = jax.vjp(_ssd_group, tuple(sv_ref[pp] for pp in range(pairs)), x_ref[...], b_ref[...], c_ref[...],
                         dt_ref[...], cum_ref[...], cumrs, dsk_ref[...])
        ds, dx, db, dc, ddt, dcum, dcrs, ddsk = vjp((dy_ref[...], tuple(dst[g * pairs + pp] for pp in range(pairs))))
        for pp in range(pairs):
            dst[g * pairs + pp] = ds[pp]
            dcr_ref[pl.ds(g * pairs + pp, 1), :] = dcrs[pp]
        dx_ref[...] = dx
        db_ref[...] = db
        dc_ref[...] = dc
        ddt_ref[...] = ddt
        dcum_ref[...] = dcum
        col = pl.ds(pl.multiple_of(g * gw, STATE), gw)
        ddsk_ref[:, col] = ddsk_ref[:, col] + ddsk

    wide = pl.BlockSpec((CHUNK, gw), lambda c, g: (ch(c), g))
    grp = pl.BlockSpec((CHUNK, STATE), lambda c, g: (ch(c), g))
    return pl.pallas_call(
        body, name=name, grid=(nch, groups),
        in_specs=specs + [pl.BlockSpec((None, pairs, STATE, STATE), lambda c, g: (ch(c), g, 0, 0)), wide],
        out_specs=[wide, grp, grp, wide, wide, pl.BlockSpec((None, CHUNK, STATE), lambda c, g: (ch(c), 0, 0)),
                   pl.BlockSpec((1, inner), lambda c, g: (0, 0))],
        out_shape=[jax.ShapeDtypeStruct((t, inner), F32), jax.ShapeDtypeStruct((t, groups * STATE), F32),
                   jax.ShapeDtypeStruct((t, groups * STATE), F32), jax.ShapeDtypeStruct((t, inner), F32),
                   jax.ShapeDtypeStruct((t, inner), F32), jax.ShapeDtypeStruct((nch, CHUNK, STATE), F32),
                   jax.ShapeDtypeStruct((1, inner), F32)],
        scratch_shapes=[pltpu.VMEM((groups * pairs, STATE, STATE), F32)],
        compiler_params=_params(("arbitrary", "arbitrary")),
    )(act, act, act, dt, cum, cumrow, dexp, states, dy)


def _ssd_conv_width(inner, cdim):
    return _divisor(math.gcd(inner, cdim), 2048, 128)


def ssd_conv_fwd(name, zx, cw, cb, inner):
    t, cdim = zx.shape[0], cw.shape[1]
    wc = _ssd_conv_width(inner, cdim)

    def fn(i, j, xt, xp, w, b):
        return [jax.nn.silu(_conv(xt, xp, w, b, i == 0))], []
    ins = [Row(zx, wc, inner // wc), Prev(zx, wc, inner // wc), Full(cw, wc), Full(cb, wc)]
    return rows_call(name, fn, t, _divisor(t, 256, 8), cdim // wc, ins, outs=[(F32, wc)])[0]


def ssd_conv_bwd(name, zx, cw, cb, dact, inner):
    t, cdim = zx.shape[0], cw.shape[1]
    wc = _ssd_conv_width(inner, cdim)
    kk = cw.shape[0]

    def fn(i, j, xt, xp, w, b, da):
        first = i == 0
        pre = _conv(xt, xp, w, b, first)
        _, vjp = jax.vjp(jax.nn.silu, pre)
        dpre, = vjp(da)
        return [dpre], [_conv_wgrad(xt, xp, dpre, kk, first), jnp.sum(dpre, axis=0, keepdims=True)]
    ins = [Row(zx, wc, inner // wc), Prev(zx, wc, inner // wc), Full(cw, wc), Full(cb, wc), Row(dact, wc)]
    return rows_call(name, fn, t, _divisor(t, 256, 8), cdim // wc, ins, outs=[(F32, wc)],
                     accs=[(SUBLANES, wc), (1, wc)])


def f_ssd_post(y, z, nw):
    yz = y * jax.nn.silu(z)
    return yz * lax.rsqrt(jnp.mean(yz * yz, axis=-1, keepdims=True) + EPS) * nw


def ssd_post_fwd(name, y, zx, nw, groups):
    t, inner = y.shape
    gw = inner // groups

    def fn(i, j, yt, zt, nwt):
        return [f_ssd_post(yt, zt, nwt)], []
    return rows_call(name, fn, t, _divisor(t, 1024, 8), groups, [Row(y, gw), Row(zx, gw), Full(nw, gw)],
                     outs=[(BF16, gw)])[0]


def ssd_post_bwd(name, y, zx, nw, dyn, groups):
    t, inner = y.shape
    gw = inner // groups

    def fn(i, j, yt, zt, nwt, dt):
        _, vjp = jax.vjp(f_ssd_post, yt, zt, nwt)
        dy, dz, dnw = vjp(dt)
        return [dy, dz], [dnw]
    return rows_call(name, fn, t, _divisor(t, 512, 8), groups, [Row(y, gw), Row(zx, gw), Full(nw, gw), Row(dyn, gw)],
                     outs=[(F32, gw), (BF16, gw)], accs=[(1, gw)])


ADAMW_TILE = 256 * 1024


def adamw(name, slots, w, m, v, layer=0, prev=None):
    nl, n, c = w.shape
    tr = _divisor(n, max(16, ADAMW_TILE // c // 16 * 16), 16)

    def body(s_ref, w_ref, m_ref, v_ref, *rest):
        g_ref, d_ref, mo_ref, vo_ref = rest[-4:]
        g = s_ref[0].astype(F32)
        for k in range(1, N_DEV):
            g = g + s_ref[k].astype(F32)
        mn = ADAM_B1 * m_ref[...] + (1.0 - ADAM_B1) * g
        vn = ADAM_B2 * v_ref[...] + (1.0 - ADAM_B2) * (g * g)
        m_hat = mn / (1.0 - ADAM_B1 ** ADAM_STEP)
        v_hat = vn / (1.0 - ADAM_B2 ** ADAM_STEP)
        g_ref[...] = g
        d_ref[...] = -ADAM_LR * (m_hat / (jnp.sqrt(v_hat) + ADAM_EPS) + ADAM_WD * w_ref[...])
        mo_ref[...] = mn
        vo_ref[...] = vn

    spec = pl.BlockSpec((None, tr, c), lambda i: (layer, i, 0))
    prev = list(prev) if prev is not None else []
    return pl.pallas_call(
        body, name=name, grid=(n // tr,),
        in_specs=[pl.BlockSpec((N_DEV, tr, c), lambda i: (0, i, 0)), spec, spec, spec] + [_HBM] * len(prev),
        out_specs=[spec] * 4, out_shape=[jax.ShapeDtypeStruct((nl, n, c), F32)] * 4,
        input_output_aliases={4 + k: k for k in range(len(prev))},
        compiler_params=_params(("parallel",)),
    )(slots, w, m, v, *prev)


def all_gather(name, shard):
    def body(x_ref, out_ref, send_sems, recv_sems, local_sem):
        x, y, c = lax.axis_index("x"), lax.axis_index("y"), lax.axis_index("c")
        me, sibling = (x, y, c), (x, y, 1 - c)
        chips = [(1 - x, y), (x, 1 - y), (1 - x, 1 - y)]

        def slab(px, py, pc):
            return out_ref.at[4 * px + 2 * py + pc]

        def copy(k, block, to, src=None):
            return pltpu.make_async_remote_copy(
                src_ref=slab(*block) if src is None else src, dst_ref=slab(*block),
                send_sem=send_sems.at[k], recv_sem=recv_sems.at[k], device_id=to, device_id_type=_MESH)

        mine = pltpu.make_async_copy(x_ref, slab(*me), local_sem)
        mine.start()
        first = [copy(0, me, sibling, src=x_ref)]
        first += [copy(1 + j, me, (*chip, c), src=x_ref) for j, chip in enumerate(chips)]
        for cp in first:
            cp.start()
        passed = [copy(4 + j, (*chip, c), sibling) for j, chip in enumerate(chips)]
        for j, chip in enumerate(chips):
            copy(1 + j, (*chip, c), me).wait_recv()
            passed[j].start()
        copy(0, sibling, me).wait_recv()
        for j, chip in enumerate(chips):
            copy(4 + j, (*chip, 1 - c), me).wait_recv()
        for cp in first + passed:
            cp.wait_send()
        mine.wait()

    return pl.pallas_call(
        body, name=name, out_shape=jax.ShapeDtypeStruct((N_DEV,) + shard.shape, shard.dtype),
        in_specs=[_HBM], out_specs=_HBM,
        scratch_shapes=[pltpu.SemaphoreType.DMA((7,)), pltpu.SemaphoreType.DMA((7,)), pltpu.SemaphoreType.DMA],
    )(shard)


def exchange(name, pieces):
    def body(p_ref, out_ref, send_sems, recv_sems, local_sem):
        local, remote = _peer_copies("exchange", p_ref, out_ref, send_sems, recv_sems, local_sem)
        local.start()
        for cp in remote:
            cp.start()
        for cp in remote:
            cp.wait_recv()
        for cp in remote:
            cp.wait_send()
        local.wait()

    return pl.pallas_call(
        body, name=name, out_shape=jax.ShapeDtypeStruct(pieces.shape, pieces.dtype),
        in_specs=[_HBM], out_specs=_HBM, scratch_shapes=_PEER_SEMS,
    )(pieces)


WEIGHTS = ['norm_mix_w', 'norm_ffn_w', 'norm_final_w', 'ev_w_in', 'lru_conv_w', 'lru_conv_b', 'lru_w_r', 'lru_b_r',
           'lru_w_i', 'lru_b_i', 'lru_lambda', 'hg_lower_bounds', 'hg_norm_w', 'ev_w_out', 'ssd_w_in', 'ssd_conv_w',
           'ssd_conv_b', 'ssd_dt_bias', 'ssd_a_log', 'ssd_d', 'ssd_norm_w', 'ssd_w_out', 'ffn_w_up', 'ffn_conv_w',
           'ffn_conv_b', 'ffn_w_down']
COL_SHARDED = ['ev_w_in', 'ssd_w_in', 'ffn_w_up']
ROW_SHARDED = ['ev_w_out', 'ssd_w_out', 'ffn_w_down']
BIG = ['ev_w_in', 'ev_w_out', 'ssd_w_out', 'ffn_w_up', 'ffn_w_down', 'ssd_w_in']
SMALL_SHARDED = ['lru_conv_w', 'ssd_conv_w', 'ssd_conv_b', 'ssd_norm_w', 'ffn_conv_w']
SMALL = [n for n in WEIGHTS if n not in BIG]
SEG = 16 * 128


def _pad_to(flat, mult):
    extra = (-flat.shape[-1]) % mult
    if extra == 0:
        return flat
    return jnp.pad(flat, [(0, 0)] * (flat.ndim - 1) + [(0, extra)])


def _pack(segments):
    offs, parts, at = [], [], 0
    for s in segments:
        s = _pad_to(s, SEG)
        offs.append(at)
        at += s.shape[-1]
        parts.append(s)
    buf = jnp.concatenate(parts, axis=-1)
    return buf.reshape(buf.shape[:-1] + (at // 128, 128)), offs


def _unshard_last(g):
    g = jnp.moveaxis(g, 0, -2)
    return g.reshape(g.shape[:-2] + (N_DEV * g.shape[-1],))


def _block_diag(w):
    nb, b, _ = w.shape
    return (w[:, :, None, :] * jnp.eye(nb, dtype=w.dtype)[:, None, :, None]).reshape(nb * b, nb * b)


def _diag_blocks(dense, nb):
    b = dense.shape[0] // nb
    d4 = dense.reshape(nb, b, nb, b)
    return jnp.stack([d4[h, :, h, :] for h in range(nb)])


def kernel(x, norm_mix_w, norm_ffn_w, norm_final_w, ev_w_in, lru_conv_w, lru_conv_b, lru_w_r, lru_b_r, lru_w_i, lru_b_i, lru_lambda, hg_lower_bounds, hg_norm_w, ev_w_out, ssd_w_in, ssd_conv_w, ssd_conv_b, ssd_dt_bias, ssd_a_log, ssd_d, ssd_norm_w, ssd_w_out, ffn_w_up, ffn_conv_w, ffn_conv_b, ffn_w_down, loss_target, m_norm_mix_w, m_norm_ffn_w, m_norm_final_w, m_ev_w_in, m_lru_conv_w, m_lru_conv_b, m_lru_w_r, m_lru_b_r, m_lru_w_i, m_lru_b_i, m_lru_lambda, m_hg_lower_bounds, m_hg_norm_w, m_ev_w_out, m_ssd_w_in, m_ssd_conv_w, m_ssd_conv_b, m_ssd_dt_bias, m_ssd_a_log, m_ssd_d, m_ssd_norm_w, m_ssd_w_out, m_ffn_w_up, m_ffn_conv_w, m_ffn_conv_b, m_ffn_w_down, v_norm_mix_w, v_norm_ffn_w, v_norm_final_w, v_ev_w_in, v_lru_conv_w, v_lru_conv_b, v_lru_w_r, v_lru_b_r, v_lru_w_i, v_lru_b_i, v_lru_lambda, v_hg_lower_bounds, v_hg_norm_w, v_ev_w_out, v_ssd_w_in, v_ssd_conv_w, v_ssd_conv_b, v_ssd_dt_bias, v_ssd_a_log, v_ssd_d, v_ssd_norm_w, v_ssd_w_out, v_ffn_w_up, v_ffn_conv_w, v_ffn_conv_b, v_ffn_w_down):
    given = dict(locals())
    wts = {n: given[n] for n in WEIGHTS}
    mom1 = {n: given["m_" + n] for n in WEIGHTS}
    mom2 = {n: given["v_" + n] for n in WEIGHTS}
    me = 4 * lax.axis_index("x") + 2 * lax.axis_index("y") + lax.axis_index("c")

    depth, d = norm_mix_w.shape
    t = x.shape[1]
    x0 = x.reshape(t, d)
    target = loss_target.reshape(t, d)
    n_even, lru_w = lru_lambda.shape
    hg_w = hg_lower_bounds.shape[1]
    n_odd, heads = ssd_dt_bias.shape
    inner = N_DEV * ssd_norm_w.shape[1]
    cdim = N_DEV * ssd_conv_b.shape[1]
    groups = (cdim - inner) // (2 * STATE)
    assert inner == heads * HEADDIM and heads <= STATE and (inner // groups) % (2 * HEADDIM) == 0

    full = {}
    shard_buf, offs = _pack([wts[n].reshape(-1) for n in SMALL_SHARDED])
    gathered = all_gather("ag_small", shard_buf).reshape(N_DEV, -1)
    for n, off in zip(SMALL_SHARDED, offs, strict=True):
        full[n] = _unshard_last(gathered[:, off:off + wts[n].size].reshape((N_DEV,) + wts[n].shape))
    shard16 = {n: wts[n].astype(BF16) for n in BIG}
    mixer = lambda l: (('ev_w_in', l // 2), ('ev_w_out', l // 2)) if l % 2 == 0 else (('ssd_w_in', l // 2), ('ssd_w_out', l // 2))
    weight = {}

    def arrived(key, g8):
        weight[key] = (jnp.transpose(g8, (1, 0, 2)).reshape(g8.shape[1], -1) if key[0] in COL_SHARDED
                       else g8.reshape(-1, g8.shape[2]))

    def gather_ride(key):
        return ("gather", shard16[key[0]][key[1]])

    for key in (*mixer(0), ('ffn_w_up', 0), ('ffn_w_down', 0)):
        arrived(key, all_gather(f"ag_{key[0]}", shard16[key[0]][key[1]]))
    w_zx = lambda o: weight['ssd_w_in', o][:, :inner + cdim]
    w_dt = lambda o: jnp.pad(weight['ssd_w_in', o][:, inner + cdim:], ((0, 0), (0, STATE - heads)))
    pad_h = lambda a: jnp.pad(a.reshape(1, heads), ((0, 0), (0, STATE - heads)))
    row = lambda a: a.reshape(1, -1)

    lbs = whole_call("lbs_fwd", lambda hb: [f_lbs(hb)], [hg_lower_bounds], [hg_lower_bounds.shape])[0]

    def mm_carry(name, a, b, key, **kw):
        if key is None or key in weight:
            return matmul(name, a, b, **kw)
        out, g8 = matmul(name, a, b, ride=gather_ride(key), **kw)
        arrived(key, g8)
        return out

    saved = []
    xcur = x0
    for l in range(depth):
        sv = {'x0': xcur}
        k_in, k_out = mixer(l)
        nxt_in, nxt_out = mixer(l + 1) if l + 1 < depth else (None, None)
        h = rms_fwd(f"rms_mix_fwd", xcur, row(norm_mix_w[l]))
        sv['h'] = h
        if l % 2 == 0:
            e = l // 2
            wr = _block_diag(lru_w_r[e]).astype(BF16)
            wi = _block_diag(lru_w_i[e]).astype(BF16)
            lru_p = (full['lru_conv_w'][e], row(lru_conv_b[e]), wr, row(lru_b_r[e]), wi, row(lru_b_i[e]), row(lru_lambda[e]))
            proj = mm_carry("ev_in", h, weight[k_in], ('ffn_w_up', l))
            xc, a, u = lru_pre_fwd(f"lru_pre_fwd", proj, *lru_p)
            hseq = lin_scan(f"lru_scan_fwd", a, u, False)
            ob, states = hg_fwd(f"hg_fwd", proj, row(lbs[e]), lru_w)
            y = even_post_fwd(f"even_post_fwd", proj, hseq, ob, row(hg_norm_w[e]))
            xmid = mm_carry("ev_out", y, weight[k_out], ('ffn_w_down', l), res=xcur)
            sv.update(proj=proj, xc=xc, a=a, hseq=hseq, ob=ob, states=states, y=y, lru_p=lru_p)
        else:
            o = l // 2
            ssd_p = (pad_h(ssd_dt_bias[o]), pad_h(ssd_a_log[o]), row(jnp.repeat(ssd_d[o], HEADDIM)))
            wzx, wdt = w_zx(o), w_dt(o)
            zx = mm_carry("ssd_in", h, wzx, ('ffn_w_up', l))
            dtr = matmul("ssd_dt", h, wdt)
            sv.update(wzx=wzx, wdt=wdt)
            act = ssd_conv_fwd(f"ssd_conv_fwd", zx, full['ssd_conv_w'][o], row(full['ssd_conv_b'][o]), inner)
            prep = ssd_prep_fwd("ssd_prep_fwd", dtr, ssd_p[0], ssd_p[1], inner)
            ys, states = ssd_fwd(f"ssd_fwd", act, *prep, ssd_p[2], inner, groups)
            sv['prep'] = prep
            yn = ssd_post_fwd(f"ssd_post_fwd", ys, zx, row(full['ssd_norm_w'][o]), groups)
            xmid = mm_carry("ssd_out", yn, weight[k_out], ('ffn_w_down', l), res=xcur)
            sv.update(zx=zx, dtr=dtr, act=act, ys=ys, states=states, yn=yn, ssd_p=ssd_p)
        h2 = rms_fwd(f"rms_ffn_fwd", xmid, row(norm_ffn_w[l]))
        u0 = mm_carry("ffn_up", h2, weight['ffn_w_up', l], nxt_in)
        actf = ffn_act_fwd(f"ffn_act_fwd", u0, full['ffn_conv_w'][l], row(ffn_conv_b[l]))
        xcur = mm_carry("ffn_down", actf, weight['ffn_w_down', l], nxt_out, res=xmid)
        sv.update(x1=xmid, h2=h2, u0=u0, actf=actf)
        saved.append(sv)

    gcur, d_nfw, loss_row = loss_head("loss_head", xcur, row(norm_final_w), target)
    loss = lax.psum(loss_row[0, 0], ("x", "y", "c"))

    gl = {n: [None] * wts[n].shape[0] for n in SMALL if n != 'norm_final_w'}
    d_lbs = [None] * n_even
    slots = {}
    dw_opts = dict(ta=True, out_dtype=BF16, tm=2048, tk=1024)

    def pieces(key, dw):
        if key[0] in COL_SHARDED:
            return jnp.transpose(dw.reshape(dw.shape[0], N_DEV, -1), (1, 0, 2))
        return dw.reshape(N_DEV, -1, dw.shape[1])

    def mm_send(name, a, b, key, dw, **kw):
        out, slots[key] = matmul(name, a, b, ride=("exchange", pieces(key, dw)), **kw)
        return out

    pending = None
    for l in reversed(range(depth)):
        sv = saved[l]
        k_in, k_out = mixer(l)
        half = sv['u0'].shape[1] // 2
        fcw = full['ffn_conv_w'][l]
        if pending is None:
            dact = matmul("ffn_down_dx", gcur, weight['ffn_w_down', l], tb=True)
        else:
            dact = mm_send("ffn_down_dx", gcur, weight['ffn_w_down', l], *pending, tb=True)
        dw_down = matmul("ffn_down_dw", sv['actf'], gcur, **dw_opts)
        dg, dv, dwg, dwv, dbg, dbv = ffn_act_bwd(f"ffn_act_bwd", sv['u0'], fcw, row(ffn_conv_b[l]), dact)
        kf = fcw.shape[0]
        gl['ffn_conv_w'][l] = jnp.concatenate([dwg[:kf], dwv[:kf]], axis=1)
        gl['ffn_conv_b'][l] = jnp.concatenate([dbg, dbv], axis=1)[0]
        wcol = _divisor(half, 512, 128)
        du0 = jnp.concatenate([conv_t_call(f"ffn_convt_g", dg, fcw, 0, wcol),
                               conv_t_call(f"ffn_convt_v", dv, fcw, half // wcol, wcol)], axis=1)
        dh2 = mm_send("ffn_up_dx", du0, weight['ffn_w_up', l], ('ffn_w_down', l), dw_down, tb=True)
        dw_up = matmul("ffn_up_dw", sv['h2'], du0, **dw_opts)
        gmid, dnw = rms_bwd(f"rms_ffn_bwd", sv['x1'], row(norm_ffn_w[l]), dh2, gcur)
        gl['norm_ffn_w'][l] = dnw[0]
        if l % 2 == 0:
            e = l // 2
            proj, lru_p = sv['proj'], sv['lru_p']
            dy = matmul("ev_out_dx", gmid, weight[k_out], tb=True)
            dw_out = matmul("ev_out_dw", sv['y'], gmid, **dw_opts)
            dhs, dga, dob, dgb, dhn = even_post_bwd(f"even_post_bwd", proj, sv['hseq'], sv['ob'], row(hg_norm_w[e]), dy)
            gl['hg_norm_w'][e] = dhn[0]
            dq, df, di, dlb = hg_bwd(f"hg_bwd", proj, row(lbs[e]), sv['states'], dob, lru_w)
            d_lbs[e] = dlb
            lamb = lin_scan(f"lru_scan_bwd", sv['a'], dhs, True)
            dxc, dcw, dcb, dwr, dbr, dwi, dbi, dlam = lru_pre_bwd(f"lru_pre_bwd", proj, *lru_p, sv['xc'], sv['hseq'], lamb)
            nb = lru_w_r.shape[1]
            gl['lru_conv_w'][e], gl['lru_conv_b'][e] = dcw[:lru_p[0].shape[0]], dcb[0]
            gl['lru_w_r'][e], gl['lru_b_r'][e] = _diag_blocks(dwr, nb), dbr[0]
            gl['lru_w_i'][e], gl['lru_b_i'][e] = _diag_blocks(dwi, nb), dbi[0]
            gl['lru_lambda'][e] = dlam[0]
            dxa = conv_t_call(f"lru_convt", dxc, lru_p[0], 0, lru_w)
            dproj = jnp.concatenate([dxa, dga, dq, df, di, dgb], axis=1)
            dh = mm_send("ev_in_dx", dproj, weight[k_in], ('ffn_w_up', l), dw_up, tb=True)
            pending = (k_in, mm_send("ev_in_dw", sv['h'], dproj, k_out, dw_out, **dw_opts))
        else:
            o = l // 2
            zx, scw = sv['zx'], full['ssd_conv_w'][o]
            dyn = matmul("ssd_out_dx", gmid, weight[k_out], tb=True)
            dw_out = matmul("ssd_out_dw", sv['yn'], gmid, **dw_opts)
            dys, dz, dnw = ssd_post_bwd(f"ssd_post_bwd", sv['ys'], zx, row(full['ssd_norm_w'][o]), dyn, groups)
            gl['ssd_norm_w'][o] = dnw[0]
            dxs, dbm, dcm, ddt, dcum, dcr, ddexp = ssd_bwd(f"ssd_bwd", sv['act'], *sv['prep'], sv['ssd_p'][2],
                                                          sv['states'], dys, inner, groups)
            ddtr, ddtb, dalog = ssd_prep_bwd("ssd_prep_bwd", sv['dtr'], sv['ssd_p'][0], sv['ssd_p'][1], ddt, dcum, dcr, inner)
            gl['ssd_dt_bias'][o], gl['ssd_a_log'][o] = ddtb[0, :heads], dalog[0, :heads]
            gl['ssd_d'][o] = jnp.sum(ddexp.reshape(heads, HEADDIM), axis=1)
            dact = jnp.concatenate([dxs, dbm, dcm], axis=1)
            dpre, dcw, dcb = ssd_conv_bwd(f"ssd_conv_bwd", zx, scw, row(full['ssd_conv_b'][o]), dact, inner)
            gl['ssd_conv_w'][o], gl['ssd_conv_b'][o] = dcw[:scw.shape[0]], dcb[0]
            dxbc = conv_t_call(f"ssd_convt", dpre, scw, 0, _ssd_conv_width(inner, cdim))
            dzx = jnp.concatenate([dz, dxbc], axis=1)
            dh = mm_send("ssd_in_dx", dzx, sv['wzx'], ('ffn_w_up', l), dw_up, tb=True)
            dh = matmul("ssd_dt_dx", ddtr, sv['wdt'], tb=True, res=dh)
            dwzx = mm_send("ssd_in_dw", sv['h'], dzx, k_out, dw_out, **dw_opts)
            dwdt = matmul("ssd_dt_dw", sv['h'], ddtr, **dw_opts)
            pending = (k_in, jnp.concatenate([dwzx, dwdt[:, :heads]], axis=1))
        gcur, dnw = rms_bwd(f"rms_mix_bwd", sv['x0'], row(norm_mix_w[l]), dh, gmid)
        gl['norm_mix_w'][l] = dnw[0]

    def lbs_bwd(hb, dl):
        _, vjp = jax.vjp(f_lbs, hb)
        return [vjp(dl)[0]]
    d_hlb = whole_call("lbs_bwd", lbs_bwd, [hg_lower_bounds, jnp.concatenate(d_lbs, axis=0)], [hg_lower_bounds.shape])[0]

    part = {n: jnp.stack(v) for n, v in gl.items() if n != 'hg_lower_bounds'}
    part['hg_lower_bounds'] = d_hlb
    part['norm_final_w'] = d_nfw[0]

    slots[pending[0]] = exchange("rs_last", pieces(*pending))
    out = {}
    for n in BIG:
        res = None
        for idx in range(wts[n].shape[0]):
            res = adamw(f"adamw_{n}", slots[n, idx], wts[n], mom1[n], mom2[n], idx, res)
        out[n] = res

    small_buf, _ = _pack([jnp.concatenate([part[n].astype(F32).reshape(-1) for n in SMALL])])
    sm = all_gather("ag_small_grads", small_buf).reshape(N_DEV, -1)
    own, at = [], 0
    for n in SMALL:
        size = part[n].size
        g8 = sm[:, at:at + size].reshape((N_DEV,) + part[n].shape)
        at += size
        if n in SMALL_SHARDED:
            g8 = lax.dynamic_slice_in_dim(g8, me * wts[n].shape[-1], wts[n].shape[-1], axis=g8.ndim - 1)
        own.append(g8.reshape(N_DEV, -1))
    sslots, _ = _pack([jnp.concatenate(own, axis=1)])
    cat = lambda dct: _pad_to(jnp.concatenate([dct[n].reshape(-1) for n in SMALL]), SEG).reshape(1, -1, 128)
    res = adamw("adamw_small", sslots, cat(wts), cat(mom1), cat(mom2))
    at = 0
    for n in SMALL:
        out[n] = [r.reshape(-1)[at:at + wts[n].size].reshape(wts[n].shape) for r in res]
        at += wts[n].size

    grad_x = gcur.reshape(x.shape)
    return (loss, grad_x, *[out[n][0] for n in WEIGHTS], *[out[n][1] for n in WEIGHTS],
            *[out[n][2] for n in WEIGHTS], *[out[n][3] for n in WEIGHTS])
```

```python
import functools
import math

import jax
import jax.numpy as jnp
from jax import lax
from jax.experimental import pallas as pl
from jax.experimental.pallas import tpu as pltpu

F32 = jnp.float32
BF16 = jnp.bfloat16
MXU = jnp.bfloat16
HI = lax.Precision.HIGHEST

N_DEV = 8
EPS = 1e-6
LRU_C = 8.0
CHUNK = 64
STATE = 128
ADAM_LR, ADAM_B1, ADAM_B2, ADAM_EPS, ADAM_WD, ADAM_STEP = 0.001, 0.9, 0.999, 1e-08, 0.01, 10

SUBLANES = 8
VMEM_LIMIT = 56 * 1024 * 1024

NN = (((1,), (0,)), ((), ()))
NT = (((1,), (1,)), ((), ()))
TN = (((0,), (0,)), ((), ()))


def _params(sem):
    return pltpu.CompilerParams(dimension_semantics=sem, vmem_limit_bytes=VMEM_LIMIT)


def _divisor(n, target, align):
    if n <= target:
        return n
    best = None
    for d in range(align, target + 1, align):
        if n % d == 0:
            best = d
    assert best is not None, (n, target, align)
    return best


def _mm(a, b, dn):
    return lax.dot_general(a.astype(MXU), b.astype(MXU), dn, preferred_element_type=F32)


@jax.custom_vjp
def mm_nn(a, b):
    return _mm(a, b, NN)


@jax.custom_vjp
def mm_nt(a, b):
    return _mm(a, b, NT)


@jax.custom_vjp
def mm_tn(a, b):
    return _mm(a, b, TN)


mm_nn.defvjp(lambda a, b: (_mm(a, b, NN), (a, b)), lambda r, g: (mm_nt(g, r[1]), mm_tn(r[0], g)))
mm_nt.defvjp(lambda a, b: (_mm(a, b, NT), (a, b)), lambda r, g: (mm_nn(g, r[1]), mm_tn(g, r[0])))
mm_tn.defvjp(lambda a, b: (_mm(a, b, TN), (a, b)), lambda r, g: (mm_nt(r[1], g), mm_nn(r[0], g)))


def dot_hi(a, b, dn=NN):
    return lax.dot_general(a, b, dn, precision=HI, preferred_element_type=F32)


def _iota(shape, dim):
    return lax.broadcasted_iota(jnp.int32, shape, dim)


def _tril(n):
    return (_iota((n, n), 0) >= _iota((n, n), 1)).astype(F32)


def _softplus(x):
    return jnp.maximum(x, 0.0) + jnp.log1p(jnp.exp(-jnp.abs(x)))


def _neg_expm1(x):
    series = -x * (1.0 + x * (0.5 + x * (1.0 / 6.0 + x * (1.0 / 24.0))))
    return jnp.where(x > -0.03, series, 1.0 - jnp.exp(x))


_HBM = pl.BlockSpec(memory_space=pltpu.HBM)
_MESH = pl.DeviceIdType.MESH


def _peer_copies(kind, src_ref, dst_ref, send_sems, recv_sems, local_sem):
    x, y, c = lax.axis_index("x"), lax.axis_index("y"), lax.axis_index("c")
    me = 4 * x + 2 * y + c
    pick = (lambda p: src_ref) if kind == "gather" else (lambda p: src_ref.at[p])
    local = pltpu.make_async_copy(pick(me), dst_ref.at[me], local_sem)
    remote = []
    for k in range(N_DEV - 1):
        px, py, pc = (x + ((k + 1) >> 2 & 1)) % 2, (y + ((k + 1) >> 1 & 1)) % 2, (c + ((k + 1) & 1)) % 2
        remote.append(pltpu.make_async_remote_copy(
            src_ref=pick(4 * px + 2 * py + pc), dst_ref=dst_ref.at[me], send_sem=send_sems.at[k],
            recv_sem=recv_sems.at[k], device_id=(px, py, pc), device_id_type=_MESH))
    return local, remote


_PEER_SEMS = [pltpu.SemaphoreType.DMA((N_DEV - 1,)), pltpu.SemaphoreType.DMA((N_DEV - 1,)), pltpu.SemaphoreType.DMA]


def matmul(name, a, b, *, ta=False, tb=False, res=None, out_dtype=F32, tm=1024, tn=1024, tk=2048, rides=()):
    m, k = (a.shape[1], a.shape[0]) if ta else a.shape
    n = b.shape[0] if tb else b.shape[1]
    assert (b.shape[1] if tb else b.shape[0]) == k, (name, a.shape, b.shape)
    tm, tn, tk = _divisor(m, tm, 128), _divisor(n, tn, 128), _divisor(k, tk, 128)
    ni, nj, nk = m // tm, n // tn, k // tk
    dn = (((0 if ta else 1,), (1 if tb else 0,)), ((), ()))
    rides = list(rides)
    nr = len(rides)
    n_in = 2 + (res is not None) + nr
    n_out = 1 + nr

    def body(*refs):
        a_ref, b_ref = refs[0], refs[1]
        r_ref = refs[2] if res is not None else None
        o_ref = refs[n_in]
        scratch = refs[n_in + n_out:]
        sems = scratch[len(scratch) - 3 * nr:]
        i, j, kk = pl.program_id(0), pl.program_id(1), pl.program_id(2)

        def peers(r):
            return _peer_copies(rides[r][0], refs[n_in - nr + r], refs[n_in + 1 + r], *sems[3 * r:3 * r + 3])

        if nr:
            @pl.when((i == 0) & (j == 0) & (kk == 0))
            def _():
                for r in range(nr):
                    local, remote = peers(r)
                    local.start()
                    for cp in remote:
                        cp.start()

        def finish(r):
            if r_ref is not None:
                r = r + r_ref[...]
            o_ref[...] = r.astype(o_ref.dtype)

        if nk == 1:
            finish(_mm(a_ref[...], b_ref[...], dn))
        else:
            acc_ref = scratch[0]

            @pl.when(kk == 0)
            def _():
                acc_ref[...] = jnp.zeros_like(acc_ref)

            acc_ref[...] += _mm(a_ref[...], b_ref[...], dn)

            @pl.when(kk == nk - 1)
            def _():
                finish(acc_ref[...])

        if nr:
            @pl.when((i == ni - 1) & (j == nj - 1) & (kk == nk - 1))
            def _():
                for r in range(nr):
                    local, remote = peers(r)
                    for cp in remote:
                        cp.wait_recv()
                    for cp in remote:
                        cp.wait_send()
                    local.wait()

    a_spec = pl.BlockSpec((tk, tm), lambda i, j, kk: (kk, i)) if ta else pl.BlockSpec((tm, tk), lambda i, j, kk: (i, kk))
    b_spec = pl.BlockSpec((tn, tk), lambda i, j, kk: (j, kk)) if tb else pl.BlockSpec((tk, tn), lambda i, j, kk: (kk, j))
    o_spec = pl.BlockSpec((tm, tn), lambda i, j, kk: (i, j))
    ins, specs = [a, b], [a_spec, b_spec]
    out_specs, out_shape = [o_spec], [jax.ShapeDtypeStruct((m, n), out_dtype)]
    scratch_shapes = [pltpu.VMEM((tm, tn), F32)] if nk > 1 else []
    if res is not None:
        ins.append(res)
        specs.append(o_spec)
    for kind, src in rides:
        ins.append(src)
        specs.append(_HBM)
        out_specs.append(_HBM)
        out_shape.append(jax.ShapeDtypeStruct(src.shape if kind == "exchange" else (N_DEV,) + src.shape, src.dtype))
        scratch_shapes += _PEER_SEMS
    outs = pl.pallas_call(
        body, name=name, grid=(ni, nj, nk), in_specs=specs, out_specs=out_specs, out_shape=out_shape,
        scratch_shapes=scratch_shapes, compiler_params=_params(("arbitrary", "arbitrary", "arbitrary")),
    )(*ins)
    return (outs[0], list(outs[1:])) if nr else outs[0]


def Row(arr, width=None, off=0, var=True):
    return ("row", arr, arr.shape[1] if width is None else width, off, var)


def Prev(arr, width=None, off=0, var=True):
    return ("prev", arr, arr.shape[1] if width is None else width, off, var)


def Next(arr, width=None, off=0, var=True):
    return ("next", arr, arr.shape[1] if width is None else width, off, var)


def Full(arr, width=None, off=0, var=True):
    return ("full", arr, arr.shape[1] if width is None else width, off, var)


def rows_call(name, fn, rows, tile, ncol, ins, outs=(), accs=()):
    nrow = rows // tile
    assert rows % tile == 0 and tile % SUBLANES == 0, (name, rows, tile)
    last8 = rows // SUBLANES - 1
    per8 = tile // SUBLANES

    def spec(kind, arr, width, off, var):
        col = (lambda j: off + j) if var else (lambda j: off)
        if kind == "row":
            return pl.BlockSpec((tile, width), lambda j, i: (i, col(j)))
        if kind == "prev":
            return pl.BlockSpec((SUBLANES, width), lambda j, i: (jnp.maximum(i * per8 - 1, 0), col(j)))
        if kind == "next":
            return pl.BlockSpec((SUBLANES, width), lambda j, i: (jnp.minimum((i + 1) * per8, last8), col(j)))
        return pl.BlockSpec((arr.shape[0], width), lambda j, i: (0, col(j)))

    n_in, n_out = len(ins), len(outs)

    def body(*refs):
        j, i = pl.program_id(0), pl.program_id(1)
        o_tiles, a_tiles = fn(i, j, *[r[...] for r in refs[:n_in]])
        for r, o in zip(refs[n_in:n_in + n_out], o_tiles, strict=True):
            r[...] = o.astype(r.dtype)
        acc_refs = refs[n_in + n_out:]
        if acc_refs:
            @pl.when(i == 0)
            def _():
                for r in acc_refs:
                    r[...] = jnp.zeros_like(r)
            for r, a in zip(acc_refs, a_tiles, strict=True):
                r[...] += a

    out_shape = [jax.ShapeDtypeStruct((rows, w * ncol), dt) for dt, w in outs]
    out_shape += [jax.ShapeDtypeStruct((r, w * ncol), F32) for r, w in accs]
    out_specs = [pl.BlockSpec((tile, w), lambda j, i: (i, j)) for _, w in outs]
    out_specs += [pl.BlockSpec((r, w), lambda j, i: (0, j)) for r, w in accs]
    res = pl.pallas_call(
        body, name=name, grid=(ncol, nrow), in_specs=[spec(*s) for s in ins], out_specs=out_specs,
        out_shape=out_shape, compiler_params=_params(("arbitrary", "arbitrary")),
    )(*[s[1] for s in ins])
    return res


def _shift_down(tile, prev8, s, first):
    if s == 0:
        return tile
    rolled = pltpu.roll(tile, s, 0)
    pr = jnp.where(first, 0.0, pltpu.roll(prev8, s, 0))
    head = jnp.where(_iota(pr.shape, 0) < s, pr, rolled[:SUBLANES])
    return jnp.concatenate([head, rolled[SUBLANES:]], axis=0)


def _shift_up(tile, next8, s, last):
    if s == 0:
        return tile
    t = tile.shape[0]
    rolled = pltpu.roll(tile, t - s, 0)
    nx = jnp.where(last, 0.0, pltpu.roll(next8, SUBLANES - s, 0))
    tail = jnp.where(_iota(nx.shape, 0) >= SUBLANES - s, nx, rolled[t - SUBLANES:])
    return jnp.concatenate([rolled[:t - SUBLANES], tail], axis=0)


def _row(w, k):
    return jnp.sum(jnp.where(_iota(w.shape, 0) == k, w, 0.0), axis=0, keepdims=True)


def _conv(x, prev8, w, b, first):
    kk = w.shape[0]
    y = b + _row(w, kk - 1) * x
    for k in range(kk - 1):
        y = y + _row(w, k) * _shift_down(x, prev8, kk - 1 - k, first)
    return y


def _conv_wgrad(x, prev8, dy, kk, first):
    out = jnp.zeros((SUBLANES, x.shape[1]), F32)
    for k in range(kk):
        r = jnp.sum(dy * _shift_down(x, prev8, kk - 1 - k, first), axis=0, keepdims=True)
        out = out + jnp.where(_iota(out.shape, 0) == k, r, 0.0)
    return out


def _conv_t(dy, next8, w, last):
    kk = w.shape[0]
    dx = _row(w, kk - 1) * dy
    for k in range(kk - 1):
        dx = dx + _row(w, k) * _shift_up(dy, next8, kk - 1 - k, last)
    return dx


def f_rms(x, w):
    return x * lax.rsqrt(jnp.mean(x * x, axis=-1, keepdims=True) + EPS) * w


def rms_fwd(name, x, w):
    def fn(i, j, xt, wt):
        return [f_rms(xt, wt)], []
    return rows_call(name, fn, x.shape[0], _divisor(x.shape[0], 512, 8), 1, [Row(x), Full(w)],
                     outs=[(BF16, x.shape[1])])[0]


def rms_bwd(name, x, w, dh, dres):
    def fn(i, j, xt, wt, dht, drt):
        _, vjp = jax.vjp(f_rms, xt, wt)
        dx, dw = vjp(dht)
        return [drt + dx], [dw]
    d = x.shape[1]
    return rows_call(name, fn, x.shape[0], _divisor(x.shape[0], 256, 8), 1, [Row(x), Full(w), Row(dh), Row(dres)],
                     outs=[(F32, d)], accs=[(1, d)])


def loss_head(name, x, w, target):
    def fn(i, j, xt, wt, tt):
        def f(xx, ww):
            err = f_rms(xx, ww) - tt
            return 0.5 * jnp.mean(err * err, axis=-1, keepdims=True)
        rows, vjp = jax.vjp(f, xt, wt)
        dx, dw = vjp(jnp.ones_like(rows))
        return [dx], [dw, jnp.broadcast_to(jnp.sum(rows, axis=0, keepdims=True), (1, 128))]
    d = x.shape[1]
    return rows_call(name, fn, x.shape[0], _divisor(x.shape[0], 256, 8), 1, [Row(x), Full(w), Row(target)],
                     outs=[(F32, d)], accs=[(1, d), (1, 128)])


def ffn_act_fwd(name, u0, cw, cb):
    t, two_f = u0.shape
    wc = _divisor(two_f // 2, 512, 128)
    nc = two_f // 2 // wc

    def fn(i, j, ug, ugp, uv, uvp, wg, wv, bg, bv):
        first = i == 0
        g = _conv(ug, ugp, wg, bg, first)
        v = _conv(uv, uvp, wv, bv, first)
        return [jax.nn.silu(g) * v], []
    ins = [Row(u0, wc), Prev(u0, wc), Row(u0, wc, nc), Prev(u0, wc, nc),
           Full(cw, wc), Full(cw, wc, nc), Full(cb, wc), Full(cb, wc, nc)]
    return rows_call(name, fn, t, _divisor(t, 1024, 8), nc, ins, outs=[(BF16, wc)])[0]


def ffn_act_bwd(name, u0, cw, cb, dact):
    t, two_f = u0.shape
    wc = _divisor(two_f // 2, 512, 128)
    nc = two_f // 2 // wc
    kk = cw.shape[0]
    tile = _divisor(t, 512, 8)
    last_i = t // tile - 1

    def conv_t(dy, w):
        rows = dy.shape[0]
        dx = _row(w, kk - 1) * dy[:tile]
        for k in range(kk - 1):
            dx = dx + _row(w, k) * pltpu.roll(dy, rows - (kk - 1 - k), 0)[:tile]
        return dx

    def fn(i, j, ug, ugp, ugn, uv, uvp, uvn, wg, wv, bg, bv, da, dan):
        first = i == 0
        ext = lambda a, b: jnp.concatenate([a, b], axis=0)
        g = _conv(ext(ug, ugn), ugp, wg, bg, first)
        v = _conv(ext(uv, uvn), uvp, wv, bv, first)
        _, vjp = jax.vjp(lambda gg, vv: jax.nn.silu(gg) * vv, g, v)
        dg, dv = vjp(ext(da, jnp.where(i == last_i, 0.0, dan)))
        accs = [_conv_wgrad(ug, ugp, dg[:tile], kk, first), _conv_wgrad(uv, uvp, dv[:tile], kk, first),
                jnp.sum(dg[:tile], axis=0, keepdims=True), jnp.sum(dv[:tile], axis=0, keepdims=True)]
        return [conv_t(dg, wg), conv_t(dv, wv)], accs
    ins = [Row(u0, wc), Prev(u0, wc), Next(u0, wc), Row(u0, wc, nc), Prev(u0, wc, nc), Next(u0, wc, nc),
           Full(cw, wc), Full(cw, wc, nc), Full(cb, wc), Full(cb, wc, nc), Row(dact, wc), Next(dact, wc)]
    return rows_call(name, fn, t, tile, nc, ins, outs=[(BF16, wc), (BF16, wc)],
                     accs=[(SUBLANES, wc), (SUBLANES, wc), (1, wc), (1, wc)])


def conv_t_call(name, dy, cw, col_off, width):
    out_dtype = BF16
    t, c = dy.shape
    nc = c // width
    nrow_tile = _divisor(t, 512, 8)
    last_i = t // nrow_tile - 1

    def fn(i, j, d, dn, w):
        return [_conv_t(d, dn, w, i == last_i)], []
    return rows_call(name, fn, t, nrow_tile, nc, [Row(dy, width), Next(dy, width), Full(cw, width, col_off)],
                     outs=[(out_dtype, width)])[0]


def f_lru_gates(xc, wr, br, wi, bi, lam):
    r = jax.nn.sigmoid(mm_nn(xc, wr) + br)
    gi = jax.nn.sigmoid(mm_nn(xc, wi) + bi)
    log_a = -LRU_C * r * _softplus(-lam)
    a = jnp.exp(log_a)
    u = jnp.sqrt(_neg_expm1(2.0 * log_a)) * (gi * xc)
    return a, u


def lru_pre_fwd(name, proj, cw, cb, wr, br, wi, bi, lam):
    t, w = proj.shape[0], lam.shape[1]

    def fn(i, j, xa, xap, cwt, cbt, wrt, brt, wit, bit, lamt):
        xc = _conv(xa, xap, cwt, cbt, i == 0)
        a, u = f_lru_gates(xc, wrt, brt, wit, bit, lamt)
        return [xc, a, u], []
    ins = [Row(proj, w), Prev(proj, w), Full(cw), Full(cb), Full(wr), Full(br), Full(wi), Full(bi), Full(lam)]
    return rows_call(name, fn, t, _divisor(t, 512, 8), 1, ins, outs=[(F32, w)] * 3)


def lru_pre_bwd(name, proj, cw, cb, wr, br, wi, bi, lam, xc, hseq, lamb):
    t, w = proj.shape[0], lam.shape[1]
    kk = cw.shape[0]

    def fn(i, j, xa, xap, xct, hs, hsp, lb, wrt, brt, wit, bit, lamt):
        first = i == 0
        da = lb * _shift_down(hs, hsp, 1, first)
        _, vjp = jax.vjp(f_lru_gates, xct, wrt.astype(F32), brt, wit.astype(F32), bit, lamt)
        dxc, dwr, dbr, dwi, dbi, dlam = vjp((da, lb))
        accs = [_conv_wgrad(xa, xap, dxc, kk, first), jnp.sum(dxc, axis=0, keepdims=True), dwr, dbr, dwi, dbi, dlam]
        return [dxc], accs
    ins = [Row(proj, w), Prev(proj, w), Row(xc), Row(hseq), Prev(hseq), Row(lamb),
           Full(wr), Full(br), Full(wi), Full(bi), Full(lam)]
    return rows_call(name, fn, t, _divisor(t, 256, 8), 1, ins, outs=[(F32, w)],
                     accs=[(SUBLANES, w), (1, w), (w, w), (1, w), (w, w), (1, w), (1, w)])


def lin_scan(name, a, x, reverse):
    t, c = a.shape
    tile = _divisor(t, 512, 8)
    n = t // tile

    def body(a_ref, x_ref, o_ref, c_ref):
        @pl.when(pl.program_id(0) == 0)
        def _():
            c_ref[...] = jnp.zeros_like(c_ref)

        def step(s, carry):
            r = (tile - 1 - s) if reverse else s
            at, xt = a_ref[pl.ds(r, 1), :], x_ref[pl.ds(r, 1), :]
            o = (xt + carry) if reverse else (at * carry + xt)
            o_ref[pl.ds(r, 1), :] = o
            return (at * o) if reverse else o
        c_ref[...] = lax.fori_loop(0, tile, step, c_ref[...], unroll=8)

    spec = pl.BlockSpec((tile, c), (lambda i: (n - 1 - i, 0)) if reverse else (lambda i: (i, 0)))
    return pl.pallas_call(
        body, name=name, grid=(n,), in_specs=[spec, spec], out_specs=spec,
        out_shape=jax.ShapeDtypeStruct((t, c), F32), scratch_shapes=[pltpu.VMEM((1, c), F32)],
        compiler_params=_params(("arbitrary",)),
    )(a, x)


def _hg_chunk(s, q, fr, v, lb):
    f = lb + (1.0 - lb) * jax.nn.sigmoid(fr)
    k = 1.0 - f
    g = jnp.log(f)
    qs = jax.nn.silu(q) * (STATE ** -0.5)
    cum = dot_hi(_tril(CHUNK), g)
    tot = jnp.sum(g, axis=0, keepdims=True)
    mid = jnp.sum(jnp.where(_iota(g.shape, 0) < CHUNK // 2, g, 0.0), axis=0, keepdims=True)
    scores = mm_nt(qs * jnp.exp(cum - mid), k * jnp.exp(mid - cum))
    scores = jnp.where(_tril(CHUNK) > 0, scores, 0.0)
    o = mm_nn(scores, v) + mm_nn(qs * jnp.exp(cum), s)
    decay = jnp.broadcast_to(jnp.exp(tot), s.shape).T
    s_new = decay * s + mm_tn(k * jnp.exp(tot - cum), v)
    return o, s_new


def _hg_specs(proj, heads, lru_w, hg_w, rows, rev):
    nblk = proj.shape[0] // rows
    blk = (lambda b: nblk - 1 - b) if rev else (lambda b: b)
    base = 2 * lru_w // STATE
    per = hg_w // STATE
    col = [pl.BlockSpec((rows, STATE), functools.partial(lambda h, b, o: (blk(b), o + h), o=base + k * per))
           for k in range(3)]
    return nblk, blk, col


def hg_fwd(name, proj, lbs, lru_w, cb=4):
    t, hg_w = proj.shape[0], lbs.shape[1]
    heads = hg_w // STATE
    cb = min(cb, t // CHUNK)
    rows = cb * CHUNK
    nblk, blk, col = _hg_specs(proj, heads, lru_w, hg_w, rows, False)

    def body(q_ref, f_ref, v_ref, lb_ref, o_ref, s_ref, st):
        @pl.when(pl.program_id(1) == 0)
        def _():
            st[...] = jnp.zeros_like(st)
        s = st[...]
        for c in range(cb):
            sl = slice(c * CHUNK, (c + 1) * CHUNK)
            s_ref[c] = s
            o, s = _hg_chunk(s, q_ref[sl, :], f_ref[sl, :], v_ref[sl, :], lb_ref[...])
            o_ref[sl, :] = o
        st[...] = s

    return pl.pallas_call(
        body, name=name, grid=(heads, nblk),
        in_specs=col + [pl.BlockSpec((1, STATE), lambda h, b: (0, h))],
        out_specs=[pl.BlockSpec((rows, STATE), lambda h, b: (b, h)),
                   pl.BlockSpec((cb, None, STATE, STATE), lambda h, b: (b, h, 0, 0))],
        out_shape=[jax.ShapeDtypeStruct((t, hg_w), F32),
                   jax.ShapeDtypeStruct((t // CHUNK, heads, STATE, STATE), F32)],
        scratch_shapes=[pltpu.VMEM((STATE, STATE), F32)],
        compiler_params=_params(("arbitrary", "arbitrary")),
    )(proj, proj, proj, lbs)


def hg_bwd(name, proj, lbs, states, do, lru_w, cb=4):
    t, hg_w = proj.shape[0], lbs.shape[1]
    heads = hg_w // STATE
    cb = min(cb, t // CHUNK)
    rows = cb * CHUNK
    nblk, blk, col = _hg_specs(proj, heads, lru_w, hg_w, rows, True)

    def body(q_ref, f_ref, v_ref, lb_ref, s_ref, do_ref, dq_ref, df_ref, dv_ref, dlb_ref, dst):
        @pl.when(pl.program_id(1) == 0)
        def _():
            dst[...] = jnp.zeros_like(dst)
            dlb_ref[...] = jnp.zeros_like(dlb_ref)
        ds = dst[...]
        dlb = jnp.zeros((1, STATE), F32)
        for c in reversed(range(cb)):
            sl = slice(c * CHUNK, (c + 1) * CHUNK)
            _, vjp = jax.vjp(_hg_chunk, s_ref[c], q_ref[sl, :], f_ref[sl, :], v_ref[sl, :], lb_ref[...])
            ds, dq, df, dv, dl = vjp((do_ref[sl, :], ds))
            dq_ref[sl, :] = dq.astype(dq_ref.dtype)
            df_ref[sl, :] = df.astype(df_ref.dtype)
            dv_ref[sl, :] = dv.astype(dv_ref.dtype)
            dlb = dlb + dl
        dst[...] = ds
        dlb_ref[...] += dlb

    rspec = pl.BlockSpec((rows, STATE), lambda h, b: (blk(b), h))
    return pl.pallas_call(
        body, name=name, grid=(heads, nblk),
        in_specs=col + [pl.BlockSpec((1, STATE), lambda h, b: (0, h)),
                        pl.BlockSpec((cb, None, STATE, STATE), lambda h, b: (blk(b), h, 0, 0)), rspec],
        out_specs=[rspec, rspec, rspec, pl.BlockSpec((1, STATE), lambda h, b: (0, h))],
        out_shape=[jax.ShapeDtypeStruct((t, hg_w), BF16)] * 3 + [jax.ShapeDtypeStruct((1, hg_w), F32)],
        scratch_shapes=[pltpu.VMEM((STATE, STATE), F32)],
        compiler_params=_params(("arbitrary", "arbitrary")),
    )(proj, proj, proj, lbs, states, do)


def f_even_post(hseq, ga, ob, gb, nw):
    parts = [hseq * jax.nn.gelu(ga)]
    for h in range(ob.shape[1] // STATE):
        o = ob[:, h * STATE:(h + 1) * STATE]
        on = o * lax.rsqrt(jnp.mean(o * o, axis=-1, keepdims=True) + EPS) * nw
        parts.append(on * jax.nn.silu(gb[:, h * STATE:(h + 1) * STATE]))
    return jnp.concatenate(parts, axis=-1)


def _even_post_ins(proj, hseq, ob, nw):
    w, v = hseq.shape[1], ob.shape[1]
    assert w == v
    return [Row(hseq), Row(proj, w, 1), Row(ob), Row(proj, v, (2 * w + 3 * v) // v), Full(nw)]


def even_post_fwd(name, proj, hseq, ob, nw):
    t = proj.shape[0]

    def fn(i, j, hs, ga, o, gb, nwt):
        return [f_even_post(hs, ga, o, gb, nwt)], []
    return rows_call(name, fn, t, _divisor(t, 256, 8), 1, _even_post_ins(proj, hseq, ob, nw),
                     outs=[(BF16, hseq.shape[1] + ob.shape[1])])[0]


def even_post_bwd(name, proj, hseq, ob, nw, dy):
    t, w, v = proj.shape[0], hseq.shape[1], ob.shape[1]

    def fn(i, j, hs, ga, o, gb, nwt, dyt):
        _, vjp = jax.vjp(f_even_post, hs, ga, o, gb, nwt)
        dhs, dga, dob, dgb, dnw = vjp(dyt)
        return [dhs, dga, dob, dgb], [dnw]
    return rows_call(name, fn, t, _divisor(t, 256, 8), 1, _even_post_ins(proj, hseq, ob, nw) + [Row(dy)],
                     outs=[(F32, w), (BF16, w), (F32, v), (BF16, v)], accs=[(1, STATE)])


def f_lbs(hb):
    e = jnp.exp(hb - jnp.max(hb, axis=0, keepdims=True))
    p = e / jnp.sum(e, axis=0, keepdims=True)
    out, run = jnp.zeros_like(p), jnp.zeros_like(p[:1])
    for r in range(hb.shape[0]):
        run = run + _row(p, r)
        out = out + jnp.where(_iota(p.shape, 0) == r, run - _row(p, 0), 0.0)
    return out


def whole_call(name, fn, ins, out_shapes):
    def body(*refs):
        outs = fn(*[r[...] for r in refs[:len(ins)]])
        for r, o in zip(refs[len(ins):], outs, strict=True):
            r[...] = o
    return pl.pallas_call(body, name=name, out_shape=[jax.ShapeDtypeStruct(s, F32) for s in out_shapes])(*ins)


HEADDIM = 64


def f_ssd_prep(dtr, dtb, alog, inner):
    rows = dtr.shape[0]
    dt_all = _softplus(dtr + dtb)
    da_all = dt_all * (-jnp.exp(alog))
    tril = _tril(CHUNK)
    cums = [dot_hi(tril, da_all[c * CHUNK:(c + 1) * CHUNK]) for c in range(rows // CHUNK)]
    cum_all = jnp.concatenate(cums, axis=0) if len(cums) > 1 else cums[0]
    head_of = _iota((STATE, inner), 1) - HEADDIM * _iota((STATE, inner), 0)
    spread = ((head_of >= 0) & (head_of < HEADDIM)).astype(F32)
    even = (_iota((CHUNK, STATE), 1) == 2 * _iota((CHUNK, STATE), 0)).astype(F32)
    odd = (_iota((CHUNK, STATE), 1) == 2 * _iota((CHUNK, STATE), 0) + 1).astype(F32)
    left = _iota((CHUNK, STATE), 1) < HEADDIM
    cumrows = []
    for cm in cums:
        twice = jnp.concatenate([cm, cm], axis=0)
        cumrows.append(jnp.where(left, dot_hi(even, twice, NT), dot_hi(odd, twice, NT)))
    return dot_hi(dt_all, spread), dot_hi(cum_all, spread), tuple(cumrows)


def ssd_prep_fwd(name, dtr, dtb, alog, inner, k=4):
    t = dtr.shape[0]
    k = min(k, t // CHUNK)
    rows = k * CHUNK

    def body(dtr_ref, dtb_ref, alog_ref, dt_ref, cum_ref, cr_ref):
        dt, cum, crs = f_ssd_prep(dtr_ref[...], dtb_ref[...], alog_ref[...], inner)
        dt_ref[...] = dt
        cum_ref[...] = cum
        for c, cr in enumerate(crs):
            cr_ref[c] = cr

    one = pl.BlockSpec((1, STATE), lambda i: (0, 0))
    wide = pl.BlockSpec((rows, inner), lambda i: (i, 0))
    return pl.pallas_call(
        body, name=name, grid=(t // rows,),
        in_specs=[pl.BlockSpec((rows, STATE), lambda i: (i, 0)), one, one],
        out_specs=[wide, wide, pl.BlockSpec((k, CHUNK, STATE), lambda i: (i, 0, 0))],
        out_shape=[jax.ShapeDtypeStruct((t, inner), F32), jax.ShapeDtypeStruct((t, inner), F32),
                   jax.ShapeDtypeStruct((t // CHUNK, CHUNK, STATE), F32)],
        compiler_params=_params(("arbitrary",)),
    )(dtr, dtb, alog)


def ssd_prep_bwd(name, dtr, dtb, alog, ddt, dcum, dcr, inner, k=4):
    t = dtr.shape[0]
    k = min(k, t // CHUNK)
    rows = k * CHUNK

    def body(dtr_ref, dtb_ref, alog_ref, ddt_ref, dcum_ref, dcr_ref, ddtr_ref, ddtb_ref, dalog_ref):
        @pl.when(pl.program_id(0) == 0)
        def _():
            ddtb_ref[...] = jnp.zeros_like(ddtb_ref)
            dalog_ref[...] = jnp.zeros_like(dalog_ref)
        _, vjp = jax.vjp(functools.partial(f_ssd_prep, inner=inner), dtr_ref[...], dtb_ref[...], alog_ref[...])
        ddtr, ddtb, dalog = vjp((ddt_ref[...], dcum_ref[...], tuple(dcr_ref[c] for c in range(k))))
        ddtr_ref[...] = ddtr
        ddtb_ref[...] += ddtb
        dalog_ref[...] += dalog

    one = pl.BlockSpec((1, STATE), lambda i: (0, 0))
    wide = pl.BlockSpec((rows, inner), lambda i: (i, 0))
    tall = pl.BlockSpec((rows, STATE), lambda i: (i, 0))
    return pl.pallas_call(
        body, name=name, grid=(t // rows,),
        in_specs=[tall, one, one, wide, wide, pl.BlockSpec((k, CHUNK, STATE), lambda i: (i, 0, 0))],
        out_specs=[tall, one, one],
        out_shape=[jax.ShapeDtypeStruct((t, STATE), F32), jax.ShapeDtypeStruct((1, STATE), F32),
                   jax.ShapeDtypeStruct((1, STATE), F32)],
        compiler_params=_params(("arbitrary",)),
    )(dtr, dtb, alog, ddt, dcum, dcr)


def _ssd_group(states, x, bm, cm, dt, cum, cumrs, dsk):
    half = _iota((CHUNK, STATE), 1) >= HEADDIM
    pos = _iota((CHUNK, STATE), 1) - jnp.where(half, HEADDIM, 0)
    row = _iota((CHUNK, STATE), 0)
    causal = row >= pos
    cb2 = mm_nt(cm, jnp.concatenate([bm, bm], axis=0))
    ys, new = [], []
    for pp, (s, cumr) in enumerate(zip(states, cumrs, strict=True)):
        lanes = slice(pp * STATE, (pp + 1) * STATE)
        cu, xdt = cum[:, lanes], x[:, lanes] * dt[:, lanes]
        tot = jnp.sum(jnp.where(row == CHUNK - 1, cu, 0.0), axis=0, keepdims=True)
        m = jnp.where(causal, cb2 * jnp.exp(jnp.where(causal, cu - cumr, 0.0)), 0.0)
        x2 = jnp.concatenate([jnp.where(half, 0.0, xdt), jnp.where(half, xdt, 0.0)], axis=0)
        ys.append(mm_nn(m, x2) + mm_nt(cm, s) * jnp.exp(cu) + x[:, lanes] * dsk[:, lanes])
        decay = jnp.broadcast_to(jnp.exp(tot), s.shape).T
        new.append(decay * s + mm_tn(xdt * jnp.exp(tot - cu), bm))
    return jnp.concatenate(ys, axis=1), tuple(new)


SSD_CHUNKS_PER_STEP = 4


def _ssd_specs(act, inner, groups, rev):
    t = act.shape[0]
    cb = min(SSD_CHUNKS_PER_STEP, t // CHUNK)
    rows = cb * CHUNK
    nblk = t // rows
    ch = (lambda c: nblk - 1 - c) if rev else (lambda c: c)
    gw = inner // groups
    wide = pl.BlockSpec((rows, gw), lambda c, g: (ch(c), g))
    specs = [wide,
             pl.BlockSpec((rows, STATE), lambda c, g: (ch(c), inner // STATE + g)),
             pl.BlockSpec((rows, STATE), lambda c, g: (ch(c), inner // STATE + groups + g)),
             wide, wide,
             pl.BlockSpec((cb, CHUNK, STATE), lambda c, g: (ch(c), 0, 0)),
             pl.BlockSpec((1, gw), lambda c, g: (0, g))]
    return cb, nblk, ch, gw, specs


def ssd_fwd(name, act, dt, cum, cumrow, dexp, inner, groups):
    t = act.shape[0]
    cb, nblk, ch, gw, specs = _ssd_specs(act, inner, groups, False)
    pairs = gw // (2 * HEADDIM)

    def body(x_ref, b_ref, c_ref, dt_ref, cum_ref, cr_ref, dsk_ref, y_ref, sv_ref, st):
        c, g = pl.program_id(0), pl.program_id(1)

        @pl.when(c == 0)
        def _():
            for pp in range(pairs):
                st[g * pairs + pp] = jnp.zeros((STATE, STATE), F32)

        states = tuple(st[g * pairs + pp] for pp in range(pairs))
        for k in range(cb):
            rs = slice(k * CHUNK, (k + 1) * CHUNK)
            cumrs = tuple(cr_ref[k, pl.ds(g * pairs + pp, 1), :] for pp in range(pairs))
            for pp in range(pairs):
                sv_ref[k, pp] = states[pp]
            y, states = _ssd_group(states, x_ref[rs, :], b_ref[rs, :], c_ref[rs, :], dt_ref[rs, :], cum_ref[rs, :],
                                   cumrs, dsk_ref[...])
            y_ref[rs, :] = y
        for pp in range(pairs):
            st[g * pairs + pp] = states[pp]

    return pl.pallas_call(
        body, name=name, grid=(nblk, groups), in_specs=specs,
        out_specs=[pl.BlockSpec((cb * CHUNK, gw), lambda c, g: (c, g)),
                   pl.BlockSpec((cb, pairs, STATE, STATE), lambda c, g: (c, g, 0, 0))],
        out_shape=[jax.ShapeDtypeStruct((t, inner), F32),
                   jax.ShapeDtypeStruct((t // CHUNK, groups * pairs, STATE, STATE), F32)],
        scratch_shapes=[pltpu.VMEM((groups * pairs, STATE, STATE), F32)],
        compiler_params=_params(("arbitrary", "arbitrary")),
    )(act, act, act, dt, cum, cumrow, dexp)


def ssd_bwd(name, act, dt, cum, cumrow, dexp, states, dy, inner, groups):
    t = act.shape[0]
    cb, nblk, ch, gw, specs = _ssd_specs(act, inner, groups, True)
    pairs = gw // (2 * HEADDIM)

    def body(x_ref, b_ref, c_ref, dt_ref, cum_ref, cr_ref, dsk_ref, sv_ref, dy_ref,
             dx_ref, db_ref, dc_ref, ddt_ref, dcum_ref, dcr_ref, ddsk_ref, dst):
        c, g = pl.program_id(0), pl.program_id(1)

        @pl.when((c == 0) & (g == 0))
        def _():
            ddsk_ref[...] = jnp.zeros_like(ddsk_ref)

        @pl.when(c == 0)
        def _():
            for pp in range(pairs):
                dst[g * pairs + pp] = jnp.zeros((STATE, STATE), F32)

        @pl.when(g == 0)
        def _():
            dcr_ref[...] = jnp.zeros_like(dcr_ref)

        ds = tuple(dst[g * pairs + pp] for pp in range(pairs))
        ddsk = jnp.zeros((1, gw), F32)
        for k in reversed(range(cb)):
            rs = slice(k * CHUNK, (k + 1) * CHUNK)
            cumrs = tuple(cr_ref[k, pl.ds(g * pairs + pp, 1), :] for pp in range(pairs))
            _, vjp = jax.vjp(_ssd_group, tuple(sv_ref[k, pp] for pp in range(pairs)), x_ref[rs, :], b_ref[rs, :],
                             c_ref[rs, :], dt_ref[rs, :], cum_ref[rs, :], cumrs, dsk_ref[...])
            ds, dx, db, dc, ddt, dcum, dcrs, ddskk = vjp((dy_ref[rs, :], ds))
            for pp in range(pairs):
                dcr_ref[k, pl.ds(g * pairs + pp, 1), :] = dcrs[pp]
            dx_ref[rs, :] = dx
            db_ref[rs, :] = db
            dc_ref[rs, :] = dc
            ddt_ref[rs, :] = ddt
            dcum_ref[rs, :] = dcum
            ddsk = ddsk + ddskk
        for pp in range(pairs):
            dst[g * pairs + pp] = ds[pp]
        col = pl.ds(pl.multiple_of(g * gw, STATE), gw)
        ddsk_ref[:, col] = ddsk_ref[:, col] + ddsk

    wide = pl.BlockSpec((cb * CHUNK, gw), lambda c, g: (ch(c), g))
    grp = pl.BlockSpec((cb * CHUNK, STATE), lambda c, g: (ch(c), g))
    return pl.pallas_call(
        body, name=name, grid=(nblk, groups),
        in_specs=specs + [pl.BlockSpec((cb, pairs, STATE, STATE), lambda c, g: (ch(c), g, 0, 0)), wide],
        out_specs=[wide, grp, grp, wide, wide, pl.BlockSpec((cb, CHUNK, STATE), lambda c, g: (ch(c), 0, 0)),
                   pl.BlockSpec((1, inner), lambda c, g: (0, 0))],
        out_shape=[jax.ShapeDtypeStruct((t, inner), F32), jax.ShapeDtypeStruct((t, groups * STATE), F32),
                   jax.ShapeDtypeStruct((t, groups * STATE), F32), jax.ShapeDtypeStruct((t, inner), F32),
                   jax.ShapeDtypeStruct((t, inner), F32), jax.ShapeDtypeStruct((t // CHUNK, CHUNK, STATE), F32),
                   jax.ShapeDtypeStruct((1, inner), F32)],
        scratch_shapes=[pltpu.VMEM((groups * pairs, STATE, STATE), F32)],
        compiler_params=_params(("arbitrary", "arbitrary")),
    )(act, act, act, dt, cum, cumrow, dexp, states, dy)


def _ssd_conv_width(inner, cdim):
    return _divisor(math.gcd(inner, cdim), 2048, 128)


def ssd_conv_fwd(name, zx, cw, cb, inner):
    t, cdim = zx.shape[0], cw.shape[1]
    wc = _ssd_conv_width(inner, cdim)

    def fn(i, j, xt, xp, w, b):
        return [jax.nn.silu(_conv(xt, xp, w, b, i == 0))], []
    ins = [Row(zx, wc, inner // wc), Prev(zx, wc, inner // wc), Full(cw, wc), Full(cb, wc)]
    return rows_call(name, fn, t, _divisor(t, 256, 8), cdim // wc, ins, outs=[(F32, wc)])[0]


def ssd_conv_bwd(name, zx, cw, cb, dact, inner):
    t, cdim = zx.shape[0], cw.shape[1]
    wc = _ssd_conv_width(inner, cdim)
    kk = cw.shape[0]

    def fn(i, j, xt, xp, w, b, da):
        first = i == 0
        pre = _conv(xt, xp, w, b, first)
        _, vjp = jax.vjp(jax.nn.silu, pre)
        dpre, = vjp(da)
        return [dpre], [_conv_wgrad(xt, xp, dpre, kk, first), jnp.sum(dpre, axis=0, keepdims=True)]
    ins = [Row(zx, wc, inner // wc), Prev(zx, wc, inner // wc), Full(cw, wc), Full(cb, wc), Row(dact, wc)]
    return rows_call(name, fn, t, _divisor(t, 256, 8), cdim // wc, ins, outs=[(F32, wc)],
                     accs=[(SUBLANES, wc), (1, wc)])


def f_ssd_post(y, z, nw):
    yz = y * jax.nn.silu(z)
    return yz * lax.rsqrt(jnp.mean(yz * yz, axis=-1, keepdims=True) + EPS) * nw


def ssd_post_fwd(name, y, zx, nw, groups):
    t, inner = y.shape
    gw = inner // groups

    def fn(i, j, yt, zt, nwt):
        return [f_ssd_post(yt, zt, nwt)], []
    return rows_call(name, fn, t, _divisor(t, 1024, 8), groups, [Row(y, gw), Row(zx, gw), Full(nw, gw)],
                     outs=[(BF16, gw)])[0]


def ssd_post_bwd(name, y, zx, nw, dyn, groups):
    t, inner = y.shape
    gw = inner // groups

    def fn(i, j, yt, zt, nwt, dt):
        _, vjp = jax.vjp(f_ssd_post, yt, zt, nwt)
        dy, dz, dnw = vjp(dt)
        return [dy, dz], [dnw]
    return rows_call(name, fn, t, _divisor(t, 512, 8), groups, [Row(y, gw), Row(zx, gw), Full(nw, gw), Row(dyn, gw)],
                     outs=[(F32, gw), (BF16, gw)], accs=[(1, gw)])


ADAMW_TILE = 256 * 1024


def adamw(name, slots, w, m, v, layer=0, prev=None):
    nl, n, c = w.shape
    tr = _divisor(n, max(16, ADAMW_TILE // c // 16 * 16), 16)

    def body(s_ref, w_ref, m_ref, v_ref, *rest):
        g_ref, d_ref, mo_ref, vo_ref = rest[-4:]
        g = s_ref[0].astype(F32)
        for k in range(1, N_DEV):
            g = g + s_ref[k].astype(F32)
        mn = ADAM_B1 * m_ref[...] + (1.0 - ADAM_B1) * g
        vn = ADAM_B2 * v_ref[...] + (1.0 - ADAM_B2) * (g * g)
        m_hat = mn / (1.0 - ADAM_B1 ** ADAM_STEP)
        v_hat = vn / (1.0 - ADAM_B2 ** ADAM_STEP)
        g_ref[...] = g
        d_ref[...] = -ADAM_LR * (m_hat / (jnp.sqrt(v_hat) + ADAM_EPS) + ADAM_WD * w_ref[...])
        mo_ref[...] = mn
        vo_ref[...] = vn

    spec = pl.BlockSpec((None, tr, c), lambda i: (layer, i, 0))
    prev = list(prev) if prev is not None else []
    return pl.pallas_call(
        body, name=name, grid=(n // tr,),
        in_specs=[pl.BlockSpec((N_DEV, tr, c), lambda i: (0, i, 0)), spec, spec, spec] + [_HBM] * len(prev),
        out_specs=[spec] * 4, out_shape=[jax.ShapeDtypeStruct((nl, n, c), F32)] * 4,
        input_output_aliases={4 + k: k for k in range(len(prev))},
        compiler_params=_params(("parallel",)),
    )(slots, w, m, v, *prev)


def all_gather(name, shard):
    def body(x_ref, out_ref, send_sems, recv_sems, local_sem):
        x, y, c = lax.axis_index("x"), lax.axis_index("y"), lax.axis_index("c")
        me, sibling = (x, y, c), (x, y, 1 - c)
        chips = [(1 - x, y), (x, 1 - y), (1 - x, 1 - y)]

        def slab(px, py, pc):
            return out_ref.at[4 * px + 2 * py + pc]

        def copy(k, block, to, src=None):
            return pltpu.make_async_remote_copy(
                src_ref=slab(*block) if src is None else src, dst_ref=slab(*block),
                send_sem=send_sems.at[k], recv_sem=recv_sems.at[k], device_id=to, device_id_type=_MESH)

        mine = pltpu.make_async_copy(x_ref, slab(*me), local_sem)
        mine.start()
        first = [copy(0, me, sibling, src=x_ref)]
        first += [copy(1 + j, me, (*chip, c), src=x_ref) for j, chip in enumerate(chips)]
        for cp in first:
            cp.start()
        passed = [copy(4 + j, (*chip, c), sibling) for j, chip in enumerate(chips)]
        for j, chip in enumerate(chips):
            copy(1 + j, (*chip, c), me).wait_recv()
            passed[j].start()
        copy(0, sibling, me).wait_recv()
        for j, chip in enumerate(chips):
            copy(4 + j, (*chip, 1 - c), me).wait_recv()
        for cp in first + passed:
            cp.wait_send()
        mine.wait()

    return pl.pallas_call(
        body, name=name, out_shape=jax.ShapeDtypeStruct((N_DEV,) + shard.shape, shard.dtype),
        in_specs=[_HBM], out_specs=_HBM,
        scratch_shapes=[pltpu.SemaphoreType.DMA((7,)), pltpu.SemaphoreType.DMA((7,)), pltpu.SemaphoreType.DMA],
    )(shard)


def exchange(name, pieces):
    def body(p_ref, out_ref, send_sems, recv_sems, local_sem):
        local, remote = _peer_copies("exchange", p_ref, out_ref, send_sems, recv_sems, local_sem)
        local.start()
        for cp in remote:
            cp.start()
        for cp in remote:
            cp.wait_recv()
        for cp in remote:
            cp.wait_send()
        local.wait()

    return pl.pallas_call(
        body, name=name, out_shape=jax.ShapeDtypeStruct(pieces.shape, pieces.dtype),
        in_specs=[_HBM], out_specs=_HBM, scratch_shapes=_PEER_SEMS,
    )(pieces)


WEIGHTS = ['norm_mix_w', 'norm_ffn_w', 'norm_final_w', 'ev_w_in', 'lru_conv_w', 'lru_conv_b', 'lru_w_r', 'lru_b_r',
           'lru_w_i', 'lru_b_i', 'lru_lambda', 'hg_lower_bounds', 'hg_norm_w', 'ev_w_out', 'ssd_w_in', 'ssd_conv_w',
           'ssd_conv_b', 'ssd_dt_bias', 'ssd_a_log', 'ssd_d', 'ssd_norm_w', 'ssd_w_out', 'ffn_w_up', 'ffn_conv_w',
           'ffn_conv_b', 'ffn_w_down']
COL_SHARDED = ['ev_w_in', 'ssd_w_in', 'ffn_w_up']
ROW_SHARDED = ['ev_w_out', 'ssd_w_out', 'ffn_w_down']
BIG = ['ev_w_in', 'ev_w_out', 'ssd_w_out', 'ffn_w_up', 'ffn_w_down', 'ssd_w_in']
SMALL_SHARDED = ['lru_conv_w', 'ssd_conv_w', 'ssd_conv_b', 'ssd_norm_w', 'ffn_conv_w']
SMALL = [n for n in WEIGHTS if n not in BIG]
SEG = 16 * 128


def _pad_to(flat, mult):
    extra = (-flat.shape[-1]) % mult
    if extra == 0:
        return flat
    return jnp.pad(flat, [(0, 0)] * (flat.ndim - 1) + [(0, extra)])


def _pack(segments):
    offs, parts, at = [], [], 0
    for s in segments:
        s = _pad_to(s, SEG)
        offs.append(at)
        at += s.shape[-1]
        parts.append(s)
    buf = jnp.concatenate(parts, axis=-1)
    return buf.reshape(buf.shape[:-1] + (at // 128, 128)), offs


def _unshard_last(g):
    g = jnp.moveaxis(g, 0, -2)
    return g.reshape(g.shape[:-2] + (N_DEV * g.shape[-1],))


def _block_diag(w):
    nb, b, _ = w.shape
    return (w[:, :, None, :] * jnp.eye(nb, dtype=w.dtype)[:, None, :, None]).reshape(nb * b, nb * b)


def _diag_blocks(dense, nb):
    b = dense.shape[0] // nb
    d4 = dense.reshape(nb, b, nb, b)
    return jnp.stack([d4[h, :, h, :] for h in range(nb)])


def kernel(x, norm_mix_w, norm_ffn_w, norm_final_w, ev_w_in, lru_conv_w, lru_conv_b, lru_w_r, lru_b_r, lru_w_i, lru_b_i, lru_lambda, hg_lower_bounds, hg_norm_w, ev_w_out, ssd_w_in, ssd_conv_w, ssd_conv_b, ssd_dt_bias, ssd_a_log, ssd_d, ssd_norm_w, ssd_w_out, ffn_w_up, ffn_conv_w, ffn_conv_b, ffn_w_down, loss_target, m_norm_mix_w, m_norm_ffn_w, m_norm_final_w, m_ev_w_in, m_lru_conv_w, m_lru_conv_b, m_lru_w_r, m_lru_b_r, m_lru_w_i, m_lru_b_i, m_lru_lambda, m_hg_lower_bounds, m_hg_norm_w, m_ev_w_out, m_ssd_w_in, m_ssd_conv_w, m_ssd_conv_b, m_ssd_dt_bias, m_ssd_a_log, m_ssd_d, m_ssd_norm_w, m_ssd_w_out, m_ffn_w_up, m_ffn_conv_w, m_ffn_conv_b, m_ffn_w_down, v_norm_mix_w, v_norm_ffn_w, v_norm_final_w, v_ev_w_in, v_lru_conv_w, v_lru_conv_b, v_lru_w_r, v_lru_b_r, v_lru_w_i, v_lru_b_i, v_lru_lambda, v_hg_lower_bounds, v_hg_norm_w, v_ev_w_out, v_ssd_w_in, v_ssd_conv_w, v_ssd_conv_b, v_ssd_dt_bias, v_ssd_a_log, v_ssd_d, v_ssd_norm_w, v_ssd_w_out, v_ffn_w_up, v_ffn_conv_w, v_ffn_conv_b, v_ffn_w_down):
    given = dict(locals())
    wts = {n: given[n] for n in WEIGHTS}
    mom1 = {n: given["m_" + n] for n in WEIGHTS}
    mom2 = {n: given["v_" + n] for n in WEIGHTS}
    me = 4 * lax.axis_index("x") + 2 * lax.axis_index("y") + lax.axis_index("c")

    depth, d = norm_mix_w.shape
    t = x.shape[1]
    x0 = x.reshape(t, d)
    target = loss_target.reshape(t, d)
    n_even, lru_w = lru_lambda.shape
    hg_w = hg_lower_bounds.shape[1]
    n_odd, heads = ssd_dt_bias.shape
    inner = N_DEV * ssd_norm_w.shape[1]
    cdim = N_DEV * ssd_conv_b.shape[1]
    groups = (cdim - inner) // (2 * STATE)
    assert inner == heads * HEADDIM and heads <= STATE and (inner // groups) % (2 * HEADDIM) == 0

    full = {}
    shard_buf, offs = _pack([wts[n].reshape(-1) for n in SMALL_SHARDED])
    gathered = all_gather("ag_small", shard_buf).reshape(N_DEV, -1)
    for n, off in zip(SMALL_SHARDED, offs, strict=True):
        full[n] = _unshard_last(gathered[:, off:off + wts[n].size].reshape((N_DEV,) + wts[n].shape))
    shard16 = {n: wts[n].astype(BF16) for n in BIG}
    mixer = lambda l: (('ev_w_in', l // 2), ('ev_w_out', l // 2)) if l % 2 == 0 else (('ssd_w_in', l // 2), ('ssd_w_out', l // 2))
    weight = {}

    def arrived(key, g8):
        weight[key] = (jnp.transpose(g8, (1, 0, 2)).reshape(g8.shape[1], -1) if key[0] in COL_SHARDED
                       else g8.reshape(-1, g8.shape[2]))

    def gather_ride(key):
        return ("gather", shard16[key[0]][key[1]])

    for key in (*mixer(0), ('ffn_w_up', 0), ('ffn_w_down', 0)):
        arrived(key, all_gather(f"ag_{key[0]}", shard16[key[0]][key[1]]))
    w_zx = lambda o: weight['ssd_w_in', o][:, :inner + cdim]
    w_dt = lambda o: jnp.pad(weight['ssd_w_in', o][:, inner + cdim:], ((0, 0), (0, STATE - heads)))
    pad_h = lambda a: jnp.pad(a.reshape(1, heads), ((0, 0), (0, STATE - heads)))
    row = lambda a: a.reshape(1, -1)

    lbs = whole_call("lbs_fwd", lambda hb: [f_lbs(hb)], [hg_lower_bounds], [hg_lower_bounds.shape])[0]

    def mm_carry(name, a, b, key, **kw):
        if key is None or key in weight:
            return matmul(name, a, b, **kw)
        out, (g8,) = matmul(name, a, b, rides=[gather_ride(key)], **kw)
        arrived(key, g8)
        return out

    saved = []
    xcur = x0
    for l in range(depth):
        sv = {'x0': xcur}
        k_in, k_out = mixer(l)
        nxt_in, nxt_out = mixer(l + 1) if l + 1 < depth else (None, None)
        h = rms_fwd(f"rms_mix_fwd", xcur, row(norm_mix_w[l]))
        sv['h'] = h
        if l % 2 == 0:
            e = l // 2
            wr = _block_diag(lru_w_r[e]).astype(BF16)
            wi = _block_diag(lru_w_i[e]).astype(BF16)
            lru_p = (full['lru_conv_w'][e], row(lru_conv_b[e]), wr, row(lru_b_r[e]), wi, row(lru_b_i[e]), row(lru_lambda[e]))
            proj = mm_carry("ev_in", h, weight[k_in], ('ffn_w_up', l))
            xc, a, u = lru_pre_fwd(f"lru_pre_fwd", proj, *lru_p)
            hseq = lin_scan(f"lru_scan_fwd", a, u, False)
            ob, states = hg_fwd(f"hg_fwd", proj, row(lbs[e]), lru_w)
            y = even_post_fwd(f"even_post_fwd", proj, hseq, ob, row(hg_norm_w[e]))
            xmid = mm_carry("ev_out", y, weight[k_out], ('ffn_w_down', l), res=xcur)
            sv.update(proj=proj, xc=xc, a=a, hseq=hseq, ob=ob, states=states, y=y, lru_p=lru_p)
        else:
            o = l // 2
            ssd_p = (pad_h(ssd_dt_bias[o]), pad_h(ssd_a_log[o]), row(jnp.repeat(ssd_d[o], HEADDIM)))
            wzx, wdt = w_zx(o), w_dt(o)
            zx = mm_carry("ssd_in", h, wzx, ('ffn_w_up', l))
            dtr = matmul("ssd_dt", h, wdt)
            sv.update(wzx=wzx, wdt=wdt)
            act = ssd_conv_fwd(f"ssd_conv_fwd", zx, full['ssd_conv_w'][o], row(full['ssd_conv_b'][o]), inner)
            prep = ssd_prep_fwd("ssd_prep_fwd", dtr, ssd_p[0], ssd_p[1], inner)
            ys, states = ssd_fwd(f"ssd_fwd", act, *prep, ssd_p[2], inner, groups)
            sv['prep'] = prep
            yn = ssd_post_fwd(f"ssd_post_fwd", ys, zx, row(full['ssd_norm_w'][o]), groups)
            xmid = mm_carry("ssd_out", yn, weight[k_out], ('ffn_w_down', l), res=xcur)
            sv.update(zx=zx, dtr=dtr, act=act, ys=ys, states=states, yn=yn, ssd_p=ssd_p)
        h2 = rms_fwd(f"rms_ffn_fwd", xmid, row(norm_ffn_w[l]))
        u0 = mm_carry("ffn_up", h2, weight['ffn_w_up', l], nxt_in)
        actf = ffn_act_fwd(f"ffn_act_fwd", u0, full['ffn_conv_w'][l], row(ffn_conv_b[l]))
        xcur = mm_carry("ffn_down", actf, weight['ffn_w_down', l], nxt_out, res=xmid)
        sv.update(x1=xmid, h2=h2, u0=u0, actf=actf)
        saved.append(sv)

    gcur, d_nfw, loss_row = loss_head("loss_head", xcur, row(norm_final_w), target)
    loss = lax.psum(loss_row[0, 0], ("x", "y", "c"))

    gl = {n: [None] * wts[n].shape[0] for n in SMALL if n != 'norm_final_w'}
    d_lbs = [None] * n_even
    slots = {}
    dw_opts = dict(ta=True, out_dtype=BF16, tm=2048, tk=1024)

    def pieces(key, dw):
        if key[0] in COL_SHARDED:
            return jnp.transpose(dw.reshape(dw.shape[0], N_DEV, -1), (1, 0, 2))
        return dw.reshape(N_DEV, -1, dw.shape[1])

    def halves(p):
        return p[:, :p.shape[1] // 2], p[:, p.shape[1] // 2:]

    def mm_send(name, a, b, sends, **kw):
        if not sends:
            return matmul(name, a, b, **kw)
        out, got = matmul(name, a, b, rides=[("exchange", p) for _, p in sends], **kw)
        for (key, _), g8 in zip(sends, got, strict=True):
            slots[key] = g8
        return out

    carried = {}
    for l in reversed(range(depth)):
        sv = saved[l]
        k_in, k_out = mixer(l)
        fcw = full['ffn_conv_w'][l]
        dact = mm_send("ffn_down_dx", gcur, weight['ffn_w_down', l], carried.get('down_dx', ()), tb=True)
        dw_down = mm_send("ffn_down_dw", sv['actf'], gcur, carried.get('down_dw', ()), **dw_opts)
        dug, duv, dwg, dwv, dbg, dbv = ffn_act_bwd(f"ffn_act_bwd", sv['u0'], fcw, row(ffn_conv_b[l]), dact)
        kf = fcw.shape[0]
        gl['ffn_conv_w'][l] = jnp.concatenate([dwg[:kf], dwv[:kf]], axis=1)
        gl['ffn_conv_b'][l] = jnp.concatenate([dbg, dbv], axis=1)[0]
        du0 = jnp.concatenate([dug, duv], axis=1)
        dh2 = mm_send("ffn_up_dx", du0, weight['ffn_w_up', l], [(('ffn_w_down', l, 0), pieces(('ffn_w_down', l), dw_down))], tb=True)
        dw_up = mm_send("ffn_up_dw", sv['h2'], du0, carried.get('up_dw', ()), **dw_opts)
        up_a, up_b = halves(pieces(('ffn_w_up', l), dw_up))
        in_dx_sends = [(('ffn_w_up', l, 0), up_a)]
        gmid, dnw = rms_bwd(f"rms_ffn_bwd", sv['x1'], row(norm_ffn_w[l]), dh2, gcur)
        gl['norm_ffn_w'][l] = dnw[0]
        if l % 2 == 0:
            e = l // 2
            proj, lru_p = sv['proj'], sv['lru_p']
            dy = matmul("ev_out_dx", gmid, weight[k_out], tb=True)
            dw_out = matmul("ev_out_dw", sv['y'], gmid, **dw_opts)
            dhs, dga, dob, dgb, dhn = even_post_bwd(f"even_post_bwd", proj, sv['hseq'], sv['ob'], row(hg_norm_w[e]), dy)
            gl['hg_norm_w'][e] = dhn[0]
            dq, df, di, dlb = hg_bwd(f"hg_bwd", proj, row(lbs[e]), sv['states'], dob, lru_w)
            d_lbs[e] = dlb
            lamb = lin_scan(f"lru_scan_bwd", sv['a'], dhs, True)
            dxc, dcw, dcb, dwr, dbr, dwi, dbi, dlam = lru_pre_bwd(f"lru_pre_bwd", proj, *lru_p, sv['xc'], sv['hseq'], lamb)
            nb = lru_w_r.shape[1]
            gl['lru_conv_w'][e], gl['lru_conv_b'][e] = dcw[:lru_p[0].shape[0]], dcb[0]
            gl['lru_w_r'][e], gl['lru_b_r'][e] = _diag_blocks(dwr, nb), dbr[0]
            gl['lru_w_i'][e], gl['lru_b_i'][e] = _diag_blocks(dwi, nb), dbi[0]
            gl['lru_lambda'][e] = dlam[0]
            dxa = conv_t_call(f"lru_convt", dxc, lru_p[0], 0, lru_w)
            dproj = jnp.concatenate([dxa, dga, dq, df, di, dgb], axis=1)
            in_dw_sends = [(('ffn_w_up', l, 1), up_b)] + ([((*k_out, 0), pieces(k_out, dw_out))] if l == 0 else [])
            dh = mm_send("ev_in_dx", dproj, weight[k_in], in_dx_sends, tb=True)
            dw_in = mm_send("ev_in_dw", sv['h'], dproj, in_dw_sends, **dw_opts)
        else:
            o = l // 2
            zx, scw = sv['zx'], full['ssd_conv_w'][o]
            dyn = matmul("ssd_out_dx", gmid, weight[k_out], tb=True)
            dw_out = matmul("ssd_out_dw", sv['yn'], gmid, **dw_opts)
            dys, dz, dnw = ssd_post_bwd(f"ssd_post_bwd", sv['ys'], zx, row(full['ssd_norm_w'][o]), dyn, groups)
            gl['ssd_norm_w'][o] = dnw[0]
            dxs, dbm, dcm, ddt, dcum, dcr, ddexp = ssd_bwd(f"ssd_bwd", sv['act'], *sv['prep'], sv['ssd_p'][2],
                                                          sv['states'], dys, inner, groups)
            ddtr, ddtb, dalog = ssd_prep_bwd("ssd_prep_bwd", sv['dtr'], sv['ssd_p'][0], sv['ssd_p'][1], ddt, dcum, dcr, inner)
            gl['ssd_dt_bias'][o], gl['ssd_a_log'][o] = ddtb[0, :heads], dalog[0, :heads]
            gl['ssd_d'][o] = jnp.sum(ddexp.reshape(heads, HEADDIM), axis=1)
            dact = jnp.concatenate([dxs, dbm, dcm], axis=1)
            dpre, dcw, dcb = ssd_conv_bwd(f"ssd_conv_bwd", zx, scw, row(full['ssd_conv_b'][o]), dact, inner)
            gl['ssd_conv_w'][o], gl['ssd_conv_b'][o] = dcw[:scw.shape[0]], dcb[0]
            dxbc = conv_t_call(f"ssd_convt", dpre, scw, 0, _ssd_conv_width(inner, cdim))
            dzx = jnp.concatenate([dz, dxbc], axis=1)
            in_dw_sends = [(('ffn_w_up', l, 1), up_b)] + ([((*k_out, 0), pieces(k_out, dw_out))] if l == 0 else [])
            dh = mm_send("ssd_in_dx", dzx, sv['wzx'], in_dx_sends, tb=True)
            dh = matmul("ssd_dt_dx", ddtr, sv['wdt'], tb=True, res=dh)
            dwzx = mm_send("ssd_in_dw", sv['h'], dzx, in_dw_sends, **dw_opts)
            dwdt = matmul("ssd_dt_dw", sv['h'], ddtr, **dw_opts)
            dw_in = jnp.concatenate([dwzx, dwdt[:, :heads]], axis=1)
        if l > 0:
            in_a, in_b = halves(pieces(k_in, dw_in))
            carried = {'down_dx': [((*k_in, 0), in_a)], 'down_dw': [((*k_in, 1), in_b)],
                       'up_dw': [((*k_out, 0), pieces(k_out, dw_out))]}
        gcur, dnw = rms_bwd(f"rms_mix_bwd", sv['x0'], row(norm_mix_w[l]), dh, gmid)
        gl['norm_mix_w'][l] = dnw[0]

    def lbs_bwd(hb, dl):
        _, vjp = jax.vjp(f_lbs, hb)
        return [vjp(dl)[0]]
    d_hlb = whole_call("lbs_bwd", lbs_bwd, [hg_lower_bounds, jnp.concatenate(d_lbs, axis=0)], [hg_lower_bounds.shape])[0]

    part = {n: jnp.stack(v) for n, v in gl.items() if n != 'hg_lower_bounds'}
    part['hg_lower_bounds'] = d_hlb
    part['norm_final_w'] = d_nfw[0]

    slots[(*k_in, 0)] = exchange("rs_last", pieces(k_in, dw_in))
    out = {}
    for n in BIG:
        res = None
        for idx in range(wts[n].shape[0]):
            parts = [slots[n, idx, p] for p in range(2) if (n, idx, p) in slots]
            got = parts[0] if len(parts) == 1 else jnp.concatenate(parts, axis=1)
            res = adamw(f"adamw_{n}", got, wts[n], mom1[n], mom2[n], idx, res)
        out[n] = res

    small_buf, _ = _pack([jnp.concatenate([part[n].astype(F32).reshape(-1) for n in SMALL])])
    sm = all_gather("ag_small_grads", small_buf).reshape(N_DEV, -1)
    own, at = [], 0
    for n in SMALL:
        size = part[n].size
        g8 = sm[:, at:at + size].reshape((N_DEV,) + part[n].shape)
        at += size
        if n in SMALL_SHARDED:
            g8 = lax.dynamic_slice_in_dim(g8, me * wts[n].shape[-1], wts[n].shape[-1], axis=g8.ndim - 1)
        own.append(g8.reshape(N_DEV, -1))
    sslots, _ = _pack([jnp.concatenate(own, axis=1)])
    cat = lambda dct: _pad_to(jnp.concatenate([dct[n].reshape(-1) for n in SMALL]), SEG).reshape(1, -1, 128)
    res = adamw("adamw_small", sslots, cat(wts), cat(mom1), cat(mom2))
    at = 0
    for n in SMALL:
        out[n] = [r.reshape(-1)[at:at + wts[n].size].reshape(wts[n].shape) for r in res]
        at += wts[n].size

    grad_x = gcur.reshape(x.shape)
    return (loss, grad_x, *[out[n][0] for n in WEIGHTS], *[out[n][1] for n in WEIGHTS],
            *[out[n][2] for n in WEIGHTS], *[out[n][3] for n in WEIGHTS])
```

```python
import functools
import math

import jax
import jax.numpy as jnp
from jax import lax
from jax.experimental import pallas as pl
from jax.experimental.pallas import tpu as pltpu

F32 = jnp.float32
BF16 = jnp.bfloat16
MXU = jnp.bfloat16
HI = lax.Precision.HIGHEST

N_DEV = 8
EPS = 1e-6
LRU_C = 8.0
CHUNK = 64
STATE = 128
ADAM_LR, ADAM_B1, ADAM_B2, ADAM_EPS, ADAM_WD, ADAM_STEP = 0.001, 0.9, 0.999, 1e-08, 0.01, 10

SUBLANES = 8
VMEM_LIMIT = 56 * 1024 * 1024

NN = (((1,), (0,)), ((), ()))
NT = (((1,), (1,)), ((), ()))
TN = (((0,), (0,)), ((), ()))


def _params(sem):
    return pltpu.CompilerParams(dimension_semantics=sem, vmem_limit_bytes=VMEM_LIMIT)


def _divisor(n, target, align):
    if n <= target:
        return n
    best = None
    for d in range(align, target + 1, align):
        if n % d == 0:
            best = d
    assert best is not None, (n, target, align)
    return best


def _mm(a, b, dn):
    return lax.dot_general(a.astype(MXU), b.astype(MXU), dn, preferred_element_type=F32)


@jax.custom_vjp
def mm_nn(a, b):
    return _mm(a, b, NN)


@jax.custom_vjp
def mm_nt(a, b):
    return _mm(a, b, NT)


@jax.custom_vjp
def mm_tn(a, b):
    return _mm(a, b, TN)


mm_nn.defvjp(lambda a, b: (_mm(a, b, NN), (a, b)), lambda r, g: (mm_nt(g, r[1]), mm_tn(r[0], g)))
mm_nt.defvjp(lambda a, b: (_mm(a, b, NT), (a, b)), lambda r, g: (mm_nn(g, r[1]), mm_tn(g, r[0])))
mm_tn.defvjp(lambda a, b: (_mm(a, b, TN), (a, b)), lambda r, g: (mm_nt(r[1], g), mm_nn(r[0], g)))


def dot_hi(a, b, dn=NN):
    return lax.dot_general(a, b, dn, precision=HI, preferred_element_type=F32)


def _onehot_mm(x, onehot, dn):
    hi = x.astype(MXU)
    rest = x - hi.astype(F32)
    mid = rest.astype(MXU)
    lo = (rest - mid.astype(F32)).astype(MXU)
    oh = onehot.astype(MXU)
    return sum(lax.dot_general(p, oh, dn, preferred_element_type=F32) for p in (hi, mid, lo))


@jax.custom_vjp
def spread_dot(x, onehot):
    return _onehot_mm(x, onehot, NN)


spread_dot.defvjp(lambda x, onehot: (_onehot_mm(x, onehot, NN), onehot),
                  lambda onehot, g: (_onehot_mm(g, onehot, NT), jnp.zeros_like(onehot)))


def _iota(shape, dim):
    return lax.broadcasted_iota(jnp.int32, shape, dim)


def _tril(n):
    return (_iota((n, n), 0) >= _iota((n, n), 1)).astype(F32)


def _softplus(x):
    return jnp.maximum(x, 0.0) + jnp.log1p(jnp.exp(-jnp.abs(x)))


def _neg_expm1(x):
    series = -x * (1.0 + x * (0.5 + x * (1.0 / 6.0 + x * (1.0 / 24.0))))
    return jnp.where(x > -0.03, series, 1.0 - jnp.exp(x))


_HBM = pl.BlockSpec(memory_space=pltpu.HBM)
_MESH = pl.DeviceIdType.MESH


def _peer_copies(kind, src_ref, dst_ref, send_sems, recv_sems, local_sem, rows=None):
    x, y, c = lax.axis_index("x"), lax.axis_index("y"), lax.axis_index("c")
    me = 4 * x + 2 * y + c
    if kind == "gather":
        pick = lambda p: src_ref
    elif rows is None:
        pick = lambda p: src_ref.at[p]
    else:
        pick = lambda p: src_ref.at[p, pl.ds(rows[0], rows[1])]
    local = pltpu.make_async_copy(pick(me), dst_ref.at[me], local_sem)
    remote = []
    for k in range(N_DEV - 1):
        px, py, pc = (x + ((k + 1) >> 2 & 1)) % 2, (y + ((k + 1) >> 1 & 1)) % 2, (c + ((k + 1) & 1)) % 2
        remote.append(pltpu.make_async_remote_copy(
            src_ref=pick(4 * px + 2 * py + pc), dst_ref=dst_ref.at[me], send_sem=send_sems.at[k],
            recv_sem=recv_sems.at[k], device_id=(px, py, pc), device_id_type=_MESH))
    return local, remote


_PEER_SEMS = [pltpu.SemaphoreType.DMA((N_DEV - 1,)), pltpu.SemaphoreType.DMA((N_DEV - 1,)), pltpu.SemaphoreType.DMA]


def matmul(name, a, b, *, ta=False, tb=False, res=None, out_dtype=F32, tm=1024, tn=1024, tk=2048, rides=(),
           b_pieces=False, out_pieces=0):
    m, k = (a.shape[1], a.shape[0]) if ta else a.shape
    if b_pieces:
        shard = b.shape[2]
        n = b.shape[1] if tb else N_DEV * shard
        assert (N_DEV * shard if tb else b.shape[1]) == k and not out_pieces, (name, a.shape, b.shape)
    else:
        n = b.shape[0] if tb else b.shape[1]
        assert (b.shape[1] if tb else b.shape[0]) == k, (name, a.shape, b.shape)
    tm, tn, tk = _divisor(m, tm, 128), _divisor(n, tn, 128), _divisor(k, tk, 128)
    if b_pieces:
        tn, tk = (tn, shard) if tb else (shard, tk)
    if out_pieces:
        tn = out_pieces
    assert m % tm == 0 and n % tn == 0 and k % tk == 0, (name, m, n, k, tm, tn, tk)
    ni, nj, nk = m // tm, n // tn, k // tk
    dn = (((0 if ta else 1,), (1 if tb else 0,)), ((), ()))
    rides = list(rides)
    nr = len(rides)
    n_in = 2 + (res is not None) + nr
    n_out = 1 + nr

    def body(*refs):
        a_ref, b_ref = refs[0], refs[1]
        r_ref = refs[2] if res is not None else None
        o_ref = refs[n_in]
        scratch = refs[n_in + n_out:]
        sems = scratch[len(scratch) - 3 * nr:]
        i, j, kk = pl.program_id(0), pl.program_id(1), pl.program_id(2)

        def peers(r):
            return _peer_copies(rides[r][0], refs[n_in - nr + r], refs[n_in + 1 + r], *sems[3 * r:3 * r + 3],
                                rows=rides[r][2])

        if nr:
            @pl.when((i == 0) & (j == 0) & (kk == 0))
            def _():
                for r in range(nr):
                    local, remote = peers(r)
                    local.start()
                    for cp in remote:
                        cp.start()

        def finish(r):
            if r_ref is not None:
                r = r + r_ref[...]
            o_ref[...] = r.astype(o_ref.dtype)

        if nk == 1:
            finish(_mm(a_ref[...], b_ref[...], dn))
        else:
            acc_ref = scratch[0]

            @pl.when(kk == 0)
            def _():
                acc_ref[...] = jnp.zeros_like(acc_ref)

            acc_ref[...] += _mm(a_ref[...], b_ref[...], dn)

            @pl.when(kk == nk - 1)
            def _():
                finish(acc_ref[...])

        if nr:
            @pl.when((i == ni - 1) & (j == nj - 1) & (kk == nk - 1))
            def _():
                for r in range(nr):
                    local, remote = peers(r)
                    for cp in remote:
                        cp.wait_recv()
                    for cp in remote:
                        cp.wait_send()
                    local.wait()

    a_spec = pl.BlockSpec((tk, tm), lambda i, j, kk: (kk, i)) if ta else pl.BlockSpec((tm, tk), lambda i, j, kk: (i, kk))
    if b_pieces:
        b_spec = (pl.BlockSpec((None, tn, tk), lambda i, j, kk: (kk, j, 0)) if tb
                  else pl.BlockSpec((None, tk, tn), lambda i, j, kk: (j, kk, 0)))
    else:
        b_spec = pl.BlockSpec((tn, tk), lambda i, j, kk: (j, kk)) if tb else pl.BlockSpec((tk, tn), lambda i, j, kk: (kk, j))
    o_spec = pl.BlockSpec((tm, tn), lambda i, j, kk: (i, j))
    ins, specs = [a, b], [a_spec, b_spec]
    if out_pieces:
        assert res is None and tn == out_pieces, (name, tn, out_pieces)
        out_specs = [pl.BlockSpec((None, tm, tn), lambda i, j, kk: (j, i, 0))]
        out_shape = [jax.ShapeDtypeStruct((nj, m, tn), out_dtype)]
    else:
        out_specs, out_shape = [o_spec], [jax.ShapeDtypeStruct((m, n), out_dtype)]
    scratch_shapes = [pltpu.VMEM((tm, tn), F32)] if nk > 1 else []
    if res is not None:
        ins.append(res)
        specs.append(o_spec)
    for kind, src, rows in rides:
        ins.append(src)
        specs.append(_HBM)
        out_specs.append(_HBM)
        if kind == "gather":
            got = (N_DEV,) + src.shape
        else:
            got = src.shape if rows is None else (N_DEV, rows[1]) + src.shape[2:]
        out_shape.append(jax.ShapeDtypeStruct(got, src.dtype))
        scratch_shapes += _PEER_SEMS
    outs = pl.pallas_call(
        body, name=name, grid=(ni, nj, nk), in_specs=specs, out_specs=out_specs, out_shape=out_shape,
        scratch_shapes=scratch_shapes, compiler_params=_params(("arbitrary", "arbitrary", "arbitrary")),
    )(*ins)
    return (outs[0], list(outs[1:])) if nr else outs[0]


def Row(arr, width=None, off=0, var=True):
    return ("row", arr, arr.shape[1] if width is None else width, off, var)


def Prev(arr, width=None, off=0, var=True):
    return ("prev", arr, arr.shape[1] if width is None else width, off, var)


def Next(arr, width=None, off=0, var=True):
    return ("next", arr, arr.shape[1] if width is None else width, off, var)


def Full(arr, width=None, off=0, var=True):
    return ("full", arr, arr.shape[1] if width is None else width, off, var)


def rows_call(name, fn, rows, tile, ncol, ins, outs=(), accs=()):
    nrow = rows // tile
    assert rows % tile == 0 and tile % SUBLANES == 0, (name, rows, tile)
    last8 = rows // SUBLANES - 1
    per8 = tile // SUBLANES

    def spec(kind, arr, width, off, var):
        col = (lambda j: off + j) if var else (lambda j: off)
        if kind == "row":
            return pl.BlockSpec((tile, width), lambda j, i: (i, col(j)))
        if kind == "prev":
            return pl.BlockSpec((SUBLANES, width), lambda j, i: (jnp.maximum(i * per8 - 1, 0), col(j)))
        if kind == "next":
            return pl.BlockSpec((SUBLANES, width), lambda j, i: (jnp.minimum((i + 1) * per8, last8), col(j)))
        return pl.BlockSpec((arr.shape[0], width), lambda j, i: (0, col(j)))

    n_in, n_out = len(ins), len(outs)

    def body(*refs):
        j, i = pl.program_id(0), pl.program_id(1)
        o_tiles, a_tiles = fn(i, j, *[r[...] for r in refs[:n_in]])
        for r, o in zip(refs[n_in:n_in + n_out], o_tiles, strict=True):
            r[...] = o.astype(r.dtype)
        acc_refs = refs[n_in + n_out:]
        if acc_refs:
            @pl.when(i == 0)
            def _():
                for r in acc_refs:
                    r[...] = jnp.zeros_like(r)
            for r, a in zip(acc_refs, a_tiles, strict=True):
                r[...] += a

    out_shape = [jax.ShapeDtypeStruct((rows, w * ncol), dt) for dt, w in outs]
    out_shape += [jax.ShapeDtypeStruct((r, w * ncol), F32) for r, w in accs]
    out_specs = [pl.BlockSpec((tile, w), lambda j, i: (i, j)) for _, w in outs]
    out_specs += [pl.BlockSpec((r, w), lambda j, i: (0, j)) for r, w in accs]
    res = pl.pallas_call(
        body, name=name, grid=(ncol, nrow), in_specs=[spec(*s) for s in ins], out_specs=out_specs,
        out_shape=out_shape, compiler_params=_params(("arbitrary", "arbitrary")),
    )(*[s[1] for s in ins])
    return res


def _shift_down(tile, prev8, s, first):
    if s == 0:
        return tile
    rolled = pltpu.roll(tile, s, 0)
    pr = jnp.where(first, 0.0, pltpu.roll(prev8, s, 0))
    head = jnp.where(_iota(pr.shape, 0) < s, pr, rolled[:SUBLANES])
    return jnp.concatenate([head, rolled[SUBLANES:]], axis=0)


def _shift_up(tile, next8, s, last):
    if s == 0:
        return tile
    t = tile.shape[0]
    rolled = pltpu.roll(tile, t - s, 0)
    nx = jnp.where(last, 0.0, pltpu.roll(next8, SUBLANES - s, 0))
    tail = jnp.where(_iota(nx.shape, 0) >= SUBLANES - s, nx, rolled[t - SUBLANES:])
    return jnp.concatenate([rolled[:t - SUBLANES], tail], axis=0)


def _row(w, k):
    return jnp.sum(jnp.where(_iota(w.shape, 0) == k, w, 0.0), axis=0, keepdims=True)


def _conv(x, prev8, w, b, first):
    kk = w.shape[0]
    y = b + _row(w, kk - 1) * x
    for k in range(kk - 1):
        y = y + _row(w, k) * _shift_down(x, prev8, kk - 1 - k, first)
    return y


def _conv_wgrad(x, prev8, dy, kk, first):
    out = jnp.zeros((SUBLANES, x.shape[1]), F32)
    for k in range(kk):
        r = jnp.sum(dy * _shift_down(x, prev8, kk - 1 - k, first), axis=0, keepdims=True)
        out = out + jnp.where(_iota(out.shape, 0) == k, r, 0.0)
    return out


def _conv_t(dy, next8, w, last):
    kk = w.shape[0]
    dx = _row(w, kk - 1) * dy
    for k in range(kk - 1):
        dx = dx + _row(w, k) * _shift_up(dy, next8, kk - 1 - k, last)
    return dx


def f_rms(x, w):
    return x * lax.rsqrt(jnp.mean(x * x, axis=-1, keepdims=True) + EPS) * w


def rms_fwd(name, x, w):
    def fn(i, j, xt, wt):
        return [f_rms(xt, wt)], []
    return rows_call(name, fn, x.shape[0], _divisor(x.shape[0], 512, 8), 1, [Row(x), Full(w)],
                     outs=[(BF16, x.shape[1])])[0]


def rms_bwd(name, x, w, dh, dres):
    def fn(i, j, xt, wt, dht, drt):
        _, vjp = jax.vjp(f_rms, xt, wt)
        dx, dw = vjp(dht)
        return [drt + dx], [dw]
    d = x.shape[1]
    return rows_call(name, fn, x.shape[0], _divisor(x.shape[0], 256, 8), 1, [Row(x), Full(w), Row(dh), Row(dres)],
                     outs=[(F32, d)], accs=[(1, d)])


def loss_head(name, x, w, target):
    def fn(i, j, xt, wt, tt):
        def f(xx, ww):
            err = f_rms(xx, ww) - tt
            return 0.5 * jnp.mean(err * err, axis=-1, keepdims=True)
        rows, vjp = jax.vjp(f, xt, wt)
        dx, dw = vjp(jnp.ones_like(rows))
        return [dx], [dw, jnp.broadcast_to(jnp.sum(rows, axis=0, keepdims=True), (1, 128))]
    d = x.shape[1]
    return rows_call(name, fn, x.shape[0], _divisor(x.shape[0], 256, 8), 1, [Row(x), Full(w), Row(target)],
                     outs=[(F32, d)], accs=[(1, d), (1, 128)])


def ffn_act_fwd(name, u0, cw, cb):
    t, two_f = u0.shape
    wc = _divisor(two_f // 2, 512, 128)
    nc = two_f // 2 // wc

    def fn(i, j, ug, ugp, uv, uvp, wg, wv, bg, bv):
        first = i == 0
        g = _conv(ug, ugp, wg, bg, first)
        v = _conv(uv, uvp, wv, bv, first)
        return [jax.nn.silu(g) * v], []
    ins = [Row(u0, wc), Prev(u0, wc), Row(u0, wc, nc), Prev(u0, wc, nc),
           Full(cw, wc), Full(cw, wc, nc), Full(cb, wc), Full(cb, wc, nc)]
    return rows_call(name, fn, t, _divisor(t, 1024, 8), nc, ins, outs=[(BF16, wc)])[0]


def ffn_act_bwd(name, u0, cw, cb, dact):
    t, two_f = u0.shape
    wc = _divisor(two_f // 2, 512, 128)
    nc = two_f // 2 // wc
    kk = cw.shape[0]
    tile = _divisor(t, 512, 8)
    last_i = t // tile - 1

    def conv_t(dy, w):
        rows = dy.shape[0]
        dx = _row(w, kk - 1) * dy[:tile]
        for k in range(kk - 1):
            dx = dx + _row(w, k) * pltpu.roll(dy, rows - (kk - 1 - k), 0)[:tile]
        return dx

    def fn(i, j, ug, ugp, ugn, uv, uvp, uvn, wg, wv, bg, bv, da, dan):
        first = i == 0
        ext = lambda a, b: jnp.concatenate([a, b], axis=0)
        g = _conv(ext(ug, ugn), ugp, wg, bg, first)
        v = _conv(ext(uv, uvn), uvp, wv, bv, first)
        _, vjp = jax.vjp(lambda gg, vv: jax.nn.silu(gg) * vv, g, v)
        dg, dv = vjp(ext(da, jnp.where(i == last_i, 0.0, dan)))
        accs = [_conv_wgrad(ug, ugp, dg[:tile], kk, first), _conv_wgrad(uv, uvp, dv[:tile], kk, first),
                jnp.sum(dg[:tile], axis=0, keepdims=True), jnp.sum(dv[:tile], axis=0, keepdims=True)]
        return [conv_t(dg, wg), conv_t(dv, wv)], accs
    ins = [Row(u0, wc), Prev(u0, wc), Next(u0, wc), Row(u0, wc, nc), Prev(u0, wc, nc), Next(u0, wc, nc),
           Full(cw, wc), Full(cw, wc, nc), Full(cb, wc), Full(cb, wc, nc), Row(dact, wc), Next(dact, wc)]
    return rows_call(name, fn, t, tile, nc, ins, outs=[(BF16, wc), (BF16, wc)],
                     accs=[(SUBLANES, wc), (SUBLANES, wc), (1, wc), (1, wc)])


def conv_t_call(name, dy, cw, col_off, width):
    out_dtype = BF16
    t, c = dy.shape
    nc = c // width
    nrow_tile = _divisor(t, 512, 8)
    last_i = t // nrow_tile - 1

    def fn(i, j, d, dn, w):
        return [_conv_t(d, dn, w, i == last_i)], []
    return rows_call(name, fn, t, nrow_tile, nc, [Row(dy, width), Next(dy, width), Full(cw, width, col_off)],
                     outs=[(out_dtype, width)])[0]


def f_lru_gates(xc, wr, br, wi, bi, lam):
    r = jax.nn.sigmoid(mm_nn(xc, wr) + br)
    gi = jax.nn.sigmoid(mm_nn(xc, wi) + bi)
    log_a = -LRU_C * r * _softplus(-lam)
    a = jnp.exp(log_a)
    u = jnp.sqrt(_neg_expm1(2.0 * log_a)) * (gi * xc)
    return a, u


def lru_pre_fwd(name, proj, cw, cb, wr, br, wi, bi, lam):
    t, w = proj.shape[0], lam.shape[1]

    def fn(i, j, xa, xap, cwt, cbt, wrt, brt, wit, bit, lamt):
        xc = _conv(xa, xap, cwt, cbt, i == 0)
        a, u = f_lru_gates(xc, wrt, brt, wit, bit, lamt)
        return [xc, a, u], []
    ins = [Row(proj, w), Prev(proj, w), Full(cw), Full(cb), Full(wr), Full(br), Full(wi), Full(bi), Full(lam)]
    return rows_call(name, fn, t, _divisor(t, 512, 8), 1, ins, outs=[(F32, w)] * 3)


def lru_pre_bwd(name, proj, cw, cb, wr, br, wi, bi, lam, xc, hseq, lamb):
    t, w = proj.shape[0], lam.shape[1]
    kk = cw.shape[0]

    def fn(i, j, xa, xap, xct, hs, hsp, lb, wrt, brt, wit, bit, lamt):
        first = i == 0
        da = lb * _shift_down(hs, hsp, 1, first)
        _, vjp = jax.vjp(f_lru_gates, xct, wrt.astype(F32), brt, wit.astype(F32), bit, lamt)
        dxc, dwr, dbr, dwi, dbi, dlam = vjp((da, lb))
        accs = [_conv_wgrad(xa, xap, dxc, kk, first), jnp.sum(dxc, axis=0, keepdims=True), dwr, dbr, dwi, dbi, dlam]
        return [dxc], accs
    ins = [Row(proj, w), Prev(proj, w), Row(xc), Row(hseq), Prev(hseq), Row(lamb),
           Full(wr), Full(br), Full(wi), Full(bi), Full(lam)]
    return rows_call(name, fn, t, _divisor(t, 256, 8), 1, ins, outs=[(F32, w)],
                     accs=[(SUBLANES, w), (1, w), (w, w), (1, w), (w, w), (1, w), (1, w)])


def lin_scan(name, a, x, reverse):
    t, c = a.shape
    tile = _divisor(t, 512, 8)
    n = t // tile

    def body(a_ref, x_ref, o_ref, c_ref):
        @pl.when(pl.program_id(0) == 0)
        def _():
            c_ref[...] = jnp.zeros_like(c_ref)

        def step(s, carry):
            r = (tile - 1 - s) if reverse else s
            at, xt = a_ref[pl.ds(r, 1), :], x_ref[pl.ds(r, 1), :]
            o = (xt + carry) if reverse else (at * carry + xt)
            o_ref[pl.ds(r, 1), :] = o
            return (at * o) if reverse else o
        c_ref[...] = lax.fori_loop(0, tile, step, c_ref[...], unroll=8)

    spec = pl.BlockSpec((tile, c), (lambda i: (n - 1 - i, 0)) if reverse else (lambda i: (i, 0)))
    return pl.pallas_call(
        body, name=name, grid=(n,), in_specs=[spec, spec], out_specs=spec,
        out_shape=jax.ShapeDtypeStruct((t, c), F32), scratch_shapes=[pltpu.VMEM((1, c), F32)],
        compiler_params=_params(("arbitrary",)),
    )(a, x)


def _hg_chunk(s, q, fr, v, lb):
    f = lb + (1.0 - lb) * jax.nn.sigmoid(fr)
    k = 1.0 - f
    g = jnp.log(f)
    qs = jax.nn.silu(q) * (STATE ** -0.5)
    cum = dot_hi(_tril(CHUNK), g)
    tot = jnp.sum(g, axis=0, keepdims=True)
    mid = jnp.sum(jnp.where(_iota(g.shape, 0) < CHUNK // 2, g, 0.0), axis=0, keepdims=True)
    scores = mm_nt(qs * jnp.exp(cum - mid), k * jnp.exp(mid - cum))
    scores = jnp.where(_tril(CHUNK) > 0, scores, 0.0)
    o = mm_nn(scores, v) + mm_nn(qs * jnp.exp(cum), s)
    decay = jnp.broadcast_to(jnp.exp(tot), s.shape).T
    s_new = decay * s + mm_tn(k * jnp.exp(tot - cum), v)
    return o, s_new


def _hg_specs(proj, heads, lru_w, hg_w, rows, rev):
    nblk = proj.shape[0] // rows
    blk = (lambda b: nblk - 1 - b) if rev else (lambda b: b)
    base = 2 * lru_w // STATE
    per = hg_w // STATE
    col = [pl.BlockSpec((rows, STATE), functools.partial(lambda h, b, o: (blk(b), o + h), o=base + k * per))
           for k in range(3)]
    return nblk, blk, col


def hg_fwd(name, proj, lbs, lru_w, cb=4):
    t, hg_w = proj.shape[0], lbs.shape[1]
    heads = hg_w // STATE
    cb = min(cb, t // CHUNK)
    rows = cb * CHUNK
    nblk, blk, col = _hg_specs(proj, heads, lru_w, hg_w, rows, False)

    def body(q_ref, f_ref, v_ref, lb_ref, o_ref, s_ref, st):
        @pl.when(pl.program_id(1) == 0)
        def _():
            st[...] = jnp.zeros_like(st)
        s = st[...]
        for c in range(cb):
            sl = slice(c * CHUNK, (c + 1) * CHUNK)
            s_ref[c] = s
            o, s = _hg_chunk(s, q_ref[sl, :], f_ref[sl, :], v_ref[sl, :], lb_ref[...])
            o_ref[sl, :] = o
        st[...] = s

    return pl.pallas_call(
        body, name=name, grid=(heads, nblk),
        in_specs=col + [pl.BlockSpec((1, STATE), lambda h, b: (0, h))],
        out_specs=[pl.BlockSpec((rows, STATE), lambda h, b: (b, h)),
                   pl.BlockSpec((cb, None, STATE, STATE), lambda h, b: (b, h, 0, 0))],
        out_shape=[jax.ShapeDtypeStruct((t, hg_w), F32),
                   jax.ShapeDtypeStruct((t // CHUNK, heads, STATE, STATE), F32)],
        scratch_shapes=[pltpu.VMEM((STATE, STATE), F32)],
        compiler_params=_params(("arbitrary", "arbitrary")),
    )(proj, proj, proj, lbs)


def hg_bwd(name, proj, lbs, states, do, lru_w, cb=4):
    t, hg_w = proj.shape[0], lbs.shape[1]
    heads = hg_w // STATE
    cb = min(cb, t // CHUNK)
    rows = cb * CHUNK
    nblk, blk, col = _hg_specs(proj, heads, lru_w, hg_w, rows, True)

    def body(q_ref, f_ref, v_ref, lb_ref, s_ref, do_ref, dq_ref, df_ref, dv_ref, dlb_ref, dst):
        @pl.when(pl.program_id(1) == 0)
        def _():
            dst[...] = jnp.zeros_like(dst)
            dlb_ref[...] = jnp.zeros_like(dlb_ref)
        ds = dst[...]
        dlb = jnp.zeros((1, STATE), F32)
        for c in reversed(range(cb)):
            sl = slice(c * CHUNK, (c + 1) * CHUNK)
            _, vjp = jax.vjp(_hg_chunk, s_ref[c], q_ref[sl, :], f_ref[sl, :], v_ref[sl, :], lb_ref[...])
            ds, dq, df, dv, dl = vjp((do_ref[sl, :], ds))
            dq_ref[sl, :] = dq.astype(dq_ref.dtype)
            df_ref[sl, :] = df.astype(df_ref.dtype)
            dv_ref[sl, :] = dv.astype(dv_ref.dtype)
            dlb = dlb + dl
        dst[...] = ds
        dlb_ref[...] += dlb

    rspec = pl.BlockSpec((rows, STATE), lambda h, b: (blk(b), h))
    return pl.pallas_call(
        body, name=name, grid=(heads, nblk),
        in_specs=col + [pl.BlockSpec((1, STATE), lambda h, b: (0, h)),
                        pl.BlockSpec((cb, None, STATE, STATE), lambda h, b: (blk(b), h, 0, 0)), rspec],
        out_specs=[rspec, rspec, rspec, pl.BlockSpec((1, STATE), lambda h, b: (0, h))],
        out_shape=[jax.ShapeDtypeStruct((t, hg_w), BF16)] * 3 + [jax.ShapeDtypeStruct((1, hg_w), F32)],
        scratch_shapes=[pltpu.VMEM((STATE, STATE), F32)],
        compiler_params=_params(("arbitrary", "arbitrary")),
    )(proj, proj, proj, lbs, states, do)


def f_even_post(hseq, ga, ob, gb, nw):
    parts = [hseq * jax.nn.gelu(ga)]
    for h in range(ob.shape[1] // STATE):
        o = ob[:, h * STATE:(h + 1) * STATE]
        on = o * lax.rsqrt(jnp.mean(o * o, axis=-1, keepdims=True) + EPS) * nw
        parts.append(on * jax.nn.silu(gb[:, h * STATE:(h + 1) * STATE]))
    return jnp.concatenate(parts, axis=-1)


def _even_post_ins(proj, hseq, ob, nw):
    w, v = hseq.shape[1], ob.shape[1]
    assert w == v
    return [Row(hseq), Row(proj, w, 1), Row(ob), Row(proj, v, (2 * w + 3 * v) // v), Full(nw)]


def even_post_fwd(name, proj, hseq, ob, nw):
    t = proj.shape[0]

    def fn(i, j, hs, ga, o, gb, nwt):
        return [f_even_post(hs, ga, o, gb, nwt)], []
    return rows_call(name, fn, t, _divisor(t, 256, 8), 1, _even_post_ins(proj, hseq, ob, nw),
                     outs=[(BF16, hseq.shape[1] + ob.shape[1])])[0]


def even_post_bwd(name, proj, hseq, ob, nw, dy):
    t, w, v = proj.shape[0], hseq.shape[1], ob.shape[1]

    def fn(i, j, hs, ga, o, gb, nwt, dyt):
        _, vjp = jax.vjp(f_even_post, hs, ga, o, gb, nwt)
        dhs, dga, dob, dgb, dnw = vjp(dyt)
        return [dhs, dga, dob, dgb], [dnw]
    return rows_call(name, fn, t, _divisor(t, 256, 8), 1, _even_post_ins(proj, hseq, ob, nw) + [Row(dy)],
                     outs=[(F32, w), (BF16, w), (F32, v), (BF16, v)], accs=[(1, STATE)])


def f_lbs(hb):
    e = jnp.exp(hb - jnp.max(hb, axis=0, keepdims=True))
    p = e / jnp.sum(e, axis=0, keepdims=True)
    out, run = jnp.zeros_like(p), jnp.zeros_like(p[:1])
    for r in range(hb.shape[0]):
        run = run + _row(p, r)
        out = out + jnp.where(_iota(p.shape, 0) == r, run - _row(p, 0), 0.0)
    return out


def whole_call(name, fn, ins, out_shapes):
    def body(*refs):
        outs = fn(*[r[...] for r in refs[:len(ins)]])
        for r, o in zip(refs[len(ins):], outs, strict=True):
            r[...] = o
    return pl.pallas_call(body, name=name, out_shape=[jax.ShapeDtypeStruct(s, F32) for s in out_shapes])(*ins)


HEADDIM = 64


def f_ssd_prep(dtr, dtb, alog, inner):
    rows = dtr.shape[0]
    dt_all = _softplus(dtr + dtb)
    da_all = dt_all * (-jnp.exp(alog))
    tril = _tril(CHUNK)
    cums = [dot_hi(tril, da_all[c * CHUNK:(c + 1) * CHUNK]) for c in range(rows // CHUNK)]
    cum_all = jnp.concatenate(cums, axis=0) if len(cums) > 1 else cums[0]
    head_of = _iota((STATE, inner), 1) - HEADDIM * _iota((STATE, inner), 0)
    spread = ((head_of >= 0) & (head_of < HEADDIM)).astype(F32)
    even = (_iota((CHUNK, STATE), 1) == 2 * _iota((CHUNK, STATE), 0)).astype(F32)
    odd = (_iota((CHUNK, STATE), 1) == 2 * _iota((CHUNK, STATE), 0) + 1).astype(F32)
    left = _iota((CHUNK, STATE), 1) < HEADDIM
    cumrows = []
    for cm in cums:
        twice = jnp.concatenate([cm, cm], axis=0)
        cumrows.append(jnp.where(left, dot_hi(even, twice, NT), dot_hi(odd, twice, NT)))
    return spread_dot(dt_all, spread), spread_dot(cum_all, spread), tuple(cumrows)


def ssd_prep_fwd(name, dtr, dtb, alog, inner, k=4):
    t = dtr.shape[0]
    k = min(k, t // CHUNK)
    rows = k * CHUNK

    def body(dtr_ref, dtb_ref, alog_ref, dt_ref, cum_ref, cr_ref):
        dt, cum, crs = f_ssd_prep(dtr_ref[...], dtb_ref[...], alog_ref[...], inner)
        dt_ref[...] = dt
        cum_ref[...] = cum
        for c, cr in enumerate(crs):
            cr_ref[c] = cr

    one = pl.BlockSpec((1, STATE), lambda i: (0, 0))
    wide = pl.BlockSpec((rows, inner), lambda i: (i, 0))
    return pl.pallas_call(
        body, name=name, grid=(t // rows,),
        in_specs=[pl.BlockSpec((rows, STATE), lambda i: (i, 0)), one, one],
        out_specs=[wide, wide, pl.BlockSpec((k, CHUNK, STATE), lambda i: (i, 0, 0))],
        out_shape=[jax.ShapeDtypeStruct((t, inner), F32), jax.ShapeDtypeStruct((t, inner), F32),
                   jax.ShapeDtypeStruct((t // CHUNK, CHUNK, STATE), F32)],
        compiler_params=_params(("arbitrary",)),
    )(dtr, dtb, alog)


def ssd_prep_bwd(name, dtr, dtb, alog, ddt, dcum, dcr, inner, k=4):
    t = dtr.shape[0]
    k = min(k, t // CHUNK)
    rows = k * CHUNK

    def body(dtr_ref, dtb_ref, alog_ref, ddt_ref, dcum_ref, dcr_ref, ddtr_ref, ddtb_ref, dalog_ref):
        @pl.when(pl.program_id(0) == 0)
        def _():
            ddtb_ref[...] = jnp.zeros_like(ddtb_ref)
            dalog_ref[...] = jnp.zeros_like(dalog_ref)
        _, vjp = jax.vjp(functools.partial(f_ssd_prep, inner=inner), dtr_ref[...], dtb_ref[...], alog_ref[...])
        ddtr, ddtb, dalog = vjp((ddt_ref[...], dcum_ref[...], tuple(dcr_ref[c] for c in range(k))))
        ddtr_ref[...] = ddtr
        ddtb_ref[...] += ddtb
        dalog_ref[...] += dalog

    one = pl.BlockSpec((1, STATE), lambda i: (0, 0))
    wide = pl.BlockSpec((rows, inner), lambda i: (i, 0))
    tall = pl.BlockSpec((rows, STATE), lambda i: (i, 0))
    return pl.pallas_call(
        body, name=name, grid=(t // rows,),
        in_specs=[tall, one, one, wide, wide, pl.BlockSpec((k, CHUNK, STATE), lambda i: (i, 0, 0))],
        out_specs=[tall, one, one],
        out_shape=[jax.ShapeDtypeStruct((t, STATE), F32), jax.ShapeDtypeStruct((1, STATE), F32),
                   jax.ShapeDtypeStruct((1, STATE), F32)],
        compiler_params=_params(("arbitrary",)),
    )(dtr, dtb, alog, ddt, dcum, dcr)


def _ssd_group(states, x, bm, cm, dt, cum, cumrs, dsk):
    half = _iota((CHUNK, STATE), 1) >= HEADDIM
    pos = _iota((CHUNK, STATE), 1) - jnp.where(half, HEADDIM, 0)
    row = _iota((CHUNK, STATE), 0)
    causal = row >= pos
    cb2 = mm_nt(cm, jnp.concatenate([bm, bm], axis=0))
    ys, new = [], []
    for pp, (s, cumr) in enumerate(zip(states, cumrs, strict=True)):
        lanes = slice(pp * STATE, (pp + 1) * STATE)
        cu, xdt = cum[:, lanes], x[:, lanes] * dt[:, lanes]
        tot = jnp.sum(jnp.where(row == CHUNK - 1, cu, 0.0), axis=0, keepdims=True)
        m = jnp.where(causal, cb2 * jnp.exp(jnp.where(causal, cu - cumr, 0.0)), 0.0)
        x2 = jnp.concatenate([jnp.where(half, 0.0, xdt), jnp.where(half, xdt, 0.0)], axis=0)
        ys.append(mm_nn(m, x2) + mm_nt(cm, s) * jnp.exp(cu) + x[:, lanes] * dsk[:, lanes])
        decay = jnp.broadcast_to(jnp.exp(tot), s.shape).T
        new.append(decay * s + mm_tn(xdt * jnp.exp(tot - cu), bm))
    return jnp.concatenate(ys, axis=1), tuple(new)


SSD_CHUNKS_PER_STEP = 4


def _ssd_specs(act, inner, groups, rev):
    t = act.shape[0]
    cb = min(SSD_CHUNKS_PER_STEP, t // CHUNK)
    rows = cb * CHUNK
    nblk = t // rows
    ch = (lambda c: nblk - 1 - c) if rev else (lambda c: c)
    gw = inner // groups
    wide = pl.BlockSpec((rows, gw), lambda c, g: (ch(c), g))
    specs = [wide,
             pl.BlockSpec((rows, STATE), lambda c, g: (ch(c), inner // STATE + g)),
             pl.BlockSpec((rows, STATE), lambda c, g: (ch(c), inner // STATE + groups + g)),
             wide, wide,
             pl.BlockSpec((cb, CHUNK, STATE), lambda c, g: (ch(c), 0, 0)),
             pl.BlockSpec((1, gw), lambda c, g: (0, g))]
    return cb, nblk, ch, gw, specs


def ssd_fwd(name, act, dt, cum, cumrow, dexp, inner, groups):
    t = act.shape[0]
    cb, nblk, ch, gw, specs = _ssd_specs(act, inner, groups, False)
    pairs = gw // (2 * HEADDIM)

    def body(x_ref, b_ref, c_ref, dt_ref, cum_ref, cr_ref, dsk_ref, y_ref, sv_ref, st):
        c, g = pl.program_id(0), pl.program_id(1)

        @pl.when(c == 0)
        def _():
            for pp in range(pairs):
                st[g * pairs + pp] = jnp.zeros((STATE, STATE), F32)

        states = tuple(st[g * pairs + pp] for pp in range(pairs))
        for k in range(cb):
            rs = slice(k * CHUNK, (k + 1) * CHUNK)
            cumrs = tuple(cr_ref[k, pl.ds(g * pairs + pp, 1), :] for pp in range(pairs))
            for pp in range(pairs):
                sv_ref[k, pp] = states[pp]
            y, states = _ssd_group(states, x_ref[rs, :], b_ref[rs, :], c_ref[rs, :], dt_ref[rs, :], cum_ref[rs, :],
                                   cumrs, dsk_ref[...])
            y_ref[rs, :] = y
        for pp in range(pairs):
            st[g * pairs + pp] = states[pp]

    return pl.pallas_call(
        body, name=name, grid=(nblk, groups), in_specs=specs,
        out_specs=[pl.BlockSpec((cb * CHUNK, gw), lambda c, g: (c, g)),
                   pl.BlockSpec((cb, pairs, STATE, STATE), lambda c, g: (c, g, 0, 0))],
        out_shape=[jax.ShapeDtypeStruct((t, inner), F32),
                   jax.ShapeDtypeStruct((t // CHUNK, groups * pairs, STATE, STATE), F32)],
        scratch_shapes=[pltpu.VMEM((groups * pairs, STATE, STATE), F32)],
        compiler_params=_params(("arbitrary", "arbitrary")),
    )(act, act, act, dt, cum, cumrow, dexp)


def ssd_bwd(name, act, dt, cum, cumrow, dexp, states, dy, inner, groups):
    t = act.shape[0]
    cb, nblk, ch, gw, specs = _ssd_specs(act, inner, groups, True)
    pairs = gw // (2 * HEADDIM)

    def body(x_ref, b_ref, c_ref, dt_ref, cum_ref, cr_ref, dsk_ref, sv_ref, dy_ref,
             dx_ref, db_ref, dc_ref, ddt_ref, dcum_ref, dcr_ref, ddsk_ref, dst):
        c, g = pl.program_id(0), pl.program_id(1)

        @pl.when((c == 0) & (g == 0))
        def _():
            ddsk_ref[...] = jnp.zeros_like(ddsk_ref)

        @pl.when(c == 0)
        def _():
            for pp in range(pairs):
                dst[g * pairs + pp] = jnp.zeros((STATE, STATE), F32)

        @pl.when(g == 0)
        def _():
            dcr_ref[...] = jnp.zeros_like(dcr_ref)

        ds = tuple(dst[g * pairs + pp] for pp in range(pairs))
        ddsk = jnp.zeros((1, gw), F32)
        for k in reversed(range(cb)):
            rs = slice(k * CHUNK, (k + 1) * CHUNK)
            cumrs = tuple(cr_ref[k, pl.ds(g * pairs + pp, 1), :] for pp in range(pairs))
            _, vjp = jax.vjp(_ssd_group, tuple(sv_ref[k, pp] for pp in range(pairs)), x_ref[rs, :], b_ref[rs, :],
                             c_ref[rs, :], dt_ref[rs, :], cum_ref[rs, :], cumrs, dsk_ref[...])
            ds, dx, db, dc, ddt, dcum, dcrs, ddskk = vjp((dy_ref[rs, :], ds))
            for pp in range(pairs):
                dcr_ref[k, pl.ds(g * pairs + pp, 1), :] = dcrs[pp]
            dx_ref[rs, :] = dx
            db_ref[rs, :] = db
            dc_ref[rs, :] = dc
            ddt_ref[rs, :] = ddt
            dcum_ref[rs, :] = dcum
            ddsk = ddsk + ddskk
        for pp in range(pairs):
            dst[g * pairs + pp] = ds[pp]
        col = pl.ds(pl.multiple_of(g * gw, STATE), gw)
        ddsk_ref[:, col] = ddsk_ref[:, col] + ddsk

    wide = pl.BlockSpec((cb * CHUNK, gw), lambda c, g: (ch(c), g))
    grp = pl.BlockSpec((cb * CHUNK, STATE), lambda c, g: (ch(c), g))
    return pl.pallas_call(
        body, name=name, grid=(nblk, groups),
        in_specs=specs + [pl.BlockSpec((cb, pairs, STATE, STATE), lambda c, g: (ch(c), g, 0, 0)), wide],
        out_specs=[wide, grp, grp, wide, wide, pl.BlockSpec((cb, CHUNK, STATE), lambda c, g: (ch(c), 0, 0)),
                   pl.BlockSpec((1, inner), lambda c, g: (0, 0))],
        out_shape=[jax.ShapeDtypeStruct((t, inner), F32), jax.ShapeDtypeStruct((t, groups * STATE), F32),
                   jax.ShapeDtypeStruct((t, groups * STATE), F32), jax.ShapeDtypeStruct((t, inner), F32),
                   jax.ShapeDtypeStruct((t, inner), F32), jax.ShapeDtypeStruct((t // CHUNK, CHUNK, STATE), F32),
                   jax.ShapeDtypeStruct((1, inner), F32)],
        scratch_shapes=[pltpu.VMEM((groups * pairs, STATE, STATE), F32)],
        compiler_params=_params(("arbitrary", "arbitrary")),
    )(act, act, act, dt, cum, cumrow, dexp, states, dy)


def _ssd_conv_width(inner, cdim):
    return _divisor(math.gcd(inner, cdim), 2048, 128)


def ssd_conv_fwd(name, zx, cw, cb, inner):
    t, cdim = zx.shape[0], cw.shape[1]
    wc = _ssd_conv_width(inner, cdim)

    def fn(i, j, xt, xp, w, b):
        return [jax.nn.silu(_conv(xt, xp, w, b, i == 0))], []
    ins = [Row(zx, wc, inner // wc), Prev(zx, wc, inner // wc), Full(cw, wc), Full(cb, wc)]
    return rows_call(name, fn, t, _divisor(t, 256, 8), cdim // wc, ins, outs=[(F32, wc)])[0]


def ssd_conv_bwd(name, zx, cw, cb, dact, inner):
    t, cdim = zx.shape[0], cw.shape[1]
    wc = _ssd_conv_width(inner, cdim)
    kk = cw.shape[0]

    def fn(i, j, xt, xp, w, b, da):
        first = i == 0
        pre = _conv(xt, xp, w, b, first)
        _, vjp = jax.vjp(jax.nn.silu, pre)
        dpre, = vjp(da)
        return [dpre], [_conv_wgrad(xt, xp, dpre, kk, first), jnp.sum(dpre, axis=0, keepdims=True)]
    ins = [Row(zx, wc, inner // wc), Prev(zx, wc, inner // wc), Full(cw, wc), Full(cb, wc), Row(dact, wc)]
    return rows_call(name, fn, t, _divisor(t, 256, 8), cdim // wc, ins, outs=[(F32, wc)],
                     accs=[(SUBLANES, wc), (1, wc)])


def f_ssd_post(y, z, nw):
    yz = y * jax.nn.silu(z)
    return yz * lax.rsqrt(jnp.mean(yz * yz, axis=-1, keepdims=True) + EPS) * nw


def ssd_post_fwd(name, y, zx, nw, groups):
    t, inner = y.shape
    gw = inner // groups

    def fn(i, j, yt, zt, nwt):
        return [f_ssd_post(yt, zt, nwt)], []
    return rows_call(name, fn, t, _divisor(t, 1024, 8), groups, [Row(y, gw), Row(zx, gw), Full(nw, gw)],
                     outs=[(BF16, gw)])[0]


def ssd_post_bwd(name, y, zx, nw, dyn, groups):
    t, inner = y.shape
    gw = inner // groups

    def fn(i, j, yt, zt, nwt, dt):
        _, vjp = jax.vjp(f_ssd_post, yt, zt, nwt)
        dy, dz, dnw = vjp(dt)
        return [dy, dz], [dnw]
    return rows_call(name, fn, t, _divisor(t, 512, 8), groups, [Row(y, gw), Row(zx, gw), Full(nw, gw), Row(dyn, gw)],
                     outs=[(F32, gw), (BF16, gw)], accs=[(1, gw)])


ADAMW_TILE = 256 * 1024


def adamw(name, slots, w, m, v, layer=0, row0=0, prev=None):
    nl, _, c = w.shape
    n = slots.shape[1]
    tr = _divisor(math.gcd(row0, n), max(16, ADAMW_TILE // c // 16 * 16), 16)
    first = row0 // tr

    def body(s_ref, w_ref, m_ref, v_ref, *rest):
        g_ref, d_ref, mo_ref, vo_ref = rest[-4:]
        g = s_ref[0].astype(F32)
        for k in range(1, N_DEV):
            g = g + s_ref[k].astype(F32)
        mn = ADAM_B1 * m_ref[...] + (1.0 - ADAM_B1) * g
        vn = ADAM_B2 * v_ref[...] + (1.0 - ADAM_B2) * (g * g)
        m_hat = mn / (1.0 - ADAM_B1 ** ADAM_STEP)
        v_hat = vn / (1.0 - ADAM_B2 ** ADAM_STEP)
        g_ref[...] = g
        d_ref[...] = -ADAM_LR * (m_hat / (jnp.sqrt(v_hat) + ADAM_EPS) + ADAM_WD * w_ref[...])
        mo_ref[...] = mn
        vo_ref[...] = vn

    spec = pl.BlockSpec((None, tr, c), lambda i: (layer, first + i, 0))
    prev = list(prev) if prev is not None else []
    return pl.pallas_call(
        body, name=name, grid=(n // tr,),
        in_specs=[pl.BlockSpec((N_DEV, tr, c), lambda i: (0, i, 0)), spec, spec, spec] + [_HBM] * len(prev),
        out_specs=[spec] * 4, out_shape=[jax.ShapeDtypeStruct(w.shape, F32)] * 4,
        input_output_aliases={4 + k: k for k in range(len(prev))},
        compiler_params=_params(("parallel",)),
    )(slots, w, m, v, *prev)


def all_gather(name, shard):
    def body(x_ref, out_ref, send_sems, recv_sems, local_sem):
        x, y, c = lax.axis_index("x"), lax.axis_index("y"), lax.axis_index("c")
        me, sibling = (x, y, c), (x, y, 1 - c)
        chips = [(1 - x, y), (x, 1 - y), (1 - x, 1 - y)]

        def slab(px, py, pc):
            return out_ref.at[4 * px + 2 * py + pc]

        def copy(k, block, to, src=None):
            return pltpu.make_async_remote_copy(
                src_ref=slab(*block) if src is None else src, dst_ref=slab(*block),
                send_sem=send_sems.at[k], recv_sem=recv_sems.at[k], device_id=to, device_id_type=_MESH)

        mine = pltpu.make_async_copy(x_ref, slab(*me), local_sem)
        mine.start()
        first = [copy(0, me, sibling, src=x_ref)]
        first += [copy(1 + j, me, (*chip, c), src=x_ref) for j, chip in enumerate(chips)]
        for cp in first:
            cp.start()
        passed = [copy(4 + j, (*chip, c), sibling) for j, chip in enumerate(chips)]
        for j, chip in enumerate(chips):
            copy(1 + j, (*chip, c), me).wait_recv()
            passed[j].start()
        copy(0, sibling, me).wait_recv()
        for j, chip in enumerate(chips):
            copy(4 + j, (*chip, 1 - c), me).wait_recv()
        for cp in first + passed:
            cp.wait_send()
        mine.wait()

    return pl.pallas_call(
        body, name=name, out_shape=jax.ShapeDtypeStruct((N_DEV,) + shard.shape, shard.dtype),
        in_specs=[_HBM], out_specs=_HBM,
        scratch_shapes=[pltpu.SemaphoreType.DMA((7,)), pltpu.SemaphoreType.DMA((7,)), pltpu.SemaphoreType.DMA],
    )(shard)


def exchange(name, pieces):
    def body(p_ref, out_ref, send_sems, recv_sems, local_sem):
        local, remote = _peer_copies("exchange", p_ref, out_ref, send_sems, recv_sems, local_sem)
        local.start()
        for cp in remote:
            cp.start()
        for cp in remote:
            cp.wait_recv()
        for cp in remote:
            cp.wait_send()
        local.wait()

    return pl.pallas_call(
        body, name=name, out_shape=jax.ShapeDtypeStruct(pieces.shape, pieces.dtype),
        in_specs=[_HBM], out_specs=_HBM, scratch_shapes=_PEER_SEMS,
    )(pieces)


WEIGHTS = ['norm_mix_w', 'norm_ffn_w', 'norm_final_w', 'ev_w_in', 'lru_conv_w', 'lru_conv_b', 'lru_w_r', 'lru_b_r',
           'lru_w_i', 'lru_b_i', 'lru_lambda', 'hg_lower_bounds', 'hg_norm_w', 'ev_w_out', 'ssd_w_in', 'ssd_conv_w',
           'ssd_conv_b', 'ssd_dt_bias', 'ssd_a_log', 'ssd_d', 'ssd_norm_w', 'ssd_w_out', 'ffn_w_up', 'ffn_conv_w',
           'ffn_conv_b', 'ffn_w_down']
COL_SHARDED = ['ev_w_in', 'ssd_w_in', 'ffn_w_up']
ROW_SHARDED = ['ev_w_out', 'ssd_w_out', 'ffn_w_down']
BIG = ['ev_w_in', 'ev_w_out', 'ssd_w_out', 'ffn_w_up', 'ffn_w_down', 'ssd_w_in']
SMALL_SHARDED = ['lru_conv_w', 'ssd_conv_w', 'ssd_conv_b', 'ssd_norm_w', 'ffn_conv_w']
SMALL = [n for n in WEIGHTS if n not in BIG]
SEG = 16 * 128


def _pad_to(flat, mult):
    extra = (-flat.shape[-1]) % mult
    if extra == 0:
        return flat
    return jnp.pad(flat, [(0, 0)] * (flat.ndim - 1) + [(0, extra)])


def _pack(segments):
    offs, parts, at = [], [], 0
    for s in segments:
        s = _pad_to(s, SEG)
        offs.append(at)
        at += s.shape[-1]
        parts.append(s)
    buf = jnp.concatenate(parts, axis=-1)
    return buf.reshape(buf.shape[:-1] + (at // 128, 128)), offs


def _unshard_last(g):
    g = jnp.moveaxis(g, 0, -2)
    return g.reshape(g.shape[:-2] + (N_DEV * g.shape[-1],))


def _block_diag(w):
    nb, b, _ = w.shape
    return (w[:, :, None, :] * jnp.eye(nb, dtype=w.dtype)[:, None, :, None]).reshape(nb * b, nb * b)


def _diag_blocks(dense, nb):
    b = dense.shape[0] // nb
    d4 = dense.reshape(nb, b, nb, b)
    return jnp.stack([d4[h, :, h, :] for h in range(nb)])


def kernel(x, norm_mix_w, norm_ffn_w, norm_final_w, ev_w_in, lru_conv_w, lru_conv_b, lru_w_r, lru_b_r, lru_w_i, lru_b_i, lru_lambda, hg_lower_bounds, hg_norm_w, ev_w_out, ssd_w_in, ssd_conv_w, ssd_conv_b, ssd_dt_bias, ssd_a_log, ssd_d, ssd_norm_w, ssd_w_out, ffn_w_up, ffn_conv_w, ffn_conv_b, ffn_w_down, loss_target, m_norm_mix_w, m_norm_ffn_w, m_norm_final_w, m_ev_w_in, m_lru_conv_w, m_lru_conv_b, m_lru_w_r, m_lru_b_r, m_lru_w_i, m_lru_b_i, m_lru_lambda, m_hg_lower_bounds, m_hg_norm_w, m_ev_w_out, m_ssd_w_in, m_ssd_conv_w, m_ssd_conv_b, m_ssd_dt_bias, m_ssd_a_log, m_ssd_d, m_ssd_norm_w, m_ssd_w_out, m_ffn_w_up, m_ffn_conv_w, m_ffn_conv_b, m_ffn_w_down, v_norm_mix_w, v_norm_ffn_w, v_norm_final_w, v_ev_w_in, v_lru_conv_w, v_lru_conv_b, v_lru_w_r, v_lru_b_r, v_lru_w_i, v_lru_b_i, v_lru_lambda, v_hg_lower_bounds, v_hg_norm_w, v_ev_w_out, v_ssd_w_in, v_ssd_conv_w, v_ssd_conv_b, v_ssd_dt_bias, v_ssd_a_log, v_ssd_d, v_ssd_norm_w, v_ssd_w_out, v_ffn_w_up, v_ffn_conv_w, v_ffn_conv_b, v_ffn_w_down):
    given = dict(locals())
    wts = {n: given[n] for n in WEIGHTS}
    mom1 = {n: given["m_" + n] for n in WEIGHTS}
    mom2 = {n: given["v_" + n] for n in WEIGHTS}
    me = 4 * lax.axis_index("x") + 2 * lax.axis_index("y") + lax.axis_index("c")

    depth, d = norm_mix_w.shape
    t = x.shape[1]
    x0 = x.reshape(t, d)
    target = loss_target.reshape(t, d)
    n_even, lru_w = lru_lambda.shape
    hg_w = hg_lower_bounds.shape[1]
    n_odd, heads = ssd_dt_bias.shape
    inner = N_DEV * ssd_norm_w.shape[1]
    cdim = N_DEV * ssd_conv_b.shape[1]
    groups = (cdim - inner) // (2 * STATE)
    assert inner == heads * HEADDIM and heads <= STATE and (inner // groups) % (2 * HEADDIM) == 0

    full = {}
    shard_buf, offs = _pack([wts[n].reshape(-1) for n in SMALL_SHARDED])
    gathered = all_gather("ag_small", shard_buf).reshape(N_DEV, -1)
    for n, off in zip(SMALL_SHARDED, offs, strict=True):
        full[n] = _unshard_last(gathered[:, off:off + wts[n].size].reshape((N_DEV,) + wts[n].shape))
    shard16 = {n: wts[n].astype(BF16) for n in BIG}
    mixer = lambda l: (('ev_w_in', l // 2), ('ev_w_out', l // 2)) if l % 2 == 0 else (('ssd_w_in', l // 2), ('ssd_w_out', l // 2))
    weight = {}

    as_pieces = {n: n in COL_SHARDED and wts[n].shape[2] % 128 == 0 for n in BIG}

    def arrived(key, g8):
        if as_pieces[key[0]]:
            weight[key] = g8
        elif key[0] in COL_SHARDED:
            weight[key] = jnp.transpose(g8, (1, 0, 2)).reshape(g8.shape[1], -1)
        else:
            weight[key] = g8.reshape(-1, g8.shape[2])

    def gather_ride(key):
        return ("gather", shard16[key[0]][key[1]], None)

    for key in mixer(0):
        arrived(key, all_gather(f"ag_{key[0]}", shard16[key[0]][key[1]]))
    w_zx = lambda o: weight['ssd_w_in', o][:, :inner + cdim]
    w_dt = lambda o: jnp.pad(weight['ssd_w_in', o][:, inner + cdim:], ((0, 0), (0, STATE - heads)))
    pad_h = lambda a: jnp.pad(a.reshape(1, heads), ((0, 0), (0, STATE - heads)))
    row = lambda a: a.reshape(1, -1)

    lbs = whole_call("lbs_fwd", lambda hb: [f_lbs(hb)], [hg_lower_bounds], [hg_lower_bounds.shape])[0]

    def mm_carry(name, a, b, key, **kw):
        if key is None or key in weight:
            return matmul(name, a, b, **kw)
        out, (g8,) = matmul(name, a, b, rides=[gather_ride(key)], **kw)
        arrived(key, g8)
        return out

    saved = []
    xcur = x0
    for l in range(depth):
        sv = {'x0': xcur}
        k_in, k_out = mixer(l)
        nxt_in, nxt_out = mixer(l + 1) if l + 1 < depth else (None, None)
        h = rms_fwd(f"rms_mix_fwd", xcur, row(norm_mix_w[l]))
        sv['h'] = h
        if l % 2 == 0:
            e = l // 2
            wr = _block_diag(lru_w_r[e]).astype(BF16)
            wi = _block_diag(lru_w_i[e]).astype(BF16)
            lru_p = (full['lru_conv_w'][e], row(lru_conv_b[e]), wr, row(lru_b_r[e]), wi, row(lru_b_i[e]), row(lru_lambda[e]))
            proj = mm_carry("ev_in", h, weight[k_in], ('ffn_w_up', l), b_pieces=as_pieces[k_in[0]])
            xc, a, u = lru_pre_fwd(f"lru_pre_fwd", proj, *lru_p)
            hseq = lin_scan(f"lru_scan_fwd", a, u, False)
            ob, states = hg_fwd(f"hg_fwd", proj, row(lbs[e]), lru_w)
            y = even_post_fwd(f"even_post_fwd", proj, hseq, ob, row(hg_norm_w[e]))
            xmid = mm_carry("ev_out", y, weight[k_out], ('ffn_w_down', l), res=xcur)
            sv.update(proj=proj, xc=xc, a=a, hseq=hseq, ob=ob, states=states, y=y, lru_p=lru_p)
        else:
            o = l // 2
            ssd_p = (pad_h(ssd_dt_bias[o]), pad_h(ssd_a_log[o]), row(jnp.repeat(ssd_d[o], HEADDIM)))
            wzx, wdt = w_zx(o), w_dt(o)
            zx = mm_carry("ssd_in", h, wzx, ('ffn_w_up', l))
            dtr = matmul("ssd_dt", h, wdt)
            sv.update(wzx=wzx, wdt=wdt)
            act = ssd_conv_fwd(f"ssd_conv_fwd", zx, full['ssd_conv_w'][o], row(full['ssd_conv_b'][o]), inner)
            prep = ssd_prep_fwd("ssd_prep_fwd", dtr, ssd_p[0], ssd_p[1], inner)
            ys, states = ssd_fwd(f"ssd_fwd", act, *prep, ssd_p[2], inner, groups)
            sv['prep'] = prep
            yn = ssd_post_fwd(f"ssd_post_fwd", ys, zx, row(full['ssd_norm_w'][o]), groups)
            xmid = mm_carry("ssd_out", yn, weight[k_out], ('ffn_w_down', l), res=xcur)
            sv.update(zx=zx, dtr=dtr, act=act, ys=ys, states=states, yn=yn, ssd_p=ssd_p)
        h2 = rms_fwd(f"rms_ffn_fwd", xmid, row(norm_ffn_w[l]))
        u0 = mm_carry("ffn_up", h2, weight['ffn_w_up', l], nxt_in, b_pieces=as_pieces['ffn_w_up'])
        actf = ffn_act_fwd(f"ffn_act_fwd", u0, full['ffn_conv_w'][l], row(ffn_conv_b[l]))
        xcur = mm_carry("ffn_down", actf, weight['ffn_w_down', l], nxt_out, res=xmid)
        sv.update(x1=xmid, h2=h2, u0=u0, actf=actf)
        saved.append(sv)

    gcur, d_nfw, loss_row = loss_head("loss_head", xcur, row(norm_final_w), target)
    loss = lax.psum(loss_row[0, 0], ("x", "y", "c"))

    gl = {n: [None] * wts[n].shape[0] for n in SMALL if n != 'norm_final_w'}
    d_lbs = [None] * n_even
    slots = {}
    dw_opts = dict(ta=True, out_dtype=BF16, tm=2048, tk=1024)

    def pieces(key, dw):
        if as_pieces[key[0]]:
            return dw
        if key[0] in COL_SHARDED:
            return jnp.transpose(dw.reshape(dw.shape[0], N_DEV, -1), (1, 0, 2))
        return dw.reshape(N_DEV, -1, dw.shape[1])

    def halves(key, p):
        r = p.shape[1] // 2
        return ((*key, 0), p, (0, r)), ((*key, 1), p, (r, p.shape[1] - r))

    def whole(key, p):
        return ((*key, 0), p, None)

    def mm_send(name, a, b, sends, **kw):
        if not sends:
            return matmul(name, a, b, **kw)
        out, got = matmul(name, a, b, rides=[("exchange", p, rows) for _, p, rows in sends], **kw)
        for (key, _, rows), g8 in zip(sends, got, strict=True):
            slots[key] = (g8, 0 if rows is None else rows[0])
        return out

    carried = {}
    for l in reversed(range(depth)):
        sv = saved[l]
        k_in, k_out = mixer(l)
        fcw = full['ffn_conv_w'][l]
        dact = mm_send("ffn_down_dx", gcur, weight['ffn_w_down', l], carried.get('down_dx', ()), tb=True)
        dw_down = mm_send("ffn_down_dw", sv['actf'], gcur, carried.get('down_dw', ()), **dw_opts)
        dug, duv, dwg, dwv, dbg, dbv = ffn_act_bwd(f"ffn_act_bwd", sv['u0'], fcw, row(ffn_conv_b[l]), dact)
        kf = fcw.shape[0]
        gl['ffn_conv_w'][l] = jnp.concatenate([dwg[:kf], dwv[:kf]], axis=1)
        gl['ffn_conv_b'][l] = jnp.concatenate([dbg, dbv], axis=1)[0]
        du0 = jnp.concatenate([dug, duv], axis=1)
        up_pieces = wts['ffn_w_up'].shape[2] if as_pieces['ffn_w_up'] else 0
        dh2 = mm_send("ffn_up_dx", du0, weight['ffn_w_up', l], [whole(('ffn_w_down', l), pieces(('ffn_w_down', l), dw_down))],
                      tb=True, b_pieces=as_pieces['ffn_w_up'])
        dw_up = mm_send("ffn_up_dw", sv['h2'], du0, carried.get('up_dw', ()), out_pieces=up_pieces, **dw_opts)
        up_a, up_b = halves(('ffn_w_up', l), pieces(('ffn_w_up', l), dw_up))
        in_dx_sends = [up_a]
        gmid, dnw = rms_bwd(f"rms_ffn_bwd", sv['x1'], row(norm_ffn_w[l]), dh2, gcur)
        gl['norm_ffn_w'][l] = dnw[0]
        if l % 2 == 0:
            e = l // 2
            proj, lru_p = sv['proj'], sv['lru_p']
            dy = matmul("ev_out_dx", gmid, weight[k_out], tb=True)
            dw_out = matmul("ev_out_dw", sv['y'], gmid, **dw_opts)
            dhs, dga, dob, dgb, dhn = even_post_bwd(f"even_post_bwd", proj, sv['hseq'], sv['ob'], row(hg_norm_w[e]), dy)
            gl['hg_norm_w'][e] = dhn[0]
            dq, df, di, dlb = hg_bwd(f"hg_bwd", proj, row(lbs[e]), sv['states'], dob, lru_w)
            d_lbs[e] = dlb
            lamb = lin_scan(f"lru_scan_bwd", sv['a'], dhs, True)
            dxc, dcw, dcb, dwr, dbr, dwi, dbi, dlam = lru_pre_bwd(f"lru_pre_bwd", proj, *lru_p, sv['xc'], sv['hseq'], lamb)
            nb = lru_w_r.shape[1]
            gl['lru_conv_w'][e], gl['lru_conv_b'][e] = dcw[:lru_p[0].shape[0]], dcb[0]
            gl['lru_w_r'][e], gl['lru_b_r'][e] = _diag_blocks(dwr, nb), dbr[0]
            gl['lru_w_i'][e], gl['lru_b_i'][e] = _diag_blocks(dwi, nb), dbi[0]
            gl['lru_lambda'][e] = dlam[0]
            dxa = conv_t_call(f"lru_convt", dxc, lru_p[0], 0, lru_w)
            dproj = jnp.concatenate([dxa, dga, dq, df, di, dgb], axis=1)
            in_dw_sends = [up_b] + ([whole(k_out, pieces(k_out, dw_out))] if l == 0 else [])
            dh = mm_send("ev_in_dx", dproj, weight[k_in], in_dx_sends, tb=True, b_pieces=as_pieces[k_in[0]])
            dw_in = mm_send("ev_in_dw", sv['h'], dproj, in_dw_sends,
                            out_pieces=wts[k_in[0]].shape[2] if as_pieces[k_in[0]] else 0, **dw_opts)
        else:
            o = l // 2
            zx, scw = sv['zx'], full['ssd_conv_w'][o]
            dyn = matmul("ssd_out_dx", gmid, weight[k_out], tb=True)
            dw_out = matmul("ssd_out_dw", sv['yn'], gmid, **dw_opts)
            dys, dz, dnw = ssd_post_bwd(f"ssd_post_bwd", sv['ys'], zx, row(full['ssd_norm_w'][o]), dyn, groups)
            gl['ssd_norm_w'][o] = dnw[0]
            dxs, dbm, dcm, ddt, dcum, dcr, ddexp = ssd_bwd(f"ssd_bwd", sv['act'], *sv['prep'], sv['ssd_p'][2],
                                                          sv['states'], dys, inner, groups)
            ddtr, ddtb, dalog = ssd_prep_bwd("ssd_prep_bwd", sv['dtr'], sv['ssd_p'][0], sv['ssd_p'][1], ddt, dcum, dcr, inner)
            gl['ssd_dt_bias'][o], gl['ssd_a_log'][o] = ddtb[0, :heads], dalog[0, :heads]
            gl['ssd_d'][o] = jnp.sum(ddexp.reshape(heads, HEADDIM), axis=1)
            dact = jnp.concatenate([dxs, dbm, dcm], axis=1)
            dpre, dcw, dcb = ssd_conv_bwd(f"ssd_conv_bwd", zx, scw, row(full['ssd_conv_b'][o]), dact, inner)
            gl['ssd_conv_w'][o], gl['ssd_conv_b'][o] = dcw[:scw.shape[0]], dcb[0]
            dxbc = conv_t_call(f"ssd_convt", dpre, scw, 0, _ssd_conv_width(inner, cdim))
            dzx = jnp.concatenate([dz, dxbc], axis=1)
            in_dw_sends = [up_b] + ([whole(k_out, pieces(k_out, dw_out))] if l == 0 else [])
            dh = mm_send("ssd_in_dx", dzx, sv['wzx'], in_dx_sends, tb=True)
            dh = matmul("ssd_dt_dx", ddtr, sv['wdt'], tb=True, res=dh)
            dwzx = mm_send("ssd_in_dw", sv['h'], dzx, in_dw_sends, **dw_opts)
            dwdt = matmul("ssd_dt_dw", sv['h'], ddtr, **dw_opts)
            dw_in = jnp.concatenate([dwzx, dwdt[:, :heads]], axis=1)
        if l > 0:
            in_a, in_b = halves(k_in, pieces(k_in, dw_in))
            carried = {'down_dx': [in_a], 'down_dw': [in_b], 'up_dw': [whole(k_out, pieces(k_out, dw_out))]}
        gcur, dnw = rms_bwd(f"rms_mix_bwd", sv['x0'], row(norm_mix_w[l]), dh, gmid)
        gl['norm_mix_w'][l] = dnw[0]

    def lbs_bwd(hb, dl):
        _, vjp = jax.vjp(f_lbs, hb)
        return [vjp(dl)[0]]
    d_hlb = whole_call("lbs_bwd", lbs_bwd, [hg_lower_bounds, jnp.concatenate(d_lbs, axis=0)], [hg_lower_bounds.shape])[0]

    part = {n: jnp.stack(v) for n, v in gl.items() if n != 'hg_lower_bounds'}
    part['hg_lower_bounds'] = d_hlb
    part['norm_final_w'] = d_nfw[0]

    slots[(*k_in, 0)] = (exchange("rs_last", pieces(k_in, dw_in)), 0)
    out = {}
    for n in BIG:
        res = None
        for idx in range(wts[n].shape[0]):
            for half_no in range(2):
                if (n, idx, half_no) in slots:
                    got, row0 = slots[n, idx, half_no]
                    res = adamw(f"adamw_{n}", got, wts[n], mom1[n], mom2[n], idx, row0, res)
        out[n] = res

    small_buf, _ = _pack([jnp.concatenate([part[n].astype(F32).reshape(-1) for n in SMALL])])
    sm = all_gather("ag_small_grads", small_buf).reshape(N_DEV, -1)
    own, at = [], 0
    for n in SMALL:
        size = part[n].size
        g8 = sm[:, at:at + size].reshape((N_DEV,) + part[n].shape)
        at += size
        if n in SMALL_SHARDED:
            g8 = lax.dynamic_slice_in_dim(g8, me * wts[n].shape[-1], wts[n].shape[-1], axis=g8.ndim - 1)
        own.append(g8.reshape(N_DEV, -1))
    sslots, _ = _pack([jnp.concatenate(own, axis=1)])
    cat = lambda dct: _pad_to(jnp.concatenate([dct[n].reshape(-1) for n in SMALL]), SEG).reshape(1, -1, 128)
    res = adamw("adamw_small", sslots, cat(wts), cat(mom1), cat(mom2))
    at = 0
    for n in SMALL:
        out[n] = [r.reshape(-1)[at:at + wts[n].size].reshape(wts[n].shape) for r in res]
        at += wts[n].size

    grad_x = gcur.reshape(x.shape)
    return (loss, grad_x, *[out[n][0] for n in WEIGHTS], *[out[n][1] for n in WEIGHTS],
            *[out[n][2] for n in WEIGHTS], *[out[n][3] for n in WEIGHTS])
```

```python
import functools
import math

import jax
import jax.numpy as jnp
from jax import lax
from jax.experimental import pallas as pl
from jax.experimental.pallas import tpu as pltpu

F32 = jnp.float32
BF16 = jnp.bfloat16
MXU = jnp.bfloat16
HI = lax.Precision.HIGHEST

N_DEV = 8
EPS = 1e-6
LRU_C = 8.0
CHUNK = 64
STATE = 128
ADAM_LR, ADAM_B1, ADAM_B2, ADAM_EPS, ADAM_WD, ADAM_STEP = 0.001, 0.9, 0.999, 1e-08, 0.01, 10

SUBLANES = 8
VMEM_LIMIT = 56 * 1024 * 1024

NN = (((1,), (0,)), ((), ()))
NT = (((1,), (1,)), ((), ()))
TN = (((0,), (0,)), ((), ()))


def _params(sem):
    return pltpu.CompilerParams(dimension_semantics=sem, vmem_limit_bytes=VMEM_LIMIT)


def _divisor(n, target, align):
    if n <= target:
        return n
    best = None
    for d in range(align, target + 1, align):
        if n % d == 0:
            best = d
    assert best is not None, (n, target, align)
    return best


def _mm(a, b, dn):
    return lax.dot_general(a.astype(MXU), b.astype(MXU), dn, preferred_element_type=F32)


@jax.custom_vjp
def mm_nn(a, b):
    return _mm(a, b, NN)


@jax.custom_vjp
def mm_nt(a, b):
    return _mm(a, b, NT)


@jax.custom_vjp
def mm_tn(a, b):
    return _mm(a, b, TN)


mm_nn.defvjp(lambda a, b: (_mm(a, b, NN), (a, b)), lambda r, g: (mm_nt(g, r[1]), mm_tn(r[0], g)))
mm_nt.defvjp(lambda a, b: (_mm(a, b, NT), (a, b)), lambda r, g: (mm_nn(g, r[1]), mm_tn(g, r[0])))
mm_tn.defvjp(lambda a, b: (_mm(a, b, TN), (a, b)), lambda r, g: (mm_nt(r[1], g), mm_nn(r[0], g)))


def dot_hi(a, b, dn=NN):
    return lax.dot_general(a, b, dn, precision=HI, preferred_element_type=F32)


def _onehot_mm(x, onehot, dn):
    hi = x.astype(MXU)
    rest = x - hi.astype(F32)
    mid = rest.astype(MXU)
    lo = (rest - mid.astype(F32)).astype(MXU)
    oh = onehot.astype(MXU)
    return sum(lax.dot_general(p, oh, dn, preferred_element_type=F32) for p in (hi, mid, lo))


@jax.custom_vjp
def spread_dot(x, onehot):
    return _onehot_mm(x, onehot, NN)


spread_dot.defvjp(lambda x, onehot: (_onehot_mm(x, onehot, NN), onehot),
                  lambda onehot, g: (_onehot_mm(g, onehot, NT), jnp.zeros_like(onehot)))


def _iota(shape, dim):
    return lax.broadcasted_iota(jnp.int32, shape, dim)


def _tril(n):
    return (_iota((n, n), 0) >= _iota((n, n), 1)).astype(F32)


def _softplus(x):
    return jnp.maximum(x, 0.0) + jnp.log1p(jnp.exp(-jnp.abs(x)))


def _neg_expm1(x):
    series = -x * (1.0 + x * (0.5 + x * (1.0 / 6.0 + x * (1.0 / 24.0))))
    return jnp.where(x > -0.03, series, 1.0 - jnp.exp(x))


_HBM = pl.BlockSpec(memory_space=pltpu.HBM)
_MESH = pl.DeviceIdType.MESH


def _peer_copies(kind, src_ref, dst_ref, send_sems, recv_sems, local_sem, rows=None):
    x, y, c = lax.axis_index("x"), lax.axis_index("y"), lax.axis_index("c")
    me = 4 * x + 2 * y + c
    if kind == "gather":
        pick = lambda p: src_ref
    elif rows is None:
        pick = lambda p: src_ref.at[p]
    else:
        pick = lambda p: src_ref.at[p, pl.ds(rows[0], rows[1])]
    local = pltpu.make_async_copy(pick(me), dst_ref.at[me], local_sem)
    remote = []
    for k in range(N_DEV - 1):
        px, py, pc = (x + ((k + 1) >> 2 & 1)) % 2, (y + ((k + 1) >> 1 & 1)) % 2, (c + ((k + 1) & 1)) % 2
        remote.append(pltpu.make_async_remote_copy(
            src_ref=pick(4 * px + 2 * py + pc), dst_ref=dst_ref.at[me], send_sem=send_sems.at[k],
            recv_sem=recv_sems.at[k], device_id=(px, py, pc), device_id_type=_MESH))
    return local, remote


_PEER_SEMS = [pltpu.SemaphoreType.DMA((N_DEV - 1,)), pltpu.SemaphoreType.DMA((N_DEV - 1,)), pltpu.SemaphoreType.DMA]


def matmul(name, a, b, *, ta=False, tb=False, res=None, out_dtype=F32, tm=1024, tn=1024, tk=2048, rides=(),
           b_pieces=False, out_pieces=0):
    m, k = (a.shape[1], a.shape[0]) if ta else a.shape
    if b_pieces:
        shard = b.shape[2]
        n = b.shape[1] if tb else b.shape[0] * shard
        assert (b.shape[0] * shard if tb else b.shape[1]) == k and not out_pieces, (name, a.shape, b.shape)
    else:
        n = b.shape[0] if tb else b.shape[1]
        assert (b.shape[1] if tb else b.shape[0]) == k, (name, a.shape, b.shape)
    tm, tn, tk = _divisor(m, tm, 128), _divisor(n, tn, 128), _divisor(k, tk, 128)
    if b_pieces:
        tn, tk = (tn, shard) if tb else (shard, tk)
    if out_pieces:
        tn = out_pieces
    assert m % tm == 0 and n % tn == 0 and k % tk == 0, (name, m, n, k, tm, tn, tk)
    ni, nj, nk = m // tm, n // tn, k // tk
    dn = (((0 if ta else 1,), (1 if tb else 0,)), ((), ()))
    rides = list(rides)
    nr = len(rides)
    n_in = 2 + (res is not None) + nr
    n_out = 1 + nr

    def body(*refs):
        a_ref, b_ref = refs[0], refs[1]
        r_ref = refs[2] if res is not None else None
        o_ref = refs[n_in]
        scratch = refs[n_in + n_out:]
        sems = scratch[len(scratch) - 3 * nr:]
        i, j, kk = pl.program_id(0), pl.program_id(1), pl.program_id(2)

        def peers(r):
            return _peer_copies(rides[r][0], refs[n_in - nr + r], refs[n_in + 1 + r], *sems[3 * r:3 * r + 3],
                                rows=rides[r][2])

        if nr:
            @pl.when((i == 0) & (j == 0) & (kk == 0))
            def _():
                for r in range(nr):
                    local, remote = peers(r)
                    local.start()
                    for cp in remote:
                        cp.start()

        def finish(r):
            if r_ref is not None:
                r = r + r_ref[...]
            o_ref[...] = r.astype(o_ref.dtype)

        if nk == 1:
            finish(_mm(a_ref[...], b_ref[...], dn))
        else:
            acc_ref = scratch[0]

            @pl.when(kk == 0)
            def _():
                acc_ref[...] = jnp.zeros_like(acc_ref)

            acc_ref[...] += _mm(a_ref[...], b_ref[...], dn)

            @pl.when(kk == nk - 1)
            def _():
                finish(acc_ref[...])

        if nr:
            @pl.when((i == ni - 1) & (j == nj - 1) & (kk == nk - 1))
            def _():
                for r in range(nr):
                    local, remote = peers(r)
                    for cp in remote:
                        cp.wait_recv()
                    for cp in remote:
                        cp.wait_send()
                    local.wait()

    a_spec = pl.BlockSpec((tk, tm), lambda i, j, kk: (kk, i)) if ta else pl.BlockSpec((tm, tk), lambda i, j, kk: (i, kk))
    if b_pieces:
        b_spec = (pl.BlockSpec((None, tn, tk), lambda i, j, kk: (kk, j, 0)) if tb
                  else pl.BlockSpec((None, tk, tn), lambda i, j, kk: (j, kk, 0)))
    else:
        b_spec = pl.BlockSpec((tn, tk), lambda i, j, kk: (j, kk)) if tb else pl.BlockSpec((tk, tn), lambda i, j, kk: (kk, j))
    o_spec = pl.BlockSpec((tm, tn), lambda i, j, kk: (i, j))
    ins, specs = [a, b], [a_spec, b_spec]
    if out_pieces:
        assert res is None and tn == out_pieces, (name, tn, out_pieces)
        out_specs = [pl.BlockSpec((None, tm, tn), lambda i, j, kk: (j, i, 0))]
        out_shape = [jax.ShapeDtypeStruct((nj, m, tn), out_dtype)]
    else:
        out_specs, out_shape = [o_spec], [jax.ShapeDtypeStruct((m, n), out_dtype)]
    scratch_shapes = [pltpu.VMEM((tm, tn), F32)] if nk > 1 else []
    if res is not None:
        ins.append(res)
        specs.append(o_spec)
    for kind, src, rows in rides:
        ins.append(src)
        specs.append(_HBM)
        out_specs.append(_HBM)
        if kind == "gather":
            got = (N_DEV,) + src.shape
        else:
            got = src.shape if rows is None else (N_DEV, rows[1]) + src.shape[2:]
        out_shape.append(jax.ShapeDtypeStruct(got, src.dtype))
        scratch_shapes += _PEER_SEMS
    outs = pl.pallas_call(
        body, name=name, grid=(ni, nj, nk), in_specs=specs, out_specs=out_specs, out_shape=out_shape,
        scratch_shapes=scratch_shapes, compiler_params=_params(("arbitrary", "arbitrary", "arbitrary")),
    )(*ins)
    return (outs[0], list(outs[1:])) if nr else outs[0]


def Row(arr, width=None, off=0, var=True):
    return ("row", arr, arr.shape[1] if width is None else width, off, var)


def Prev(arr, width=None, off=0, var=True):
    return ("prev", arr, arr.shape[1] if width is None else width, off, var)


def Next(arr, width=None, off=0, var=True):
    return ("next", arr, arr.shape[1] if width is None else width, off, var)


def Full(arr, width=None, off=0, var=True):
    return ("full", arr, arr.shape[1] if width is None else width, off, var)


def rows_call(name, fn, rows, tile, ncol, ins, outs=(), accs=()):
    nrow = rows // tile
    assert rows % tile == 0 and tile % SUBLANES == 0, (name, rows, tile)
    last8 = rows // SUBLANES - 1
    per8 = tile // SUBLANES

    def spec(kind, arr, width, off, var):
        col = (lambda j: off + j) if var else (lambda j: off)
        if kind == "row":
            return pl.BlockSpec((tile, width), lambda j, i: (i, col(j)))
        if kind == "prev":
            return pl.BlockSpec((SUBLANES, width), lambda j, i: (jnp.maximum(i * per8 - 1, 0), col(j)))
        if kind == "next":
            return pl.BlockSpec((SUBLANES, width), lambda j, i: (jnp.minimum((i + 1) * per8, last8), col(j)))
        return pl.BlockSpec((arr.shape[0], width), lambda j, i: (0, col(j)))

    n_in, n_out = len(ins), len(outs)

    def body(*refs):
        j, i = pl.program_id(0), pl.program_id(1)
        o_tiles, a_tiles = fn(i, j, *[r[...] for r in refs[:n_in]])
        for r, o in zip(refs[n_in:n_in + n_out], o_tiles, strict=True):
            r[...] = o.astype(r.dtype)
        acc_refs = refs[n_in + n_out:]
        if acc_refs:
            @pl.when(i == 0)
            def _():
                for r in acc_refs:
                    r[...] = jnp.zeros_like(r)
            for r, a in zip(acc_refs, a_tiles, strict=True):
                r[...] += a

    out_shape = [jax.ShapeDtypeStruct((rows, w * ncol), dt) for dt, w in outs]
    out_shape += [jax.ShapeDtypeStruct((r, w * ncol), F32) for r, w in accs]
    out_specs = [pl.BlockSpec((tile, w), lambda j, i: (i, j)) for _, w in outs]
    out_specs += [pl.BlockSpec((r, w), lambda j, i: (0, j)) for r, w in accs]
    res = pl.pallas_call(
        body, name=name, grid=(ncol, nrow), in_specs=[spec(*s) for s in ins], out_specs=out_specs,
        out_shape=out_shape, compiler_params=_params(("arbitrary", "arbitrary")),
    )(*[s[1] for s in ins])
    return res


def _shift_down(tile, prev8, s, first):
    if s == 0:
        return tile
    rolled = pltpu.roll(tile, s, 0)
    pr = jnp.where(first, 0.0, pltpu.roll(prev8, s, 0))
    head = jnp.where(_iota(pr.shape, 0) < s, pr, rolled[:SUBLANES])
    return jnp.concatenate([head, rolled[SUBLANES:]], axis=0)


def _shift_up(tile, next8, s, last):
    if s == 0:
        return tile
    t = tile.shape[0]
    rolled = pltpu.roll(tile, t - s, 0)
    nx = jnp.where(last, 0.0, pltpu.roll(next8, SUBLANES - s, 0))
    tail = jnp.where(_iota(nx.shape, 0) >= SUBLANES - s, nx, rolled[t - SUBLANES:])
    return jnp.concatenate([rolled[:t - SUBLANES], tail], axis=0)


def _row(w, k):
    return jnp.sum(jnp.where(_iota(w.shape, 0) == k, w, 0.0), axis=0, keepdims=True)


def _conv(x, prev8, w, b, first):
    kk = w.shape[0]
    y = b + _row(w, kk - 1) * x
    for k in range(kk - 1):
        y = y + _row(w, k) * _shift_down(x, prev8, kk - 1 - k, first)
    return y


def _conv_wgrad(x, prev8, dy, kk, first):
    out = jnp.zeros((SUBLANES, x.shape[1]), F32)
    for k in range(kk):
        r = jnp.sum(dy * _shift_down(x, prev8, kk - 1 - k, first), axis=0, keepdims=True)
        out = out + jnp.where(_iota(out.shape, 0) == k, r, 0.0)
    return out


def _conv_t(dy, next8, w, last):
    kk = w.shape[0]
    dx = _row(w, kk - 1) * dy
    for k in range(kk - 1):
        dx = dx + _row(w, k) * _shift_up(dy, next8, kk - 1 - k, last)
    return dx


def f_rms(x, w):
    return x * lax.rsqrt(jnp.mean(x * x, axis=-1, keepdims=True) + EPS) * w


def rms_fwd(name, x, w):
    def fn(i, j, xt, wt):
        return [f_rms(xt, wt)], []
    return rows_call(name, fn, x.shape[0], _divisor(x.shape[0], 512, 8), 1, [Row(x), Full(w)],
                     outs=[(BF16, x.shape[1])])[0]


def rms_bwd(name, x, w, dh, dres):
    def fn(i, j, xt, wt, dht, drt):
        _, vjp = jax.vjp(f_rms, xt, wt)
        dx, dw = vjp(dht)
        return [drt + dx], [dw]
    d = x.shape[1]
    return rows_call(name, fn, x.shape[0], _divisor(x.shape[0], 256, 8), 1, [Row(x), Full(w), Row(dh), Row(dres)],
                     outs=[(F32, d)], accs=[(1, d)])


def loss_head(name, x, w, target):
    def fn(i, j, xt, wt, tt):
        def f(xx, ww):
            err = f_rms(xx, ww) - tt
            return 0.5 * jnp.mean(err * err, axis=-1, keepdims=True)
        rows, vjp = jax.vjp(f, xt, wt)
        dx, dw = vjp(jnp.ones_like(rows))
        return [dx], [dw, jnp.broadcast_to(jnp.sum(rows, axis=0, keepdims=True), (1, 128))]
    d = x.shape[1]
    return rows_call(name, fn, x.shape[0], _divisor(x.shape[0], 256, 8), 1, [Row(x), Full(w), Row(target)],
                     outs=[(F32, d)], accs=[(1, d), (1, 128)])


def ffn_act_fwd(name, u0, cw, cb):
    t, two_f = u0.shape
    wc = _divisor(two_f // 2, 512, 128)
    nc = two_f // 2 // wc

    def fn(i, j, ug, ugp, uv, uvp, wg, wv, bg, bv):
        first = i == 0
        g = _conv(ug, ugp, wg, bg, first)
        v = _conv(uv, uvp, wv, bv, first)
        return [jax.nn.silu(g) * v], []
    ins = [Row(u0, wc), Prev(u0, wc), Row(u0, wc, nc), Prev(u0, wc, nc),
           Full(cw, wc), Full(cw, wc, nc), Full(cb, wc), Full(cb, wc, nc)]
    return rows_call(name, fn, t, _divisor(t, 1024, 8), nc, ins, outs=[(BF16, wc)])[0]


def ffn_act_bwd(name, u0, cw, cb, dact):
    t, two_f = u0.shape
    wc = _divisor(two_f // 2, 512, 128)
    nc = two_f // 2 // wc
    kk = cw.shape[0]
    tile = _divisor(t, 512, 8)
    last_i = t // tile - 1

    def conv_t(dy, w):
        rows = dy.shape[0]
        dx = _row(w, kk - 1) * dy[:tile]
        for k in range(kk - 1):
            dx = dx + _row(w, k) * pltpu.roll(dy, rows - (kk - 1 - k), 0)[:tile]
        return dx

    def fn(i, j, ug, ugp, ugn, uv, uvp, uvn, wg, wv, bg, bv, da, dan):
        first = i == 0
        ext = lambda a, b: jnp.concatenate([a, b], axis=0)
        g = _conv(ext(ug, ugn), ugp, wg, bg, first)
        v = _conv(ext(uv, uvn), uvp, wv, bv, first)
        _, vjp = jax.vjp(lambda gg, vv: jax.nn.silu(gg) * vv, g, v)
        dg, dv = vjp(ext(da, jnp.where(i == last_i, 0.0, dan)))
        accs = [_conv_wgrad(ug, ugp, dg[:tile], kk, first), _conv_wgrad(uv, uvp, dv[:tile], kk, first),
                jnp.sum(dg[:tile], axis=0, keepdims=True), jnp.sum(dv[:tile], axis=0, keepdims=True)]
        return [conv_t(dg, wg), conv_t(dv, wv)], accs
    ins = [Row(u0, wc), Prev(u0, wc), Next(u0, wc), Row(u0, wc, nc), Prev(u0, wc, nc), Next(u0, wc, nc),
           Full(cw, wc), Full(cw, wc, nc), Full(cb, wc), Full(cb, wc, nc), Row(dact, wc), Next(dact, wc)]
    return rows_call(name, fn, t, tile, nc, ins, outs=[(BF16, wc), (BF16, wc)],
                     accs=[(SUBLANES, wc), (SUBLANES, wc), (1, wc), (1, wc)])


def conv_t_call(name, dy, cw, col_off, width):
    out_dtype = BF16
    t, c = dy.shape
    nc = c // width
    nrow_tile = _divisor(t, 512, 8)
    last_i = t // nrow_tile - 1

    def fn(i, j, d, dn, w):
        return [_conv_t(d, dn, w, i == last_i)], []
    return rows_call(name, fn, t, nrow_tile, nc, [Row(dy, width), Next(dy, width), Full(cw, width, col_off)],
                     outs=[(out_dtype, width)])[0]


def f_lru_gates(xc, wr, br, wi, bi, lam):
    r = jax.nn.sigmoid(mm_nn(xc, wr) + br)
    gi = jax.nn.sigmoid(mm_nn(xc, wi) + bi)
    log_a = -LRU_C * r * _softplus(-lam)
    a = jnp.exp(log_a)
    u = jnp.sqrt(_neg_expm1(2.0 * log_a)) * (gi * xc)
    return a, u


def lru_pre_fwd(name, proj, cw, cb, wr, br, wi, bi, lam):
    t, w = proj.shape[0], lam.shape[1]

    def fn(i, j, xa, xap, cwt, cbt, wrt, brt, wit, bit, lamt):
        xc = _conv(xa, xap, cwt, cbt, i == 0)
        a, u = f_lru_gates(xc, wrt, brt, wit, bit, lamt)
        return [xc, a, u], []
    ins = [Row(proj, w), Prev(proj, w), Full(cw), Full(cb), Full(wr), Full(br), Full(wi), Full(bi), Full(lam)]
    return rows_call(name, fn, t, _divisor(t, 512, 8), 1, ins, outs=[(F32, w)] * 3)


def lru_pre_bwd(name, proj, cw, cb, wr, br, wi, bi, lam, xc, hseq, lamb):
    t, w = proj.shape[0], lam.shape[1]
    kk = cw.shape[0]

    def fn(i, j, xa, xap, xct, hs, hsp, lb, wrt, brt, wit, bit, lamt):
        first = i == 0
        da = lb * _shift_down(hs, hsp, 1, first)
        _, vjp = jax.vjp(f_lru_gates, xct, wrt.astype(F32), brt, wit.astype(F32), bit, lamt)
        dxc, dwr, dbr, dwi, dbi, dlam = vjp((da, lb))
        accs = [_conv_wgrad(xa, xap, dxc, kk, first), jnp.sum(dxc, axis=0, keepdims=True), dwr, dbr, dwi, dbi, dlam]
        return [dxc], accs
    ins = [Row(proj, w), Prev(proj, w), Row(xc), Row(hseq), Prev(hseq), Row(lamb),
           Full(wr), Full(br), Full(wi), Full(bi), Full(lam)]
    return rows_call(name, fn, t, _divisor(t, 256, 8), 1, ins, outs=[(F32, w)],
                     accs=[(SUBLANES, w), (1, w), (w, w), (1, w), (w, w), (1, w), (1, w)])


def lin_scan(name, a, x, reverse):
    t, c = a.shape
    tile = _divisor(t, 512, 8)
    n = t // tile

    def body(a_ref, x_ref, o_ref, c_ref):
        @pl.when(pl.program_id(0) == 0)
        def _():
            c_ref[...] = jnp.zeros_like(c_ref)

        def step(s, carry):
            r = (tile - 1 - s) if reverse else s
            at, xt = a_ref[pl.ds(r, 1), :], x_ref[pl.ds(r, 1), :]
            o = (xt + carry) if reverse else (at * carry + xt)
            o_ref[pl.ds(r, 1), :] = o
            return (at * o) if reverse else o
        c_ref[...] = lax.fori_loop(0, tile, step, c_ref[...], unroll=8)

    spec = pl.BlockSpec((tile, c), (lambda i: (n - 1 - i, 0)) if reverse else (lambda i: (i, 0)))
    return pl.pallas_call(
        body, name=name, grid=(n,), in_specs=[spec, spec], out_specs=spec,
        out_shape=jax.ShapeDtypeStruct((t, c), F32), scratch_shapes=[pltpu.VMEM((1, c), F32)],
        compiler_params=_params(("arbitrary",)),
    )(a, x)


def _hg_chunk(s, q, fr, v, lb):
    f = lb + (1.0 - lb) * jax.nn.sigmoid(fr)
    k = 1.0 - f
    g = jnp.log(f)
    qs = jax.nn.silu(q) * (STATE ** -0.5)
    cum = dot_hi(_tril(CHUNK), g)
    tot = jnp.sum(g, axis=0, keepdims=True)
    mid = jnp.sum(jnp.where(_iota(g.shape, 0) < CHUNK // 2, g, 0.0), axis=0, keepdims=True)
    scores = mm_nt(qs * jnp.exp(cum - mid), k * jnp.exp(mid - cum))
    scores = jnp.where(_tril(CHUNK) > 0, scores, 0.0)
    o = mm_nn(scores, v) + mm_nn(qs * jnp.exp(cum), s)
    decay = jnp.broadcast_to(jnp.exp(tot), s.shape).T
    s_new = decay * s + mm_tn(k * jnp.exp(tot - cum), v)
    return o, s_new


def _hg_specs(proj, heads, lru_w, hg_w, rows, rev):
    nblk = proj.shape[0] // rows
    blk = (lambda b: nblk - 1 - b) if rev else (lambda b: b)
    base = 2 * lru_w // STATE
    per = hg_w // STATE
    col = [pl.BlockSpec((rows, STATE), functools.partial(lambda h, b, o: (blk(b), o + h), o=base + k * per))
           for k in range(3)]
    return nblk, blk, col


def hg_fwd(name, proj, lbs, lru_w, cb=4):
    t, hg_w = proj.shape[0], lbs.shape[1]
    heads = hg_w // STATE
    cb = min(cb, t // CHUNK)
    rows = cb * CHUNK
    nblk, blk, col = _hg_specs(proj, heads, lru_w, hg_w, rows, False)

    def body(q_ref, f_ref, v_ref, lb_ref, o_ref, s_ref, st):
        @pl.when(pl.program_id(1) == 0)
        def _():
            st[...] = jnp.zeros_like(st)
        s = st[...]
        for c in range(cb):
            sl = slice(c * CHUNK, (c + 1) * CHUNK)
            s_ref[c] = s
            o, s = _hg_chunk(s, q_ref[sl, :], f_ref[sl, :], v_ref[sl, :], lb_ref[...])
            o_ref[sl, :] = o
        st[...] = s

    return pl.pallas_call(
        body, name=name, grid=(heads, nblk),
        in_specs=col + [pl.BlockSpec((1, STATE), lambda h, b: (0, h))],
        out_specs=[pl.BlockSpec((rows, STATE), lambda h, b: (b, h)),
                   pl.BlockSpec((cb, None, STATE, STATE), lambda h, b: (b, h, 0, 0))],
        out_shape=[jax.ShapeDtypeStruct((t, hg_w), F32),
                   jax.ShapeDtypeStruct((t // CHUNK, heads, STATE, STATE), F32)],
        scratch_shapes=[pltpu.VMEM((STATE, STATE), F32)],
        compiler_params=_params(("arbitrary", "arbitrary")),
    )(proj, proj, proj, lbs)


def hg_bwd(name, proj, lbs, states, do, lru_w, cb=4):
    t, hg_w = proj.shape[0], lbs.shape[1]
    heads = hg_w // STATE
    cb = min(cb, t // CHUNK)
    rows = cb * CHUNK
    nblk, blk, col = _hg_specs(proj, heads, lru_w, hg_w, rows, True)

    def body(q_ref, f_ref, v_ref, lb_ref, s_ref, do_ref, dq_ref, df_ref, dv_ref, dlb_ref, dst):
        @pl.when(pl.program_id(1) == 0)
        def _():
            dst[...] = jnp.zeros_like(dst)
            dlb_ref[...] = jnp.zeros_like(dlb_ref)
        ds = dst[...]
        dlb = jnp.zeros((1, STATE), F32)
        for c in reversed(range(cb)):
            sl = slice(c * CHUNK, (c + 1) * CHUNK)
            _, vjp = jax.vjp(_hg_chunk, s_ref[c], q_ref[sl, :], f_ref[sl, :], v_ref[sl, :], lb_ref[...])
            ds, dq, df, dv, dl = vjp((do_ref[sl, :], ds))
            dq_ref[sl, :] = dq.astype(dq_ref.dtype)
            df_ref[sl, :] = df.astype(df_ref.dtype)
            dv_ref[sl, :] = dv.astype(dv_ref.dtype)
            dlb = dlb + dl
        dst[...] = ds
        dlb_ref[...] += dlb

    rspec = pl.BlockSpec((rows, STATE), lambda h, b: (blk(b), h))
    return pl.pallas_call(
        body, name=name, grid=(heads, nblk),
        in_specs=col + [pl.BlockSpec((1, STATE), lambda h, b: (0, h)),
                        pl.BlockSpec((cb, None, STATE, STATE), lambda h, b: (blk(b), h, 0, 0)), rspec],
        out_specs=[rspec, rspec, rspec, pl.BlockSpec((1, STATE), lambda h, b: (0, h))],
        out_shape=[jax.ShapeDtypeStruct((t, hg_w), BF16)] * 3 + [jax.ShapeDtypeStruct((1, hg_w), F32)],
        scratch_shapes=[pltpu.VMEM((STATE, STATE), F32)],
        compiler_params=_params(("arbitrary", "arbitrary")),
    )(proj, proj, proj, lbs, states, do)


def f_even_post(hseq, ga, ob, gb, nw):
    parts = [hseq * jax.nn.gelu(ga)]
    for h in range(ob.shape[1] // STATE):
        o = ob[:, h * STATE:(h + 1) * STATE]
        on = o * lax.rsqrt(jnp.mean(o * o, axis=-1, keepdims=True) + EPS) * nw
        parts.append(on * jax.nn.silu(gb[:, h * STATE:(h + 1) * STATE]))
    return jnp.concatenate(parts, axis=-1)


def _even_post_ins(proj, hseq, ob, nw):
    w, v = hseq.shape[1], ob.shape[1]
    assert w == v
    return [Row(hseq), Row(proj, w, 1), Row(ob), Row(proj, v, (2 * w + 3 * v) // v), Full(nw)]


def even_post_fwd(name, proj, hseq, ob, nw):
    t = proj.shape[0]

    def fn(i, j, hs, ga, o, gb, nwt):
        return [f_even_post(hs, ga, o, gb, nwt)], []
    return rows_call(name, fn, t, _divisor(t, 256, 8), 1, _even_post_ins(proj, hseq, ob, nw),
                     outs=[(BF16, hseq.shape[1] + ob.shape[1])])[0]


def even_post_bwd(name, proj, hseq, ob, nw, dy):
    t, w, v = proj.shape[0], hseq.shape[1], ob.shape[1]

    def fn(i, j, hs, ga, o, gb, nwt, dyt):
        _, vjp = jax.vjp(f_even_post, hs, ga, o, gb, nwt)
        dhs, dga, dob, dgb, dnw = vjp(dyt)
        return [dhs, dga, dob, dgb], [dnw]
    return rows_call(name, fn, t, _divisor(t, 256, 8), 1, _even_post_ins(proj, hseq, ob, nw) + [Row(dy)],
                     outs=[(F32, w), (BF16, w), (F32, v), (BF16, v)], accs=[(1, STATE)])


def f_lbs(hb):
    e = jnp.exp(hb - jnp.max(hb, axis=0, keepdims=True))
    p = e / jnp.sum(e, axis=0, keepdims=True)
    out, run = jnp.zeros_like(p), jnp.zeros_like(p[:1])
    for r in range(hb.shape[0]):
        run = run + _row(p, r)
        out = out + jnp.where(_iota(p.shape, 0) == r, run - _row(p, 0), 0.0)
    return out


def whole_call(name, fn, ins, out_shapes):
    def body(*refs):
        outs = fn(*[r[...] for r in refs[:len(ins)]])
        for r, o in zip(refs[len(ins):], outs, strict=True):
            r[...] = o
    return pl.pallas_call(body, name=name, out_shape=[jax.ShapeDtypeStruct(s, F32) for s in out_shapes])(*ins)


HEADDIM = 64


def f_ssd_prep(dtr, dtb, alog, inner):
    rows = dtr.shape[0]
    dt_all = _softplus(dtr + dtb)
    da_all = dt_all * (-jnp.exp(alog))
    tril = _tril(CHUNK)
    cums = [dot_hi(tril, da_all[c * CHUNK:(c + 1) * CHUNK]) for c in range(rows // CHUNK)]
    cum_all = jnp.concatenate(cums, axis=0) if len(cums) > 1 else cums[0]
    head_of = _iota((STATE, inner), 1) - HEADDIM * _iota((STATE, inner), 0)
    spread = ((head_of >= 0) & (head_of < HEADDIM)).astype(F32)
    even = (_iota((CHUNK, STATE), 1) == 2 * _iota((CHUNK, STATE), 0)).astype(F32)
    odd = (_iota((CHUNK, STATE), 1) == 2 * _iota((CHUNK, STATE), 0) + 1).astype(F32)
    left = _iota((CHUNK, STATE), 1) < HEADDIM
    cumrows = []
    for cm in cums:
        twice = jnp.concatenate([cm, cm], axis=0)
        cumrows.append(jnp.where(left, dot_hi(even, twice, NT), dot_hi(odd, twice, NT)))
    return spread_dot(dt_all, spread), spread_dot(cum_all, spread), tuple(cumrows)


def ssd_prep_fwd(name, dtr, dtb, alog, inner, k=4):
    t = dtr.shape[0]
    k = min(k, t // CHUNK)
    rows = k * CHUNK

    def body(dtr_ref, dtb_ref, alog_ref, dt_ref, cum_ref, cr_ref):
        dt, cum, crs = f_ssd_prep(dtr_ref[...], dtb_ref[...], alog_ref[...], inner)
        dt_ref[...] = dt
        cum_ref[...] = cum
        for c, cr in enumerate(crs):
            cr_ref[c] = cr

    one = pl.BlockSpec((1, STATE), lambda i: (0, 0))
    wide = pl.BlockSpec((rows, inner), lambda i: (i, 0))
    return pl.pallas_call(
        body, name=name, grid=(t // rows,),
        in_specs=[pl.BlockSpec((rows, STATE), lambda i: (i, 0)), one, one],
        out_specs=[wide, wide, pl.BlockSpec((k, CHUNK, STATE), lambda i: (i, 0, 0))],
        out_shape=[jax.ShapeDtypeStruct((t, inner), F32), jax.ShapeDtypeStruct((t, inner), F32),
                   jax.ShapeDtypeStruct((t // CHUNK, CHUNK, STATE), F32)],
        compiler_params=_params(("arbitrary",)),
    )(dtr, dtb, alog)


def ssd_prep_bwd(name, dtr, dtb, alog, ddt, dcum, dcr, inner, k=4):
    t = dtr.shape[0]
    k = min(k, t // CHUNK)
    rows = k * CHUNK

    def body(dtr_ref, dtb_ref, alog_ref, ddt_ref, dcum_ref, dcr_ref, ddtr_ref, ddtb_ref, dalog_ref):
        @pl.when(pl.program_id(0) == 0)
        def _():
            ddtb_ref[...] = jnp.zeros_like(ddtb_ref)
            dalog_ref[...] = jnp.zeros_like(dalog_ref)
        _, vjp = jax.vjp(functools.partial(f_ssd_prep, inner=inner), dtr_ref[...], dtb_ref[...], alog_ref[...])
        ddtr, ddtb, dalog = vjp((ddt_ref[...], dcum_ref[...], tuple(dcr_ref[c] for c in range(k))))
        ddtr_ref[...] = ddtr
        ddtb_ref[...] += ddtb
        dalog_ref[...] += dalog

    one = pl.BlockSpec((1, STATE), lambda i: (0, 0))
    wide = pl.BlockSpec((rows, inner), lambda i: (i, 0))
    tall = pl.BlockSpec((rows, STATE), lambda i: (i, 0))
    return pl.pallas_call(
        body, name=name, grid=(t // rows,),
        in_specs=[tall, one, one, wide, wide, pl.BlockSpec((k, CHUNK, STATE), lambda i: (i, 0, 0))],
        out_specs=[tall, one, one],
        out_shape=[jax.ShapeDtypeStruct((t, STATE), F32), jax.ShapeDtypeStruct((1, STATE), F32),
                   jax.ShapeDtypeStruct((1, STATE), F32)],
        compiler_params=_params(("arbitrary",)),
    )(dtr, dtb, alog, ddt, dcum, dcr)


def _ssd_group(states, x, bm, cm, dt, cum, cumrs, dsk):
    half = _iota((CHUNK, STATE), 1) >= HEADDIM
    pos = _iota((CHUNK, STATE), 1) - jnp.where(half, HEADDIM, 0)
    row = _iota((CHUNK, STATE), 0)
    causal = row >= pos
    cb2 = mm_nt(cm, jnp.concatenate([bm, bm], axis=0))
    ys, new = [], []
    for pp, (s, cumr) in enumerate(zip(states, cumrs, strict=True)):
        lanes = slice(pp * STATE, (pp + 1) * STATE)
        cu, xdt = cum[:, lanes], x[:, lanes] * dt[:, lanes]
        tot = jnp.sum(jnp.where(row == CHUNK - 1, cu, 0.0), axis=0, keepdims=True)
        m = jnp.where(causal, cb2 * jnp.exp(jnp.where(causal, cu - cumr, 0.0)), 0.0)
        x2 = jnp.concatenate([jnp.where(half, 0.0, xdt), jnp.where(half, xdt, 0.0)], axis=0)
        ys.append(mm_nn(m, x2) + mm_nt(cm, s) * jnp.exp(cu) + x[:, lanes] * dsk[:, lanes])
        decay = jnp.broadcast_to(jnp.exp(tot), s.shape).T
        new.append(decay * s + mm_tn(xdt * jnp.exp(tot - cu), bm))
    return jnp.concatenate(ys, axis=1), tuple(new)


SSD_CHUNKS_PER_STEP = 4


def _ssd_specs(act, inner, groups, rev):
    t = act.shape[0]
    cb = min(SSD_CHUNKS_PER_STEP, t // CHUNK)
    rows = cb * CHUNK
    nblk = t // rows
    ch = (lambda c: nblk - 1 - c) if rev else (lambda c: c)
    gw = inner // groups
    wide = pl.BlockSpec((rows, gw), lambda c, g: (ch(c), g))
    specs = [wide,
             pl.BlockSpec((rows, STATE), lambda c, g: (ch(c), inner // STATE + g)),
             pl.BlockSpec((rows, STATE), lambda c, g: (ch(c), inner // STATE + groups + g)),
             wide, wide,
             pl.BlockSpec((cb, CHUNK, STATE), lambda c, g: (ch(c), 0, 0)),
             pl.BlockSpec((1, gw), lambda c, g: (0, g))]
    return cb, nblk, ch, gw, specs


def ssd_fwd(name, act, dt, cum, cumrow, dexp, inner, groups):
    t = act.shape[0]
    cb, nblk, ch, gw, specs = _ssd_specs(act, inner, groups, False)
    pairs = gw // (2 * HEADDIM)

    def body(x_ref, b_ref, c_ref, dt_ref, cum_ref, cr_ref, dsk_ref, y_ref, sv_ref, st):
        c, g = pl.program_id(0), pl.program_id(1)

        @pl.when(c == 0)
        def _():
            for pp in range(pairs):
                st[g * pairs + pp] = jnp.zeros((STATE, STATE), F32)

        states = tuple(st[g * pairs + pp] for pp in range(pairs))
        for k in range(cb):
            rs = slice(k * CHUNK, (k + 1) * CHUNK)
            cumrs = tuple(cr_ref[k, pl.ds(g * pairs + pp, 1), :] for pp in range(pairs))
            for pp in range(pairs):
                sv_ref[k, pp] = states[pp]
            y, states = _ssd_group(states, x_ref[rs, :], b_ref[rs, :], c_ref[rs, :], dt_ref[rs, :], cum_ref[rs, :],
                                   cumrs, dsk_ref[...])
            y_ref[rs, :] = y
        for pp in range(pairs):
            st[g * pairs + pp] = states[pp]

    return pl.pallas_call(
        body, name=name, grid=(nblk, groups), in_specs=specs,
        out_specs=[pl.BlockSpec((cb * CHUNK, gw), lambda c, g: (c, g)),
                   pl.BlockSpec((cb, pairs, STATE, STATE), lambda c, g: (c, g, 0, 0))],
        out_shape=[jax.ShapeDtypeStruct((t, inner), F32),
                   jax.ShapeDtypeStruct((t // CHUNK, groups * pairs, STATE, STATE), F32)],
        scratch_shapes=[pltpu.VMEM((groups * pairs, STATE, STATE), F32)],
        compiler_params=_params(("arbitrary", "arbitrary")),
    )(act, act, act, dt, cum, cumrow, dexp)


def ssd_bwd(name, act, dt, cum, cumrow, dexp, states, dy, inner, groups):
    t = act.shape[0]
    cb, nblk, ch, gw, specs = _ssd_specs(act, inner, groups, True)
    pairs = gw // (2 * HEADDIM)

    def body(x_ref, b_ref, c_ref, dt_ref, cum_ref, cr_ref, dsk_ref, sv_ref, dy_ref,
             dx_ref, db_ref, dc_ref, ddt_ref, dcum_ref, dcr_ref, ddsk_ref, dst):
        c, g = pl.program_id(0), pl.program_id(1)

        @pl.when((c == 0) & (g == 0))
        def _():
            ddsk_ref[...] = jnp.zeros_like(ddsk_ref)

        @pl.when(c == 0)
        def _():
            for pp in range(pairs):
                dst[g * pairs + pp] = jnp.zeros((STATE, STATE), F32)

        @pl.when(g == 0)
        def _():
            dcr_ref[...] = jnp.zeros_like(dcr_ref)

        ds = tuple(dst[g * pairs + pp] for pp in range(pairs))
        ddsk = jnp.zeros((1, gw), F32)
        for k in reversed(range(cb)):
            rs = slice(k * CHUNK, (k + 1) * CHUNK)
            cumrs = tuple(cr_ref[k, pl.ds(g * pairs + pp, 1), :] for pp in range(pairs))
            _, vjp = jax.vjp(_ssd_group, tuple(sv_ref[k, pp] for pp in range(pairs)), x_ref[rs, :], b_ref[rs, :],
                             c_ref[rs, :], dt_ref[rs, :], cum_ref[rs, :], cumrs, dsk_ref[...])
            ds, dx, db, dc, ddt, dcum, dcrs, ddskk = vjp((dy_ref[rs, :], ds))
            for pp in range(pairs):
                dcr_ref[k, pl.ds(g * pairs + pp, 1), :] = dcrs[pp]
            dx_ref[rs, :] = dx
            db_ref[rs, :] = db
            dc_ref[rs, :] = dc
            ddt_ref[rs, :] = ddt
            dcum_ref[rs, :] = dcum
            ddsk = ddsk + ddskk
        for pp in range(pairs):
            dst[g * pairs + pp] = ds[pp]
        col = pl.ds(pl.multiple_of(g * gw, STATE), gw)
        ddsk_ref[:, col] = ddsk_ref[:, col] + ddsk

    wide = pl.BlockSpec((cb * CHUNK, gw), lambda c, g: (ch(c), g))
    grp = pl.BlockSpec((cb * CHUNK, STATE), lambda c, g: (ch(c), g))
    return pl.pallas_call(
        body, name=name, grid=(nblk, groups),
        in_specs=specs + [pl.BlockSpec((cb, pairs, STATE, STATE), lambda c, g: (ch(c), g, 0, 0)), wide],
        out_specs=[wide, grp, grp, wide, wide, pl.BlockSpec((cb, CHUNK, STATE), lambda c, g: (ch(c), 0, 0)),
                   pl.BlockSpec((1, inner), lambda c, g: (0, 0))],
        out_shape=[jax.ShapeDtypeStruct((t, inner), F32), jax.ShapeDtypeStruct((t, groups * STATE), F32),
                   jax.ShapeDtypeStruct((t, groups * STATE), F32), jax.ShapeDtypeStruct((t, inner), F32),
                   jax.ShapeDtypeStruct((t, inner), F32), jax.ShapeDtypeStruct((t // CHUNK, CHUNK, STATE), F32),
                   jax.ShapeDtypeStruct((1, inner), F32)],
        scratch_shapes=[pltpu.VMEM((groups * pairs, STATE, STATE), F32)],
        compiler_params=_params(("arbitrary", "arbitrary")),
    )(act, act, act, dt, cum, cumrow, dexp, states, dy)


def _ssd_conv_width(inner, cdim):
    return _divisor(math.gcd(inner, cdim), 2048, 128)


def ssd_conv_fwd(name, zx, cw, cb, inner):
    t, cdim = zx.shape[0], cw.shape[1]
    wc = _ssd_conv_width(inner, cdim)

    def fn(i, j, xt, xp, w, b):
        return [jax.nn.silu(_conv(xt, xp, w, b, i == 0))], []
    ins = [Row(zx, wc, inner // wc), Prev(zx, wc, inner // wc), Full(cw, wc), Full(cb, wc)]
    return rows_call(name, fn, t, _divisor(t, 256, 8), cdim // wc, ins, outs=[(F32, wc)])[0]


def ssd_conv_bwd(name, zx, cw, cb, dact, inner):
    t, cdim = zx.shape[0], cw.shape[1]
    wc = _ssd_conv_width(inner, cdim)
    kk = cw.shape[0]

    def fn(i, j, xt, xp, w, b, da):
        first = i == 0
        pre = _conv(xt, xp, w, b, first)
        _, vjp = jax.vjp(jax.nn.silu, pre)
        dpre, = vjp(da)
        return [dpre], [_conv_wgrad(xt, xp, dpre, kk, first), jnp.sum(dpre, axis=0, keepdims=True)]
    ins = [Row(zx, wc, inner // wc), Prev(zx, wc, inner // wc), Full(cw, wc), Full(cb, wc), Row(dact, wc)]
    return rows_call(name, fn, t, _divisor(t, 256, 8), cdim // wc, ins, outs=[(F32, wc)],
                     accs=[(SUBLANES, wc), (1, wc)])


def f_ssd_post(y, z, nw):
    yz = y * jax.nn.silu(z)
    return yz * lax.rsqrt(jnp.mean(yz * yz, axis=-1, keepdims=True) + EPS) * nw


def ssd_post_fwd(name, y, zx, nw, groups):
    t, inner = y.shape
    gw = inner // groups

    def fn(i, j, yt, zt, nwt):
        return [f_ssd_post(yt, zt, nwt)], []
    return rows_call(name, fn, t, _divisor(t, 1024, 8), groups, [Row(y, gw), Row(zx, gw), Full(nw, gw)],
                     outs=[(BF16, gw)])[0]


def ssd_post_bwd(name, y, zx, nw, dyn, groups):
    t, inner = y.shape
    gw = inner // groups

    def fn(i, j, yt, zt, nwt, dt):
        _, vjp = jax.vjp(f_ssd_post, yt, zt, nwt)
        dy, dz, dnw = vjp(dt)
        return [dy, dz], [dnw]
    return rows_call(name, fn, t, _divisor(t, 512, 8), groups, [Row(y, gw), Row(zx, gw), Full(nw, gw), Row(dyn, gw)],
                     outs=[(F32, gw), (BF16, gw)], accs=[(1, gw)])


ADAMW_TILE = 256 * 1024


def adamw(name, slots, w, m, v, layer=0, row0=0, prev=None):
    nl, _, c = w.shape
    n = slots.shape[1]
    tr = _divisor(math.gcd(row0, n), max(16, ADAMW_TILE // c // 16 * 16), 16)
    first = row0 // tr

    def body(s_ref, w_ref, m_ref, v_ref, *rest):
        g_ref, d_ref, mo_ref, vo_ref = rest[-4:]
        g = s_ref[0].astype(F32)
        for k in range(1, N_DEV):
            g = g + s_ref[k].astype(F32)
        mn = ADAM_B1 * m_ref[...] + (1.0 - ADAM_B1) * g
        vn = ADAM_B2 * v_ref[...] + (1.0 - ADAM_B2) * (g * g)
        m_hat = mn / (1.0 - ADAM_B1 ** ADAM_STEP)
        v_hat = vn / (1.0 - ADAM_B2 ** ADAM_STEP)
        g_ref[...] = g
        d_ref[...] = -ADAM_LR * (m_hat / (jnp.sqrt(v_hat) + ADAM_EPS) + ADAM_WD * w_ref[...])
        mo_ref[...] = mn
        vo_ref[...] = vn

    spec = pl.BlockSpec((None, tr, c), lambda i: (layer, first + i, 0))
    prev = list(prev) if prev is not None else []
    return pl.pallas_call(
        body, name=name, grid=(n // tr,),
        in_specs=[pl.BlockSpec((N_DEV, tr, c), lambda i: (0, i, 0)), spec, spec, spec] + [_HBM] * len(prev),
        out_specs=[spec] * 4, out_shape=[jax.ShapeDtypeStruct(w.shape, F32)] * 4,
        input_output_aliases={4 + k: k for k in range(len(prev))},
        compiler_params=_params(("parallel",)),
    )(slots, w, m, v, *prev)


def all_gather(name, shard):
    def body(x_ref, out_ref, send_sems, recv_sems, local_sem):
        x, y, c = lax.axis_index("x"), lax.axis_index("y"), lax.axis_index("c")
        me, sibling = (x, y, c), (x, y, 1 - c)
        chips = [(1 - x, y), (x, 1 - y), (1 - x, 1 - y)]

        def slab(px, py, pc):
            return out_ref.at[4 * px + 2 * py + pc]

        def copy(k, block, to, src=None):
            return pltpu.make_async_remote_copy(
                src_ref=slab(*block) if src is None else src, dst_ref=slab(*block),
                send_sem=send_sems.at[k], recv_sem=recv_sems.at[k], device_id=to, device_id_type=_MESH)

        mine = pltpu.make_async_copy(x_ref, slab(*me), local_sem)
        mine.start()
        first = [copy(0, me, sibling, src=x_ref)]
        first += [copy(1 + j, me, (*chip, c), src=x_ref) for j, chip in enumerate(chips)]
        for cp in first:
            cp.start()
        passed = [copy(4 + j, (*chip, c), sibling) for j, chip in enumerate(chips)]
        for j, chip in enumerate(chips):
            copy(1 + j, (*chip, c), me).wait_recv()
            passed[j].start()
        copy(0, sibling, me).wait_recv()
        for j, chip in enumerate(chips):
            copy(4 + j, (*chip, 1 - c), me).wait_recv()
        for cp in first + passed:
            cp.wait_send()
        mine.wait()

    return pl.pallas_call(
        body, name=name, out_shape=jax.ShapeDtypeStruct((N_DEV,) + shard.shape, shard.dtype),
        in_specs=[_HBM], out_specs=_HBM,
        scratch_shapes=[pltpu.SemaphoreType.DMA((7,)), pltpu.SemaphoreType.DMA((7,)), pltpu.SemaphoreType.DMA],
    )(shard)


def exchange(name, pieces):
    def body(p_ref, out_ref, send_sems, recv_sems, local_sem):
        local, remote = _peer_copies("exchange", p_ref, out_ref, send_sems, recv_sems, local_sem)
        local.start()
        for cp in remote:
            cp.start()
        for cp in remote:
            cp.wait_recv()
        for cp in remote:
            cp.wait_send()
        local.wait()

    return pl.pallas_call(
        body, name=name, out_shape=jax.ShapeDtypeStruct(pieces.shape, pieces.dtype),
        in_specs=[_HBM], out_specs=_HBM, scratch_shapes=_PEER_SEMS,
    )(pieces)


WEIGHTS = ['norm_mix_w', 'norm_ffn_w', 'norm_final_w', 'ev_w_in', 'lru_conv_w', 'lru_conv_b', 'lru_w_r', 'lru_b_r',
           'lru_w_i', 'lru_b_i', 'lru_lambda', 'hg_lower_bounds', 'hg_norm_w', 'ev_w_out', 'ssd_w_in', 'ssd_conv_w',
           'ssd_conv_b', 'ssd_dt_bias', 'ssd_a_log', 'ssd_d', 'ssd_norm_w', 'ssd_w_out', 'ffn_w_up', 'ffn_conv_w',
           'ffn_conv_b', 'ffn_w_down']
COL_SHARDED = ['ev_w_in', 'ssd_w_in', 'ffn_w_up']
ROW_SHARDED = ['ev_w_out', 'ssd_w_out', 'ffn_w_down']
BIG = ['ev_w_in', 'ev_w_out', 'ssd_w_out', 'ffn_w_up', 'ffn_w_down', 'ssd_w_in']
SMALL_SHARDED = ['lru_conv_w', 'ssd_conv_w', 'ssd_conv_b', 'ssd_norm_w', 'ffn_conv_w']
SMALL = [n for n in WEIGHTS if n not in BIG]
SEG = 16 * 128


def _pad_to(flat, mult):
    extra = (-flat.shape[-1]) % mult
    if extra == 0:
        return flat
    return jnp.pad(flat, [(0, 0)] * (flat.ndim - 1) + [(0, extra)])


def _pack(segments):
    offs, parts, at = [], [], 0
    for s in segments:
        s = _pad_to(s, SEG)
        offs.append(at)
        at += s.shape[-1]
        parts.append(s)
    buf = jnp.concatenate(parts, axis=-1)
    return buf.reshape(buf.shape[:-1] + (at // 128, 128)), offs


def _unshard_last(g):
    g = jnp.moveaxis(g, 0, -2)
    return g.reshape(g.shape[:-2] + (N_DEV * g.shape[-1],))


def _block_diag(w):
    nb, b, _ = w.shape
    return (w[:, :, None, :] * jnp.eye(nb, dtype=w.dtype)[:, None, :, None]).reshape(nb * b, nb * b)


def _diag_blocks(dense, nb):
    b = dense.shape[0] // nb
    d4 = dense.reshape(nb, b, nb, b)
    return jnp.stack([d4[h, :, h, :] for h in range(nb)])


def kernel(x, norm_mix_w, norm_ffn_w, norm_final_w, ev_w_in, lru_conv_w, lru_conv_b, lru_w_r, lru_b_r, lru_w_i, lru_b_i, lru_lambda, hg_lower_bounds, hg_norm_w, ev_w_out, ssd_w_in, ssd_conv_w, ssd_conv_b, ssd_dt_bias, ssd_a_log, ssd_d, ssd_norm_w, ssd_w_out, ffn_w_up, ffn_conv_w, ffn_conv_b, ffn_w_down, loss_target, m_norm_mix_w, m_norm_ffn_w, m_norm_final_w, m_ev_w_in, m_lru_conv_w, m_lru_conv_b, m_lru_w_r, m_lru_b_r, m_lru_w_i, m_lru_b_i, m_lru_lambda, m_hg_lower_bounds, m_hg_norm_w, m_ev_w_out, m_ssd_w_in, m_ssd_conv_w, m_ssd_conv_b, m_ssd_dt_bias, m_ssd_a_log, m_ssd_d, m_ssd_norm_w, m_ssd_w_out, m_ffn_w_up, m_ffn_conv_w, m_ffn_conv_b, m_ffn_w_down, v_norm_mix_w, v_norm_ffn_w, v_norm_final_w, v_ev_w_in, v_lru_conv_w, v_lru_conv_b, v_lru_w_r, v_lru_b_r, v_lru_w_i, v_lru_b_i, v_lru_lambda, v_hg_lower_bounds, v_hg_norm_w, v_ev_w_out, v_ssd_w_in, v_ssd_conv_w, v_ssd_conv_b, v_ssd_dt_bias, v_ssd_a_log, v_ssd_d, v_ssd_norm_w, v_ssd_w_out, v_ffn_w_up, v_ffn_conv_w, v_ffn_conv_b, v_ffn_w_down):
    given = dict(locals())
    wts = {n: given[n] for n in WEIGHTS}
    mom1 = {n: given["m_" + n] for n in WEIGHTS}
    mom2 = {n: given["v_" + n] for n in WEIGHTS}
    me = 4 * lax.axis_index("x") + 2 * lax.axis_index("y") + lax.axis_index("c")

    depth, d = norm_mix_w.shape
    t = x.shape[1]
    x0 = x.reshape(t, d)
    target = loss_target.reshape(t, d)
    n_even, lru_w = lru_lambda.shape
    hg_w = hg_lower_bounds.shape[1]
    n_odd, heads = ssd_dt_bias.shape
    inner = N_DEV * ssd_norm_w.shape[1]
    cdim = N_DEV * ssd_conv_b.shape[1]
    groups = (cdim - inner) // (2 * STATE)
    assert inner == heads * HEADDIM and heads <= STATE and (inner // groups) % (2 * HEADDIM) == 0

    full = {}
    shard_buf, offs = _pack([wts[n].reshape(-1) for n in SMALL_SHARDED])
    gathered = all_gather("ag_small", shard_buf).reshape(N_DEV, -1)
    for n, off in zip(SMALL_SHARDED, offs, strict=True):
        full[n] = _unshard_last(gathered[:, off:off + wts[n].size].reshape((N_DEV,) + wts[n].shape))
    shard16 = {n: wts[n].astype(BF16) for n in BIG}
    mixer = lambda l: (('ev_w_in', l // 2), ('ev_w_out', l // 2)) if l % 2 == 0 else (('ssd_w_in', l // 2), ('ssd_w_out', l // 2))
    weight = {}

    as_pieces = {n: n in COL_SHARDED and wts[n].shape[2] % 128 == 0 for n in BIG}

    def arrived(key, g8):
        if as_pieces[key[0]]:
            weight[key] = g8
        elif key[0] in COL_SHARDED:
            weight[key] = jnp.transpose(g8, (1, 0, 2)).reshape(g8.shape[1], -1)
        else:
            weight[key] = g8.reshape(-1, g8.shape[2])

    def gather_ride(key):
        return ("gather", shard16[key[0]][key[1]], None)

    for key in mixer(0):
        arrived(key, all_gather(f"ag_{key[0]}", shard16[key[0]][key[1]]))
    w_zx = lambda o: weight['ssd_w_in', o][:, :inner + cdim]
    w_dt = lambda o: jnp.pad(weight['ssd_w_in', o][:, inner + cdim:], ((0, 0), (0, STATE - heads)))
    pad_h = lambda a: jnp.pad(a.reshape(1, heads), ((0, 0), (0, STATE - heads)))
    row = lambda a: a.reshape(1, -1)

    lbs = whole_call("lbs_fwd", lambda hb: [f_lbs(hb)], [hg_lower_bounds], [hg_lower_bounds.shape])[0]

    def mm_carry(name, a, b, key, **kw):
        if key is None or key in weight:
            return matmul(name, a, b, **kw)
        out, (g8,) = matmul(name, a, b, rides=[gather_ride(key)], **kw)
        arrived(key, g8)
        return out

    saved = []
    xcur = x0
    for l in range(depth):
        sv = {'x0': xcur}
        k_in, k_out = mixer(l)
        nxt_in, nxt_out = mixer(l + 1) if l + 1 < depth else (None, None)
        h = rms_fwd(f"rms_mix_fwd", xcur, row(norm_mix_w[l]))
        sv['h'] = h
        if l % 2 == 0:
            e = l // 2
            wr = _block_diag(lru_w_r[e]).astype(BF16)
            wi = _block_diag(lru_w_i[e]).astype(BF16)
            lru_p = (full['lru_conv_w'][e], row(lru_conv_b[e]), wr, row(lru_b_r[e]), wi, row(lru_b_i[e]), row(lru_lambda[e]))
            proj = mm_carry("ev_in", h, weight[k_in], ('ffn_w_up', l), b_pieces=as_pieces[k_in[0]])
            xc, a, u = lru_pre_fwd(f"lru_pre_fwd", proj, *lru_p)
            hseq = lin_scan(f"lru_scan_fwd", a, u, False)
            ob, states = hg_fwd(f"hg_fwd", proj, row(lbs[e]), lru_w)
            y = even_post_fwd(f"even_post_fwd", proj, hseq, ob, row(hg_norm_w[e]))
            xmid = mm_carry("ev_out", y, weight[k_out], ('ffn_w_down', l), res=xcur)
            sv.update(proj=proj, xc=xc, a=a, hseq=hseq, ob=ob, states=states, y=y, lru_p=lru_p)
        else:
            o = l // 2
            ssd_p = (pad_h(ssd_dt_bias[o]), pad_h(ssd_a_log[o]), row(jnp.repeat(ssd_d[o], HEADDIM)))
            wzx, wdt = w_zx(o), w_dt(o)
            zx = mm_carry("ssd_in", h, wzx, ('ffn_w_up', l))
            dtr = matmul("ssd_dt", h, wdt)
            sv.update(wzx=wzx, wdt=wdt)
            act = ssd_conv_fwd(f"ssd_conv_fwd", zx, full['ssd_conv_w'][o], row(full['ssd_conv_b'][o]), inner)
            prep = ssd_prep_fwd("ssd_prep_fwd", dtr, ssd_p[0], ssd_p[1], inner)
            ys, states = ssd_fwd(f"ssd_fwd", act, *prep, ssd_p[2], inner, groups)
            sv['prep'] = prep
            yn = ssd_post_fwd(f"ssd_post_fwd", ys, zx, row(full['ssd_norm_w'][o]), groups)
            xmid = mm_carry("ssd_out", yn, weight[k_out], ('ffn_w_down', l), res=xcur)
            sv.update(zx=zx, dtr=dtr, act=act, ys=ys, states=states, yn=yn, ssd_p=ssd_p)
        h2 = rms_fwd(f"rms_ffn_fwd", xmid, row(norm_ffn_w[l]))
        u0 = mm_carry("ffn_up", h2, weight['ffn_w_up', l], nxt_in, b_pieces=as_pieces['ffn_w_up'])
        actf = ffn_act_fwd(f"ffn_act_fwd", u0, full['ffn_conv_w'][l], row(ffn_conv_b[l]))
        xcur = mm_carry("ffn_down", actf, weight['ffn_w_down', l], nxt_out, res=xmid)
        sv.update(x1=xmid, h2=h2, u0=u0, actf=actf)
        saved.append(sv)

    gcur, d_nfw, loss_row = loss_head("loss_head", xcur, row(norm_final_w), target)
    loss = lax.psum(loss_row[0, 0], ("x", "y", "c"))

    gl = {n: [None] * wts[n].shape[0] for n in SMALL if n != 'norm_final_w'}
    d_lbs = [None] * n_even
    slots = {}
    dw_opts = dict(ta=True, out_dtype=BF16, tm=2048, tk=1024)

    def pieces(key, dw):
        if as_pieces[key[0]]:
            return dw
        if key[0] in COL_SHARDED:
            return jnp.transpose(dw.reshape(dw.shape[0], N_DEV, -1), (1, 0, 2))
        return dw.reshape(N_DEV, -1, dw.shape[1])

    def halves(key, p):
        r = p.shape[1] // 2
        return ((*key, 0), p, (0, r)), ((*key, 1), p, (r, p.shape[1] - r))

    def whole(key, p):
        return ((*key, 0), p, None)

    def mm_send(name, a, b, sends, **kw):
        if not sends:
            return matmul(name, a, b, **kw)
        out, got = matmul(name, a, b, rides=[("exchange", p, rows) for _, p, rows in sends], **kw)
        for (key, _, rows), g8 in zip(sends, got, strict=True):
            slots[key] = (g8, 0 if rows is None else rows[0])
        return out

    carried = {}
    for l in reversed(range(depth)):
        sv = saved[l]
        k_in, k_out = mixer(l)
        fcw = full['ffn_conv_w'][l]
        dact = mm_send("ffn_down_dx", gcur, weight['ffn_w_down', l], carried.get('down_dx', ()), tb=True)
        dw_down = mm_send("ffn_down_dw", sv['actf'], gcur, carried.get('down_dw', ()), **dw_opts)
        dug, duv, dwg, dwv, dbg, dbv = ffn_act_bwd(f"ffn_act_bwd", sv['u0'], fcw, row(ffn_conv_b[l]), dact)
        kf = fcw.shape[0]
        gl['ffn_conv_w'][l] = jnp.concatenate([dwg[:kf], dwv[:kf]], axis=1)
        gl['ffn_conv_b'][l] = jnp.concatenate([dbg, dbv], axis=1)[0]
        down_send = [whole(('ffn_w_down', l), pieces(('ffn_w_down', l), dw_down))]
        if as_pieces['ffn_w_up']:
            w_up, shard = weight['ffn_w_up', l], wts['ffn_w_up'].shape[2]
            dh2 = mm_send("ffn_up_dx", dug, w_up[:N_DEV // 2], down_send, tb=True, b_pieces=True)
            dh2 = matmul("ffn_up_dx", duv, w_up[N_DEV // 2:], tb=True, b_pieces=True, res=dh2)
            dw_up = jnp.concatenate([
                mm_send("ffn_up_dw", sv['h2'], dug, carried.get('up_dw', ()), out_pieces=shard, **dw_opts),
                matmul("ffn_up_dw", sv['h2'], duv, out_pieces=shard, **dw_opts)], axis=0)
        else:
            du0 = jnp.concatenate([dug, duv], axis=1)
            dh2 = mm_send("ffn_up_dx", du0, weight['ffn_w_up', l], down_send, tb=True)
            dw_up = mm_send("ffn_up_dw", sv['h2'], du0, carried.get('up_dw', ()), **dw_opts)
        up_a, up_b = halves(('ffn_w_up', l), pieces(('ffn_w_up', l), dw_up))
        in_dx_sends = [up_a]
        gmid, dnw = rms_bwd(f"rms_ffn_bwd", sv['x1'], row(norm_ffn_w[l]), dh2, gcur)
        gl['norm_ffn_w'][l] = dnw[0]
        if l % 2 == 0:
            e = l // 2
            proj, lru_p = sv['proj'], sv['lru_p']
            dy = matmul("ev_out_dx", gmid, weight[k_out], tb=True)
            dw_out = matmul("ev_out_dw", sv['y'], gmid, **dw_opts)
            dhs, dga, dob, dgb, dhn = even_post_bwd(f"even_post_bwd", proj, sv['hseq'], sv['ob'], row(hg_norm_w[e]), dy)
            gl['hg_norm_w'][e] = dhn[0]
            dq, df, di, dlb = hg_bwd(f"hg_bwd", proj, row(lbs[e]), sv['states'], dob, lru_w)
            d_lbs[e] = dlb
            lamb = lin_scan(f"lru_scan_bwd", sv['a'], dhs, True)
            dxc, dcw, dcb, dwr, dbr, dwi, dbi, dlam = lru_pre_bwd(f"lru_pre_bwd", proj, *lru_p, sv['xc'], sv['hseq'], lamb)
            nb = lru_w_r.shape[1]
            gl['lru_conv_w'][e], gl['lru_conv_b'][e] = dcw[:lru_p[0].shape[0]], dcb[0]
            gl['lru_w_r'][e], gl['lru_b_r'][e] = _diag_blocks(dwr, nb), dbr[0]
            gl['lru_w_i'][e], gl['lru_b_i'][e] = _diag_blocks(dwi, nb), dbi[0]
            gl['lru_lambda'][e] = dlam[0]
            dxa = conv_t_call(f"lru_convt", dxc, lru_p[0], 0, lru_w)
            dproj = jnp.concatenate([dxa, dga, dq, df, di, dgb], axis=1)
            in_dw_sends = [up_b] + ([whole(k_out, pieces(k_out, dw_out))] if l == 0 else [])
            dh = mm_send("ev_in_dx", dproj, weight[k_in], in_dx_sends, tb=True, b_pieces=as_pieces[k_in[0]])
            dw_in = mm_send("ev_in_dw", sv['h'], dproj, in_dw_sends,
                            out_pieces=wts[k_in[0]].shape[2] if as_pieces[k_in[0]] else 0, **dw_opts)
        else:
            o = l // 2
            zx, scw = sv['zx'], full['ssd_conv_w'][o]
            dyn = matmul("ssd_out_dx", gmid, weight[k_out], tb=True)
            dw_out = matmul("ssd_out_dw", sv['yn'], gmid, **dw_opts)
            dys, dz, dnw = ssd_post_bwd(f"ssd_post_bwd", sv['ys'], zx, row(full['ssd_norm_w'][o]), dyn, groups)
            gl['ssd_norm_w'][o] = dnw[0]
            dxs, dbm, dcm, ddt, dcum, dcr, ddexp = ssd_bwd(f"ssd_bwd", sv['act'], *sv['prep'], sv['ssd_p'][2],
                                                          sv['states'], dys, inner, groups)
            ddtr, ddtb, dalog = ssd_prep_bwd("ssd_prep_bwd", sv['dtr'], sv['ssd_p'][0], sv['ssd_p'][1], ddt, dcum, dcr, inner)
            gl['ssd_dt_bias'][o], gl['ssd_a_log'][o] = ddtb[0, :heads], dalog[0, :heads]
            gl['ssd_d'][o] = jnp.sum(ddexp.reshape(heads, HEADDIM), axis=1)
            dact = jnp.concatenate([dxs, dbm, dcm], axis=1)
            dpre, dcw, dcb = ssd_conv_bwd(f"ssd_conv_bwd", zx, scw, row(full['ssd_conv_b'][o]), dact, inner)
            gl['ssd_conv_w'][o], gl['ssd_conv_b'][o] = dcw[:scw.shape[0]], dcb[0]
            dxbc = conv_t_call(f"ssd_convt", dpre, scw, 0, _ssd_conv_width(inner, cdim))
            dzx = jnp.concatenate([dz, dxbc], axis=1)
            in_dw_sends = [up_b] + ([whole(k_out, pieces(k_out, dw_out))] if l == 0 else [])
            dh = mm_send("ssd_in_dx", dzx, sv['wzx'], in_dx_sends, tb=True)
            dh = matmul("ssd_dt_dx", ddtr, sv['wdt'], tb=True, res=dh)
            dwzx = mm_send("ssd_in_dw", sv['h'], dzx, in_dw_sends, **dw_opts)
            dwdt = matmul("ssd_dt_dw", sv['h'], ddtr, **dw_opts)
            dw_in = jnp.concatenate([dwzx, dwdt[:, :heads]], axis=1)
        if l > 0:
            in_a, in_b = halves(k_in, pieces(k_in, dw_in))
            carried = {'down_dx': [in_a], 'down_dw': [in_b], 'up_dw': [whole(k_out, pieces(k_out, dw_out))]}
        gcur, dnw = rms_bwd(f"rms_mix_bwd", sv['x0'], row(norm_mix_w[l]), dh, gmid)
        gl['norm_mix_w'][l] = dnw[0]

    def lbs_bwd(hb, dl):
        _, vjp = jax.vjp(f_lbs, hb)
        return [vjp(dl)[0]]
    d_hlb = whole_call("lbs_bwd", lbs_bwd, [hg_lower_bounds, jnp.concatenate(d_lbs, axis=0)], [hg_lower_bounds.shape])[0]

    part = {n: jnp.stack(v) for n, v in gl.items() if n != 'hg_lower_bounds'}
    part['hg_lower_bounds'] = d_hlb
    part['norm_final_w'] = d_nfw[0]

    slots[(*k_in, 0)] = (exchange("rs_last", pieces(k_in, dw_in)), 0)
    out = {}
    for n in BIG:
        res = None
        for idx in range(wts[n].shape[0]):
            for half_no in range(2):
                if (n, idx, half_no) in slots:
                    got, row0 = slots[n, idx, half_no]
                    res = adamw(f"adamw_{n}", got, wts[n], mom1[n], mom2[n], idx, row0, res)
        out[n] = res

    small_buf, _ = _pack([jnp.concatenate([part[n].astype(F32).reshape(-1) for n in SMALL])])
    sm = all_gather("ag_small_grads", small_buf).reshape(N_DEV, -1)
    own, at = [], 0
    for n in SMALL:
        size = part[n].size
        g8 = sm[:, at:at + size].reshape((N_DEV,) + part[n].shape)
        at += size
        if n in SMALL_SHARDED:
            g8 = lax.dynamic_slice_in_dim(g8, me * wts[n].shape[-1], wts[n].shape[-1], axis=g8.ndim - 1)
        own.append(g8.reshape(N_DEV, -1))
    sslots, _ = _pack([jnp.concatenate(own, axis=1)])
    cat = lambda dct: _pad_to(jnp.concatenate([dct[n].reshape(-1) for n in SMALL]), SEG).reshape(1, -1, 128)
    res = adamw("adamw_small", sslots, cat(wts), cat(mom1), cat(mom2))
    at = 0
    for n in SMALL:
        out[n] = [r.reshape(-1)[at:at + wts[n].size].reshape(wts[n].shape) for r in res]
        at += wts[n].size

    grad_x = gcur.reshape(x.shape)
    return (loss, grad_x, *[out[n][0] for n in WEIGHTS], *[out[n][1] for n in WEIGHTS],
            *[out[n][2] for n in WEIGHTS], *[out[n][3] for n in WEIGHTS])
```

```python
import functools
import math

import jax
import jax.numpy as jnp
from jax import lax
from jax.experimental import pallas as pl
from jax.experimental.pallas import tpu as pltpu

F32 = jnp.float32
BF16 = jnp.bfloat16
MXU = jnp.bfloat16
HI = lax.Precision.HIGHEST

N_DEV = 8
EPS = 1e-6
LRU_C = 8.0
CHUNK = 64
STATE = 128
ADAM_LR, ADAM_B1, ADAM_B2, ADAM_EPS, ADAM_WD, ADAM_STEP = 0.001, 0.9, 0.999, 1e-08, 0.01, 10

SUBLANES = 8
VMEM_LIMIT = 56 * 1024 * 1024

NN = (((1,), (0,)), ((), ()))
NT = (((1,), (1,)), ((), ()))
TN = (((0,), (0,)), ((), ()))


def _params(sem):
    return pltpu.CompilerParams(dimension_semantics=sem, vmem_limit_bytes=VMEM_LIMIT)


def _divisor(n, target, align):
    if n <= target:
        return n
    best = None
    for d in range(align, target + 1, align):
        if n % d == 0:
            best = d
    assert best is not None, (n, target, align)
    return best


def _mm(a, b, dn):
    return lax.dot_general(a.astype(MXU), b.astype(MXU), dn, preferred_element_type=F32)


@jax.custom_vjp
def mm_nn(a, b):
    return _mm(a, b, NN)


@jax.custom_vjp
def mm_nt(a, b):
    return _mm(a, b, NT)


@jax.custom_vjp
def mm_tn(a, b):
    return _mm(a, b, TN)


mm_nn.defvjp(lambda a, b: (_mm(a, b, NN), (a, b)), lambda r, g: (mm_nt(g, r[1]), mm_tn(r[0], g)))
mm_nt.defvjp(lambda a, b: (_mm(a, b, NT), (a, b)), lambda r, g: (mm_nn(g, r[1]), mm_tn(g, r[0])))
mm_tn.defvjp(lambda a, b: (_mm(a, b, TN), (a, b)), lambda r, g: (mm_nt(r[1], g), mm_nn(r[0], g)))


def dot_hi(a, b, dn=NN):
    return lax.dot_general(a, b, dn, precision=HI, preferred_element_type=F32)


def _onehot_mm(x, onehot, dn):
    hi = x.astype(MXU)
    rest = x - hi.astype(F32)
    mid = rest.astype(MXU)
    lo = (rest - mid.astype(F32)).astype(MXU)
    oh = onehot.astype(MXU)
    return sum(lax.dot_general(p, oh, dn, preferred_element_type=F32) for p in (hi, mid, lo))


@jax.custom_vjp
def spread_dot(x, onehot):
    return _onehot_mm(x, onehot, NN)


spread_dot.defvjp(lambda x, onehot: (_onehot_mm(x, onehot, NN), onehot),
                  lambda onehot, g: (_onehot_mm(g, onehot, NT), jnp.zeros_like(onehot)))


def _iota(shape, dim):
    return lax.broadcasted_iota(jnp.int32, shape, dim)


def _tril(n):
    return (_iota((n, n), 0) >= _iota((n, n), 1)).astype(F32)


def _softplus(x):
    return jnp.maximum(x, 0.0) + jnp.log1p(jnp.exp(-jnp.abs(x)))


def _neg_expm1(x):
    series = -x * (1.0 + x * (0.5 + x * (1.0 / 6.0 + x * (1.0 / 24.0))))
    return jnp.where(x > -0.03, series, 1.0 - jnp.exp(x))


_HBM = pl.BlockSpec(memory_space=pltpu.HBM)
_MESH = pl.DeviceIdType.MESH


def _peer_copies(kind, src_ref, dst_ref, send_sems, recv_sems, local_sem, rows=None):
    x, y, c = lax.axis_index("x"), lax.axis_index("y"), lax.axis_index("c")
    me = 4 * x + 2 * y + c
    if kind == "gather":
        pick = lambda p: src_ref
    elif rows is None:
        pick = lambda p: src_ref.at[p]
    else:
        pick = lambda p: src_ref.at[p, pl.ds(rows[0], rows[1])]
    local = pltpu.make_async_copy(pick(me), dst_ref.at[me], local_sem)
    remote = []
    for k in range(N_DEV - 1):
        px, py, pc = (x + ((k + 1) >> 2 & 1)) % 2, (y + ((k + 1) >> 1 & 1)) % 2, (c + ((k + 1) & 1)) % 2
        remote.append(pltpu.make_async_remote_copy(
            src_ref=pick(4 * px + 2 * py + pc), dst_ref=dst_ref.at[me], send_sem=send_sems.at[k],
            recv_sem=recv_sems.at[k], device_id=(px, py, pc), device_id_type=_MESH))
    return local, remote


_PEER_SEMS = [pltpu.SemaphoreType.DMA((N_DEV - 1,)), pltpu.SemaphoreType.DMA((N_DEV - 1,)), pltpu.SemaphoreType.DMA]


def matmul(name, a, b, *, ta=False, tb=False, res=None, out_dtype=F32, tm=1024, tn=1024, tk=2048, rides=(),
           b_pieces=False, out_pieces=0):
    m, k = (a.shape[1], a.shape[0]) if ta else a.shape
    if b_pieces:
        shard = b.shape[2]
        n = b.shape[1] if tb else b.shape[0] * shard
        assert (b.shape[0] * shard if tb else b.shape[1]) == k and not out_pieces, (name, a.shape, b.shape)
    else:
        n = b.shape[0] if tb else b.shape[1]
        assert (b.shape[1] if tb else b.shape[0]) == k, (name, a.shape, b.shape)
    tm, tn, tk = _divisor(m, tm, 128), _divisor(n, tn, 128), _divisor(k, tk, 128)
    if b_pieces:
        tn, tk = (tn, shard) if tb else (shard, tk)
    if out_pieces:
        tn = out_pieces
    assert m % tm == 0 and n % tn == 0 and k % tk == 0, (name, m, n, k, tm, tn, tk)
    ni, nj, nk = m // tm, n // tn, k // tk
    dn = (((0 if ta else 1,), (1 if tb else 0,)), ((), ()))
    rides = list(rides)
    nr = len(rides)
    n_in = 2 + (res is not None) + nr
    n_out = 1 + nr

    def body(*refs):
        a_ref, b_ref = refs[0], refs[1]
        r_ref = refs[2] if res is not None else None
        o_ref = refs[n_in]
        scratch = refs[n_in + n_out:]
        sems = scratch[len(scratch) - 3 * nr:]
        i, j, kk = pl.program_id(0), pl.program_id(1), pl.program_id(2)

        def peers(r):
            return _peer_copies(rides[r][0], refs[n_in - nr + r], refs[n_in + 1 + r], *sems[3 * r:3 * r + 3],
                                rows=rides[r][2])

        if nr:
            @pl.when((i == 0) & (j == 0) & (kk == 0))
            def _():
                for r in range(nr):
                    local, remote = peers(r)
                    local.start()
                    for cp in remote:
                        cp.start()

        def finish(r):
            if r_ref is not None:
                r = r + r_ref[...]
            o_ref[...] = r.astype(o_ref.dtype)

        if nk == 1:
            finish(_mm(a_ref[...], b_ref[...], dn))
        else:
            acc_ref = scratch[0]

            @pl.when(kk == 0)
            def _():
                acc_ref[...] = jnp.zeros_like(acc_ref)

            acc_ref[...] += _mm(a_ref[...], b_ref[...], dn)

            @pl.when(kk == nk - 1)
            def _():
                finish(acc_ref[...])

        if nr:
            @pl.when((i == ni - 1) & (j == nj - 1) & (kk == nk - 1))
            def _():
                for r in range(nr):
                    local, remote = peers(r)
                    for cp in remote:
                        cp.wait_recv()
                    for cp in remote:
                        cp.wait_send()
                    local.wait()

    a_spec = pl.BlockSpec((tk, tm), lambda i, j, kk: (kk, i)) if ta else pl.BlockSpec((tm, tk), lambda i, j, kk: (i, kk))
    if b_pieces:
        b_spec = (pl.BlockSpec((None, tn, tk), lambda i, j, kk: (kk, j, 0)) if tb
                  else pl.BlockSpec((None, tk, tn), lambda i, j, kk: (j, kk, 0)))
    else:
        b_spec = pl.BlockSpec((tn, tk), lambda i, j, kk: (j, kk)) if tb else pl.BlockSpec((tk, tn), lambda i, j, kk: (kk, j))
    o_spec = pl.BlockSpec((tm, tn), lambda i, j, kk: (i, j))
    ins, specs = [a, b], [a_spec, b_spec]
    if out_pieces:
        assert res is None and tn == out_pieces, (name, tn, out_pieces)
        out_specs = [pl.BlockSpec((None, tm, tn), lambda i, j, kk: (j, i, 0))]
        out_shape = [jax.ShapeDtypeStruct((nj, m, tn), out_dtype)]
    else:
        out_specs, out_shape = [o_spec], [jax.ShapeDtypeStruct((m, n), out_dtype)]
    scratch_shapes = [pltpu.VMEM((tm, tn), F32)] if nk > 1 else []
    if res is not None:
        ins.append(res)
        specs.append(o_spec)
    for kind, src, rows in rides:
        ins.append(src)
        specs.append(_HBM)
        out_specs.append(_HBM)
        if kind == "gather":
            got = (N_DEV,) + src.shape
        else:
            got = src.shape if rows is None else (N_DEV, rows[1]) + src.shape[2:]
        out_shape.append(jax.ShapeDtypeStruct(got, src.dtype))
        scratch_shapes += _PEER_SEMS
    outs = pl.pallas_call(
        body, name=name, grid=(ni, nj, nk), in_specs=specs, out_specs=out_specs, out_shape=out_shape,
        scratch_shapes=scratch_shapes, compiler_params=_params(("arbitrary", "arbitrary", "arbitrary")),
    )(*ins)
    return (outs[0], list(outs[1:])) if nr else outs[0]


def Row(arr, width=None, off=0, var=True):
    return ("row", arr, arr.shape[1] if width is None else width, off, var)


def Prev(arr, width=None, off=0, var=True):
    return ("prev", arr, arr.shape[1] if width is None else width, off, var)


def Next(arr, width=None, off=0, var=True):
    return ("next", arr, arr.shape[1] if width is None else width, off, var)


def Full(arr, width=None, off=0, var=True):
    return ("full", arr, arr.shape[1] if width is None else width, off, var)


def rows_call(name, fn, rows, tile, ncol, ins, outs=(), accs=()):
    nrow = rows // tile
    assert rows % tile == 0 and tile % SUBLANES == 0, (name, rows, tile)
    last8 = rows // SUBLANES - 1
    per8 = tile // SUBLANES

    def spec(kind, arr, width, off, var):
        col = (lambda j: off + j) if var else (lambda j: off)
        if kind == "row":
            return pl.BlockSpec((tile, width), lambda j, i: (i, col(j)))
        if kind == "prev":
            return pl.BlockSpec((SUBLANES, width), lambda j, i: (jnp.maximum(i * per8 - 1, 0), col(j)))
        if kind == "next":
            return pl.BlockSpec((SUBLANES, width), lambda j, i: (jnp.minimum((i + 1) * per8, last8), col(j)))
        return pl.BlockSpec((arr.shape[0], width), lambda j, i: (0, col(j)))

    n_in, n_out = len(ins), len(outs)

    def body(*refs):
        j, i = pl.program_id(0), pl.program_id(1)
        o_tiles, a_tiles = fn(i, j, *[r[...] for r in refs[:n_in]])
        for r, o in zip(refs[n_in:n_in + n_out], o_tiles, strict=True):
            r[...] = o.astype(r.dtype)
        acc_refs = refs[n_in + n_out:]
        if acc_refs:
            @pl.when(i == 0)
            def _():
                for r in acc_refs:
                    r[...] = jnp.zeros_like(r)
            for r, a in zip(acc_refs, a_tiles, strict=True):
                r[...] += a

    out_shape = [jax.ShapeDtypeStruct((rows, w * ncol), dt) for dt, w in outs]
    out_shape += [jax.ShapeDtypeStruct((r, w * ncol), F32) for r, w in accs]
    out_specs = [pl.BlockSpec((tile, w), lambda j, i: (i, j)) for _, w in outs]
    out_specs += [pl.BlockSpec((r, w), lambda j, i: (0, j)) for r, w in accs]
    res = pl.pallas_call(
        body, name=name, grid=(ncol, nrow), in_specs=[spec(*s) for s in ins], out_specs=out_specs,
        out_shape=out_shape, compiler_params=_params(("arbitrary", "arbitrary")),
    )(*[s[1] for s in ins])
    return res


def _shift_down(tile, prev8, s, first):
    if s == 0:
        return tile
    rolled = pltpu.roll(tile, s, 0)
    pr = jnp.where(first, 0.0, pltpu.roll(prev8, s, 0))
    head = jnp.where(_iota(pr.shape, 0) < s, pr, rolled[:SUBLANES])
    return jnp.concatenate([head, rolled[SUBLANES:]], axis=0)


def _shift_up(tile, next8, s, last):
    if s == 0:
        return tile
    t = tile.shape[0]
    rolled = pltpu.roll(tile, t - s, 0)
    nx = jnp.where(last, 0.0, pltpu.roll(next8, SUBLANES - s, 0))
    tail = jnp.where(_iota(nx.shape, 0) >= SUBLANES - s, nx, rolled[t - SUBLANES:])
    return jnp.concatenate([rolled[:t - SUBLANES], tail], axis=0)


def _row(w, k):
    return jnp.sum(jnp.where(_iota(w.shape, 0) == k, w, 0.0), axis=0, keepdims=True)


def _conv(x, prev8, w, b, first):
    kk = w.shape[0]
    y = b + _row(w, kk - 1) * x
    for k in range(kk - 1):
        y = y + _row(w, k) * _shift_down(x, prev8, kk - 1 - k, first)
    return y


def _conv_wgrad(x, prev8, dy, kk, first):
    out = jnp.zeros((SUBLANES, x.shape[1]), F32)
    for k in range(kk):
        r = jnp.sum(dy * _shift_down(x, prev8, kk - 1 - k, first), axis=0, keepdims=True)
        out = out + jnp.where(_iota(out.shape, 0) == k, r, 0.0)
    return out


def _conv_t(dy, next8, w, last):
    kk = w.shape[0]
    dx = _row(w, kk - 1) * dy
    for k in range(kk - 1):
        dx = dx + _row(w, k) * _shift_up(dy, next8, kk - 1 - k, last)
    return dx


def f_rms(x, w):
    return x * lax.rsqrt(jnp.mean(x * x, axis=-1, keepdims=True) + EPS) * w


def rms_fwd(name, x, w):
    def fn(i, j, xt, wt):
        return [f_rms(xt, wt)], []
    return rows_call(name, fn, x.shape[0], _divisor(x.shape[0], 512, 8), 1, [Row(x), Full(w)],
                     outs=[(BF16, x.shape[1])])[0]


def rms_bwd(name, x, w, dh, dres):
    def fn(i, j, xt, wt, dht, drt):
        _, vjp = jax.vjp(f_rms, xt, wt)
        dx, dw = vjp(dht)
        return [drt + dx], [dw]
    d = x.shape[1]
    return rows_call(name, fn, x.shape[0], _divisor(x.shape[0], 256, 8), 1, [Row(x), Full(w), Row(dh), Row(dres)],
                     outs=[(F32, d)], accs=[(1, d)])


def loss_head(name, x, w, target):
    def fn(i, j, xt, wt, tt):
        def f(xx, ww):
            err = f_rms(xx, ww) - tt
            return 0.5 * jnp.mean(err * err, axis=-1, keepdims=True)
        rows, vjp = jax.vjp(f, xt, wt)
        dx, dw = vjp(jnp.ones_like(rows))
        return [dx], [dw, jnp.broadcast_to(jnp.sum(rows, axis=0, keepdims=True), (1, 128))]
    d = x.shape[1]
    return rows_call(name, fn, x.shape[0], _divisor(x.shape[0], 256, 8), 1, [Row(x), Full(w), Row(target)],
                     outs=[(F32, d)], accs=[(1, d), (1, 128)])


def ffn_act_fwd(name, u0, cw, cb):
    t, two_f = u0.shape
    wc = _divisor(two_f // 2, 512, 128)
    nc = two_f // 2 // wc

    def fn(i, j, ug, ugp, uv, uvp, wg, wv, bg, bv):
        first = i == 0
        g = _conv(ug, ugp, wg, bg, first)
        v = _conv(uv, uvp, wv, bv, first)
        return [jax.nn.silu(g) * v], []
    ins = [Row(u0, wc), Prev(u0, wc), Row(u0, wc, nc), Prev(u0, wc, nc),
           Full(cw, wc), Full(cw, wc, nc), Full(cb, wc), Full(cb, wc, nc)]
    return rows_call(name, fn, t, _divisor(t, 1024, 8), nc, ins, outs=[(BF16, wc)])[0]


def ffn_act_bwd(name, u0, cw, cb, dact):
    t, two_f = u0.shape
    wc = _divisor(two_f // 2, 512, 128)
    nc = two_f // 2 // wc
    kk = cw.shape[0]
    tile = _divisor(t, 512, 8)
    last_i = t // tile - 1

    def conv_t(dy, w):
        rows = dy.shape[0]
        dx = _row(w, kk - 1) * dy[:tile]
        for k in range(kk - 1):
            dx = dx + _row(w, k) * pltpu.roll(dy, rows - (kk - 1 - k), 0)[:tile]
        return dx

    def fn(i, j, ug, ugp, ugn, uv, uvp, uvn, wg, wv, bg, bv, da, dan):
        first = i == 0
        ext = lambda a, b: jnp.concatenate([a, b], axis=0)
        g = _conv(ext(ug, ugn), ugp, wg, bg, first)
        v = _conv(ext(uv, uvn), uvp, wv, bv, first)
        _, vjp = jax.vjp(lambda gg, vv: jax.nn.silu(gg) * vv, g, v)
        dg, dv = vjp(ext(da, jnp.where(i == last_i, 0.0, dan)))
        accs = [_conv_wgrad(ug, ugp, dg[:tile], kk, first), _conv_wgrad(uv, uvp, dv[:tile], kk, first),
                jnp.sum(dg[:tile], axis=0, keepdims=True), jnp.sum(dv[:tile], axis=0, keepdims=True)]
        return [conv_t(dg, wg), conv_t(dv, wv)], accs
    ins = [Row(u0, wc), Prev(u0, wc), Next(u0, wc), Row(u0, wc, nc), Prev(u0, wc, nc), Next(u0, wc, nc),
           Full(cw, wc), Full(cw, wc, nc), Full(cb, wc), Full(cb, wc, nc), Row(dact, wc), Next(dact, wc)]
    return rows_call(name, fn, t, tile, nc, ins, outs=[(BF16, wc), (BF16, wc)],
                     accs=[(SUBLANES, wc), (SUBLANES, wc), (1, wc), (1, wc)])


def conv_t_call(name, dy, cw, col_off, width):
    out_dtype = BF16
    t, c = dy.shape
    nc = c // width
    nrow_tile = _divisor(t, 512, 8)
    last_i = t // nrow_tile - 1

    def fn(i, j, d, dn, w):
        return [_conv_t(d, dn, w, i == last_i)], []
    return rows_call(name, fn, t, nrow_tile, nc, [Row(dy, width), Next(dy, width), Full(cw, width, col_off)],
                     outs=[(out_dtype, width)])[0]


def f_lru_gates(xc, wr, br, wi, bi, lam):
    r = jax.nn.sigmoid(mm_nn(xc, wr) + br)
    gi = jax.nn.sigmoid(mm_nn(xc, wi) + bi)
    log_a = -LRU_C * r * _softplus(-lam)
    a = jnp.exp(log_a)
    u = jnp.sqrt(_neg_expm1(2.0 * log_a)) * (gi * xc)
    return a, u


def lru_pre_fwd(name, proj, cw, cb, wr, br, wi, bi, lam):
    t, w = proj.shape[0], lam.shape[1]

    def fn(i, j, xa, xap, cwt, cbt, wrt, brt, wit, bit, lamt):
        xc = _conv(xa, xap, cwt, cbt, i == 0)
        a, u = f_lru_gates(xc, wrt, brt, wit, bit, lamt)
        return [xc, a, u], []
    ins = [Row(proj, w), Prev(proj, w), Full(cw), Full(cb), Full(wr), Full(br), Full(wi), Full(bi), Full(lam)]
    return rows_call(name, fn, t, _divisor(t, 512, 8), 1, ins, outs=[(F32, w)] * 3)


def lru_pre_bwd(name, proj, cw, cb, wr, br, wi, bi, lam, xc, hseq, lamb):
    t, w = proj.shape[0], lam.shape[1]
    kk = cw.shape[0]

    def fn(i, j, xa, xap, xct, hs, hsp, lb, wrt, brt, wit, bit, lamt):
        first = i == 0
        da = lb * _shift_down(hs, hsp, 1, first)
        _, vjp = jax.vjp(f_lru_gates, xct, wrt.astype(F32), brt, wit.astype(F32), bit, lamt)
        dxc, dwr, dbr, dwi, dbi, dlam = vjp((da, lb))
        accs = [_conv_wgrad(xa, xap, dxc, kk, first), jnp.sum(dxc, axis=0, keepdims=True), dwr, dbr, dwi, dbi, dlam]
        return [dxc], accs
    ins = [Row(proj, w), Prev(proj, w), Row(xc), Row(hseq), Prev(hseq), Row(lamb),
           Full(wr), Full(br), Full(wi), Full(bi), Full(lam)]
    return rows_call(name, fn, t, _divisor(t, 256, 8), 1, ins, outs=[(F32, w)],
                     accs=[(SUBLANES, w), (1, w), (w, w), (1, w), (w, w), (1, w), (1, w)])


def lin_scan(name, a, x, reverse):
    t, c = a.shape
    tile = _divisor(t, 512, 8)
    n = t // tile

    def body(a_ref, x_ref, o_ref, c_ref):
        @pl.when(pl.program_id(0) == 0)
        def _():
            c_ref[...] = jnp.zeros_like(c_ref)

        def step(s, carry):
            r = (tile - 1 - s) if reverse else s
            at, xt = a_ref[pl.ds(r, 1), :], x_ref[pl.ds(r, 1), :]
            o = (xt + carry) if reverse else (at * carry + xt)
            o_ref[pl.ds(r, 1), :] = o
            return (at * o) if reverse else o
        c_ref[...] = lax.fori_loop(0, tile, step, c_ref[...], unroll=8)

    spec = pl.BlockSpec((tile, c), (lambda i: (n - 1 - i, 0)) if reverse else (lambda i: (i, 0)))
    return pl.pallas_call(
        body, name=name, grid=(n,), in_specs=[spec, spec], out_specs=spec,
        out_shape=jax.ShapeDtypeStruct((t, c), F32), scratch_shapes=[pltpu.VMEM((1, c), F32)],
        compiler_params=_params(("arbitrary",)),
    )(a, x)


def _hg_chunk(s, q, fr, v, lb):
    f = lb + (1.0 - lb) * jax.nn.sigmoid(fr)
    k = 1.0 - f
    g = jnp.log(f)
    qs = jax.nn.silu(q) * (STATE ** -0.5)
    cum = dot_hi(_tril(CHUNK), g)
    tot = jnp.sum(g, axis=0, keepdims=True)
    mid = jnp.sum(jnp.where(_iota(g.shape, 0) < CHUNK // 2, g, 0.0), axis=0, keepdims=True)
    scores = mm_nt(qs * jnp.exp(cum - mid), k * jnp.exp(mid - cum))
    scores = jnp.where(_tril(CHUNK) > 0, scores, 0.0)
    o = mm_nn(scores, v) + mm_nn(qs * jnp.exp(cum), s)
    decay = jnp.broadcast_to(jnp.exp(tot), s.shape).T
    s_new = decay * s + mm_tn(k * jnp.exp(tot - cum), v)
    return o, s_new


def _hg_specs(proj, heads, lru_w, hg_w, rows, rev):
    nblk = proj.shape[0] // rows
    blk = (lambda b: nblk - 1 - b) if rev else (lambda b: b)
    base = 2 * lru_w // STATE
    per = hg_w // STATE
    col = [pl.BlockSpec((rows, STATE), functools.partial(lambda h, b, o: (blk(b), o + h), o=base + k * per))
           for k in range(3)]
    return nblk, blk, col


def hg_fwd(name, proj, lbs, lru_w, cb=4):
    t, hg_w = proj.shape[0], lbs.shape[1]
    heads = hg_w // STATE
    cb = min(cb, t // CHUNK)
    rows = cb * CHUNK
    nblk, blk, col = _hg_specs(proj, heads, lru_w, hg_w, rows, False)

    def body(q_ref, f_ref, v_ref, lb_ref, o_ref, s_ref, st):
        @pl.when(pl.program_id(1) == 0)
        def _():
            st[...] = jnp.zeros_like(st)
        s = st[...]
        for c in range(cb):
            sl = slice(c * CHUNK, (c + 1) * CHUNK)
            s_ref[c] = s
            o, s = _hg_chunk(s, q_ref[sl, :], f_ref[sl, :], v_ref[sl, :], lb_ref[...])
            o_ref[sl, :] = o
        st[...] = s

    return pl.pallas_call(
        body, name=name, grid=(heads, nblk),
        in_specs=col + [pl.BlockSpec((1, STATE), lambda h, b: (0, h))],
        out_specs=[pl.BlockSpec((rows, STATE), lambda h, b: (b, h)),
                   pl.BlockSpec((cb, None, STATE, STATE), lambda h, b: (b, h, 0, 0))],
        out_shape=[jax.ShapeDtypeStruct((t, hg_w), F32),
                   jax.ShapeDtypeStruct((t // CHUNK, heads, STATE, STATE), F32)],
        scratch_shapes=[pltpu.VMEM((STATE, STATE), F32)],
        compiler_params=_params(("arbitrary", "arbitrary")),
    )(proj, proj, proj, lbs)


def hg_bwd(name, proj, lbs, states, do, lru_w, cb=4):
    t, hg_w = proj.shape[0], lbs.shape[1]
    heads = hg_w // STATE
    cb = min(cb, t // CHUNK)
    rows = cb * CHUNK
    nblk, blk, col = _hg_specs(proj, heads, lru_w, hg_w, rows, True)

    def body(q_ref, f_ref, v_ref, lb_ref, s_ref, do_ref, dq_ref, df_ref, dv_ref, dlb_ref, dst):
        @pl.when(pl.program_id(1) == 0)
        def _():
            dst[...] = jnp.zeros_like(dst)
            dlb_ref[...] = jnp.zeros_like(dlb_ref)
        ds = dst[...]
        dlb = jnp.zeros((1, STATE), F32)
        for c in reversed(range(cb)):
            sl = slice(c * CHUNK, (c + 1) * CHUNK)
            _, vjp = jax.vjp(_hg_chunk, s_ref[c], q_ref[sl, :], f_ref[sl, :], v_ref[sl, :], lb_ref[...])
            ds, dq, df, dv, dl = vjp((do_ref[sl, :], ds))
            dq_ref[sl, :] = dq.astype(dq_ref.dtype)
            df_ref[sl, :] = df.astype(df_ref.dtype)
            dv_ref[sl, :] = dv.astype(dv_ref.dtype)
            dlb = dlb + dl
        dst[...] = ds
        dlb_ref[...] += dlb

    rspec = pl.BlockSpec((rows, STATE), lambda h, b: (blk(b), h))
    return pl.pallas_call(
        body, name=name, grid=(heads, nblk),
        in_specs=col + [pl.BlockSpec((1, STATE), lambda h, b: (0, h)),
                        pl.BlockSpec((cb, None, STATE, STATE), lambda h, b: (blk(b), h, 0, 0)), rspec],
        out_specs=[rspec, rspec, rspec, pl.BlockSpec((1, STATE), lambda h, b: (0, h))],
        out_shape=[jax.ShapeDtypeStruct((t, hg_w), BF16)] * 3 + [jax.ShapeDtypeStruct((1, hg_w), F32)],
        scratch_shapes=[pltpu.VMEM((STATE, STATE), F32)],
        compiler_params=_params(("arbitrary", "arbitrary")),
    )(proj, proj, proj, lbs, states, do)


def f_even_post(hseq, ga, ob, gb, nw):
    parts = [hseq * jax.nn.gelu(ga)]
    for h in range(ob.shape[1] // STATE):
        o = ob[:, h * STATE:(h + 1) * STATE]
        on = o * lax.rsqrt(jnp.mean(o * o, axis=-1, keepdims=True) + EPS) * nw
        parts.append(on * jax.nn.silu(gb[:, h * STATE:(h + 1) * STATE]))
    return jnp.concatenate(parts, axis=-1)


def _even_post_ins(proj, hseq, ob, nw):
    w, v = hseq.shape[1], ob.shape[1]
    assert w == v
    return [Row(hseq), Row(proj, w, 1), Row(ob), Row(proj, v, (2 * w + 3 * v) // v), Full(nw)]


def even_post_fwd(name, proj, hseq, ob, nw):
    t = proj.shape[0]

    def fn(i, j, hs, ga, o, gb, nwt):
        return [f_even_post(hs, ga, o, gb, nwt)], []
    return rows_call(name, fn, t, _divisor(t, 256, 8), 1, _even_post_ins(proj, hseq, ob, nw),
                     outs=[(BF16, hseq.shape[1] + ob.shape[1])])[0]


def even_post_bwd(name, proj, hseq, ob, nw, dy):
    t, w, v = proj.shape[0], hseq.shape[1], ob.shape[1]

    def fn(i, j, hs, ga, o, gb, nwt, dyt):
        _, vjp = jax.vjp(f_even_post, hs, ga, o, gb, nwt)
        dhs, dga, dob, dgb, dnw = vjp(dyt)
        return [dhs, dga, dob, dgb], [dnw]
    return rows_call(name, fn, t, _divisor(t, 256, 8), 1, _even_post_ins(proj, hseq, ob, nw) + [Row(dy)],
                     outs=[(F32, w), (BF16, w), (F32, v), (BF16, v)], accs=[(1, STATE)])


def f_lbs(hb):
    e = jnp.exp(hb - jnp.max(hb, axis=0, keepdims=True))
    p = e / jnp.sum(e, axis=0, keepdims=True)
    out, run = jnp.zeros_like(p), jnp.zeros_like(p[:1])
    for r in range(hb.shape[0]):
        run = run + _row(p, r)
        out = out + jnp.where(_iota(p.shape, 0) == r, run - _row(p, 0), 0.0)
    return out


def whole_call(name, fn, ins, out_shapes):
    def body(*refs):
        outs = fn(*[r[...] for r in refs[:len(ins)]])
        for r, o in zip(refs[len(ins):], outs, strict=True):
            r[...] = o
    return pl.pallas_call(body, name=name, out_shape=[jax.ShapeDtypeStruct(s, F32) for s in out_shapes])(*ins)


HEADDIM = 64


def f_ssd_prep(dtr, dtb, alog, inner):
    rows = dtr.shape[0]
    dt_all = _softplus(dtr + dtb)
    da_all = dt_all * (-jnp.exp(alog))
    tril = _tril(CHUNK)
    cums = [dot_hi(tril, da_all[c * CHUNK:(c + 1) * CHUNK]) for c in range(rows // CHUNK)]
    cum_all = jnp.concatenate(cums, axis=0) if len(cums) > 1 else cums[0]
    head_of = _iota((STATE, inner), 1) - HEADDIM * _iota((STATE, inner), 0)
    spread = ((head_of >= 0) & (head_of < HEADDIM)).astype(F32)
    even = (_iota((CHUNK, STATE), 1) == 2 * _iota((CHUNK, STATE), 0)).astype(F32)
    odd = (_iota((CHUNK, STATE), 1) == 2 * _iota((CHUNK, STATE), 0) + 1).astype(F32)
    left = _iota((CHUNK, STATE), 1) < HEADDIM
    cumrows = []
    for cm in cums:
        twice = jnp.concatenate([cm, cm], axis=0)
        cumrows.append(jnp.where(left, dot_hi(even, twice, NT), dot_hi(odd, twice, NT)))
    return spread_dot(dt_all, spread), spread_dot(cum_all, spread), tuple(cumrows)


def ssd_prep_fwd(name, dtr, dtb, alog, inner, k=4):
    t = dtr.shape[0]
    k = min(k, t // CHUNK)
    rows = k * CHUNK

    def body(dtr_ref, dtb_ref, alog_ref, dt_ref, cum_ref, cr_ref):
        dt, cum, crs = f_ssd_prep(dtr_ref[...], dtb_ref[...], alog_ref[...], inner)
        dt_ref[...] = dt
        cum_ref[...] = cum
        for c, cr in enumerate(crs):
            cr_ref[c] = cr

    one = pl.BlockSpec((1, STATE), lambda i: (0, 0))
    wide = pl.BlockSpec((rows, inner), lambda i: (i, 0))
    return pl.pallas_call(
        body, name=name, grid=(t // rows,),
        in_specs=[pl.BlockSpec((rows, STATE), lambda i: (i, 0)), one, one],
        out_specs=[wide, wide, pl.BlockSpec((k, CHUNK, STATE), lambda i: (i, 0, 0))],
        out_shape=[jax.ShapeDtypeStruct((t, inner), F32), jax.ShapeDtypeStruct((t, inner), F32),
                   jax.ShapeDtypeStruct((t // CHUNK, CHUNK, STATE), F32)],
        compiler_params=_params(("arbitrary",)),
    )(dtr, dtb, alog)


def ssd_prep_bwd(name, dtr, dtb, alog, ddt, dcum, dcr, inner, k=4):
    t = dtr.shape[0]
    k = min(k, t // CHUNK)
    rows = k * CHUNK

    def body(dtr_ref, dtb_ref, alog_ref, ddt_ref, dcum_ref, dcr_ref, ddtr_ref, ddtb_ref, dalog_ref):
        @pl.when(pl.program_id(0) == 0)
        def _():
            ddtb_ref[...] = jnp.zeros_like(ddtb_ref)
            dalog_ref[...] = jnp.zeros_like(dalog_ref)
        _, vjp = jax.vjp(functools.partial(f_ssd_prep, inner=inner), dtr_ref[...], dtb_ref[...], alog_ref[...])
        ddtr, ddtb, dalog = vjp((ddt_ref[...], dcum_ref[...], tuple(dcr_ref[c] for c in range(k))))
        ddtr_ref[...] = ddtr
        ddtb_ref[...] += ddtb
        dalog_ref[...] += dalog

    one = pl.BlockSpec((1, STATE), lambda i: (0, 0))
    wide = pl.BlockSpec((rows, inner), lambda i: (i, 0))
    tall = pl.BlockSpec((rows, STATE), lambda i: (i, 0))
    return pl.pallas_call(
        body, name=name, grid=(t // rows,),
        in_specs=[tall, one, one, wide, wide, pl.BlockSpec((k, CHUNK, STATE), lambda i: (i, 0, 0))],
        out_specs=[tall, one, one],
        out_shape=[jax.ShapeDtypeStruct((t, STATE), F32), jax.ShapeDtypeStruct((1, STATE), F32),
                   jax.ShapeDtypeStruct((1, STATE), F32)],
        compiler_params=_params(("arbitrary",)),
    )(dtr, dtb, alog, ddt, dcum, dcr)


def _ssd_group(states, x, bm, cm, dt, cum, cumrs, dsk):
    half = _iota((CHUNK, STATE), 1) >= HEADDIM
    pos = _iota((CHUNK, STATE), 1) - jnp.where(half, HEADDIM, 0)
    row = _iota((CHUNK, STATE), 0)
    causal = row >= pos
    cb2 = mm_nt(cm, jnp.concatenate([bm, bm], axis=0))
    ys, new = [], []
    for pp, (s, cumr) in enumerate(zip(states, cumrs, strict=True)):
        lanes = slice(pp * STATE, (pp + 1) * STATE)
        cu, xdt = cum[:, lanes], x[:, lanes] * dt[:, lanes]
        tot = jnp.sum(jnp.where(row == CHUNK - 1, cu, 0.0), axis=0, keepdims=True)
        m = jnp.where(causal, cb2 * jnp.exp(jnp.where(causal, cu - cumr, 0.0)), 0.0)
        x2 = jnp.concatenate([jnp.where(half, 0.0, xdt), jnp.where(half, xdt, 0.0)], axis=0)
        ys.append(mm_nn(m, x2) + mm_nt(cm, s) * jnp.exp(cu) + x[:, lanes] * dsk[:, lanes])
        decay = jnp.broadcast_to(jnp.exp(tot), s.shape).T
        new.append(decay * s + mm_tn(xdt * jnp.exp(tot - cu), bm))
    return jnp.concatenate(ys, axis=1), tuple(new)


SSD_CHUNKS_PER_STEP = 8


def _ssd_specs(act, inner, groups, rev):
    t = act.shape[0]
    cb = min(SSD_CHUNKS_PER_STEP, t // CHUNK)
    rows = cb * CHUNK
    nblk = t // rows
    ch = (lambda c: nblk - 1 - c) if rev else (lambda c: c)
    gw = inner // groups
    wide = pl.BlockSpec((rows, gw), lambda c, g: (ch(c), g))
    specs = [wide,
             pl.BlockSpec((rows, STATE), lambda c, g: (ch(c), inner // STATE + g)),
             pl.BlockSpec((rows, STATE), lambda c, g: (ch(c), inner // STATE + groups + g)),
             wide, wide,
             pl.BlockSpec((cb, CHUNK, STATE), lambda c, g: (ch(c), 0, 0)),
             pl.BlockSpec((1, gw), lambda c, g: (0, g))]
    return cb, nblk, ch, gw, specs


def ssd_fwd(name, act, dt, cum, cumrow, dexp, inner, groups):
    t = act.shape[0]
    cb, nblk, ch, gw, specs = _ssd_specs(act, inner, groups, False)
    pairs = gw // (2 * HEADDIM)

    def body(x_ref, b_ref, c_ref, dt_ref, cum_ref, cr_ref, dsk_ref, y_ref, sv_ref, st):
        c, g = pl.program_id(0), pl.program_id(1)

        @pl.when(c == 0)
        def _():
            for pp in range(pairs):
                st[g * pairs + pp] = jnp.zeros((STATE, STATE), F32)

        states = tuple(st[g * pairs + pp] for pp in range(pairs))
        for k in range(cb):
            rs = slice(k * CHUNK, (k + 1) * CHUNK)
            cumrs = tuple(cr_ref[k, pl.ds(g * pairs + pp, 1), :] for pp in range(pairs))
            for pp in range(pairs):
                sv_ref[k, pp] = states[pp]
            y, states = _ssd_group(states, x_ref[rs, :], b_ref[rs, :], c_ref[rs, :], dt_ref[rs, :], cum_ref[rs, :],
                                   cumrs, dsk_ref[...])
            y_ref[rs, :] = y
        for pp in range(pairs):
            st[g * pairs + pp] = states[pp]

    return pl.pallas_call(
        body, name=name, grid=(nblk, groups), in_specs=specs,
        out_specs=[pl.BlockSpec((cb * CHUNK, gw), lambda c, g: (c, g)),
                   pl.BlockSpec((cb, pairs, STATE, STATE), lambda c, g: (c, g, 0, 0))],
        out_shape=[jax.ShapeDtypeStruct((t, inner), F32),
                   jax.ShapeDtypeStruct((t // CHUNK, groups * pairs, STATE, STATE), F32)],
        scratch_shapes=[pltpu.VMEM((groups * pairs, STATE, STATE), F32)],
        compiler_params=_params(("arbitrary", "arbitrary")),
    )(act, act, act, dt, cum, cumrow, dexp)


def ssd_bwd(name, act, dt, cum, cumrow, dexp, states, dy, inner, groups):
    t = act.shape[0]
    cb, nblk, ch, gw, specs = _ssd_specs(act, inner, groups, True)
    pairs = gw // (2 * HEADDIM)

    def body(x_ref, b_ref, c_ref, dt_ref, cum_ref, cr_ref, dsk_ref, sv_ref, dy_ref,
             dx_ref, db_ref, dc_ref, ddt_ref, dcum_ref, dcr_ref, ddsk_ref, dst):
        c, g = pl.program_id(0), pl.program_id(1)

        @pl.when((c == 0) & (g == 0))
        def _():
            ddsk_ref[...] = jnp.zeros_like(ddsk_ref)

        @pl.when(c == 0)
        def _():
            for pp in range(pairs):
                dst[g * pairs + pp] = jnp.zeros((STATE, STATE), F32)

        @pl.when(g == 0)
        def _():
            dcr_ref[...] = jnp.zeros_like(dcr_ref)

        ds = tuple(dst[g * pairs + pp] for pp in range(pairs))
        ddsk = jnp.zeros((1, gw), F32)
        for k in reversed(range(cb)):
            rs = slice(k * CHUNK, (k + 1) * CHUNK)
            cumrs = tuple(cr_ref[k, pl.ds(g * pairs + pp, 1), :] for pp in range(pairs))
            _, vjp = jax.vjp(_ssd_group, tuple(sv_ref[k, pp] for pp in range(pairs)), x_ref[rs, :], b_ref[rs, :],
                             c_ref[rs, :], dt_ref[rs, :], cum_ref[rs, :], cumrs, dsk_ref[...])
            ds, dx, db, dc, ddt, dcum, dcrs, ddskk = vjp((dy_ref[rs, :], ds))
            for pp in range(pairs):
                dcr_ref[k, pl.ds(g * pairs + pp, 1), :] = dcrs[pp]
            dx_ref[rs, :] = dx
            db_ref[rs, :] = db
            dc_ref[rs, :] = dc
            ddt_ref[rs, :] = ddt
            dcum_ref[rs, :] = dcum
            ddsk = ddsk + ddskk
        for pp in range(pairs):
            dst[g * pairs + pp] = ds[pp]
        col = pl.ds(pl.multiple_of(g * gw, STATE), gw)
        ddsk_ref[:, col] = ddsk_ref[:, col] + ddsk

    wide = pl.BlockSpec((cb * CHUNK, gw), lambda c, g: (ch(c), g))
    grp = pl.BlockSpec((cb * CHUNK, STATE), lambda c, g: (ch(c), g))
    return pl.pallas_call(
        body, name=name, grid=(nblk, groups),
        in_specs=specs + [pl.BlockSpec((cb, pairs, STATE, STATE), lambda c, g: (ch(c), g, 0, 0)), wide],
        out_specs=[wide, grp, grp, wide, wide, pl.BlockSpec((cb, CHUNK, STATE), lambda c, g: (ch(c), 0, 0)),
                   pl.BlockSpec((1, inner), lambda c, g: (0, 0))],
        out_shape=[jax.ShapeDtypeStruct((t, inner), F32), jax.ShapeDtypeStruct((t, groups * STATE), F32),
                   jax.ShapeDtypeStruct((t, groups * STATE), F32), jax.ShapeDtypeStruct((t, inner), F32),
                   jax.ShapeDtypeStruct((t, inner), F32), jax.ShapeDtypeStruct((t // CHUNK, CHUNK, STATE), F32),
                   jax.ShapeDtypeStruct((1, inner), F32)],
        scratch_shapes=[pltpu.VMEM((groups * pairs, STATE, STATE), F32)],
        compiler_params=_params(("arbitrary", "arbitrary")),
    )(act, act, act, dt, cum, cumrow, dexp, states, dy)


def _ssd_conv_width(inner, cdim):
    return _divisor(math.gcd(inner, cdim), 2048, 128)


def ssd_conv_fwd(name, zx, cw, cb, inner):
    t, cdim = zx.shape[0], cw.shape[1]
    wc = _ssd_conv_width(inner, cdim)

    def fn(i, j, xt, xp, w, b):
        return [jax.nn.silu(_conv(xt, xp, w, b, i == 0))], []
    ins = [Row(zx, wc, inner // wc), Prev(zx, wc, inner // wc), Full(cw, wc), Full(cb, wc)]
    return rows_call(name, fn, t, _divisor(t, 256, 8), cdim // wc, ins, outs=[(F32, wc)])[0]


def ssd_conv_bwd(name, zx, cw, cb, dact, inner):
    t, cdim = zx.shape[0], cw.shape[1]
    wc = _ssd_conv_width(inner, cdim)
    kk = cw.shape[0]

    def fn(i, j, xt, xp, w, b, da):
        first = i == 0
        pre = _conv(xt, xp, w, b, first)
        _, vjp = jax.vjp(jax.nn.silu, pre)
        dpre, = vjp(da)
        return [dpre], [_conv_wgrad(xt, xp, dpre, kk, first), jnp.sum(dpre, axis=0, keepdims=True)]
    ins = [Row(zx, wc, inner // wc), Prev(zx, wc, inner // wc), Full(cw, wc), Full(cb, wc), Row(dact, wc)]
    return rows_call(name, fn, t, _divisor(t, 256, 8), cdim // wc, ins, outs=[(F32, wc)],
                     accs=[(SUBLANES, wc), (1, wc)])


def f_ssd_post(y, z, nw):
    yz = y * jax.nn.silu(z)
    return yz * lax.rsqrt(jnp.mean(yz * yz, axis=-1, keepdims=True) + EPS) * nw


def ssd_post_fwd(name, y, zx, nw, groups):
    t, inner = y.shape
    gw = inner // groups

    def fn(i, j, yt, zt, nwt):
        return [f_ssd_post(yt, zt, nwt)], []
    return rows_call(name, fn, t, _divisor(t, 1024, 8), groups, [Row(y, gw), Row(zx, gw), Full(nw, gw)],
                     outs=[(BF16, gw)])[0]


def ssd_post_bwd(name, y, zx, nw, dyn, groups):
    t, inner = y.shape
    gw = inner // groups

    def fn(i, j, yt, zt, nwt, dt):
        _, vjp = jax.vjp(f_ssd_post, yt, zt, nwt)
        dy, dz, dnw = vjp(dt)
        return [dy, dz], [dnw]
    return rows_call(name, fn, t, _divisor(t, 512, 8), groups, [Row(y, gw), Row(zx, gw), Full(nw, gw), Row(dyn, gw)],
                     outs=[(F32, gw), (BF16, gw)], accs=[(1, gw)])


ADAMW_TILE = 256 * 1024


def adamw(name, slots, w, m, v, layer=0, row0=0, prev=None):
    nl, _, c = w.shape
    n = slots.shape[1]
    tr = _divisor(math.gcd(row0, n), max(16, ADAMW_TILE // c // 16 * 16), 16)
    first = row0 // tr

    def body(s_ref, w_ref, m_ref, v_ref, *rest):
        g_ref, d_ref, mo_ref, vo_ref = rest[-4:]
        g = s_ref[0].astype(F32)
        for k in range(1, N_DEV):
            g = g + s_ref[k].astype(F32)
        mn = ADAM_B1 * m_ref[...] + (1.0 - ADAM_B1) * g
        vn = ADAM_B2 * v_ref[...] + (1.0 - ADAM_B2) * (g * g)
        m_hat = mn / (1.0 - ADAM_B1 ** ADAM_STEP)
        v_hat = vn / (1.0 - ADAM_B2 ** ADAM_STEP)
        g_ref[...] = g
        d_ref[...] = -ADAM_LR * (m_hat / (jnp.sqrt(v_hat) + ADAM_EPS) + ADAM_WD * w_ref[...])
        mo_ref[...] = mn
        vo_ref[...] = vn

    spec = pl.BlockSpec((None, tr, c), lambda i: (layer, first + i, 0))
    prev = list(prev) if prev is not None else []
    return pl.pallas_call(
        body, name=name, grid=(n // tr,),
        in_specs=[pl.BlockSpec((N_DEV, tr, c), lambda i: (0, i, 0)), spec, spec, spec] + [_HBM] * len(prev),
        out_specs=[spec] * 4, out_shape=[jax.ShapeDtypeStruct(w.shape, F32)] * 4,
        input_output_aliases={4 + k: k for k in range(len(prev))},
        compiler_params=_params(("parallel",)),
    )(slots, w, m, v, *prev)


def all_gather(name, shard):
    def body(x_ref, out_ref, send_sems, recv_sems, local_sem):
        x, y, c = lax.axis_index("x"), lax.axis_index("y"), lax.axis_index("c")
        me, sibling = (x, y, c), (x, y, 1 - c)
        chips = [(1 - x, y), (x, 1 - y), (1 - x, 1 - y)]

        def slab(px, py, pc):
            return out_ref.at[4 * px + 2 * py + pc]

        def copy(k, block, to, src=None):
            return pltpu.make_async_remote_copy(
                src_ref=slab(*block) if src is None else src, dst_ref=slab(*block),
                send_sem=send_sems.at[k], recv_sem=recv_sems.at[k], device_id=to, device_id_type=_MESH)

        mine = pltpu.make_async_copy(x_ref, slab(*me), local_sem)
        mine.start()
        first = [copy(0, me, sibling, src=x_ref)]
        first += [copy(1 + j, me, (*chip, c), src=x_ref) for j, chip in enumerate(chips)]
        for cp in first:
            cp.start()
        passed = [copy(4 + j, (*chip, c), sibling) for j, chip in enumerate(chips)]
        for j, chip in enumerate(chips):
            copy(1 + j, (*chip, c), me).wait_recv()
            passed[j].start()
        copy(0, sibling, me).wait_recv()
        for j, chip in enumerate(chips):
            copy(4 + j, (*chip, 1 - c), me).wait_recv()
        for cp in first + passed:
            cp.wait_send()
        mine.wait()

    return pl.pallas_call(
        body, name=name, out_shape=jax.ShapeDtypeStruct((N_DEV,) + shard.shape, shard.dtype),
        in_specs=[_HBM], out_specs=_HBM,
        scratch_shapes=[pltpu.SemaphoreType.DMA((7,)), pltpu.SemaphoreType.DMA((7,)), pltpu.SemaphoreType.DMA],
    )(shard)


def exchange(name, pieces):
    def body(p_ref, out_ref, send_sems, recv_sems, local_sem):
        local, remote = _peer_copies("exchange", p_ref, out_ref, send_sems, recv_sems, local_sem)
        local.start()
        for cp in remote:
            cp.start()
        for cp in remote:
            cp.wait_recv()
        for cp in remote:
            cp.wait_send()
        local.wait()

    return pl.pallas_call(
        body, name=name, out_shape=jax.ShapeDtypeStruct(pieces.shape, pieces.dtype),
        in_specs=[_HBM], out_specs=_HBM, scratch_shapes=_PEER_SEMS,
    )(pieces)


WEIGHTS = ['norm_mix_w', 'norm_ffn_w', 'norm_final_w', 'ev_w_in', 'lru_conv_w', 'lru_conv_b', 'lru_w_r', 'lru_b_r',
           'lru_w_i', 'lru_b_i', 'lru_lambda', 'hg_lower_bounds', 'hg_norm_w', 'ev_w_out', 'ssd_w_in', 'ssd_conv_w',
           'ssd_conv_b', 'ssd_dt_bias', 'ssd_a_log', 'ssd_d', 'ssd_norm_w', 'ssd_w_out', 'ffn_w_up', 'ffn_conv_w',
           'ffn_conv_b', 'ffn_w_down']
COL_SHARDED = ['ev_w_in', 'ssd_w_in', 'ffn_w_up']
ROW_SHARDED = ['ev_w_out', 'ssd_w_out', 'ffn_w_down']
BIG = ['ev_w_in', 'ev_w_out', 'ssd_w_out', 'ffn_w_up', 'ffn_w_down', 'ssd_w_in']
SMALL_SHARDED = ['lru_conv_w', 'ssd_conv_w', 'ssd_conv_b', 'ssd_norm_w', 'ffn_conv_w']
SMALL = [n for n in WEIGHTS if n not in BIG]
SEG = 16 * 128


def _pad_to(flat, mult):
    extra = (-flat.shape[-1]) % mult
    if extra == 0:
        return flat
    return jnp.pad(flat, [(0, 0)] * (flat.ndim - 1) + [(0, extra)])


def _pack(segments):
    offs, parts, at = [], [], 0
    for s in segments:
        s = _pad_to(s, SEG)
        offs.append(at)
        at += s.shape[-1]
        parts.append(s)
    buf = jnp.concatenate(parts, axis=-1)
    return buf.reshape(buf.shape[:-1] + (at // 128, 128)), offs


def _unshard_last(g):
    g = jnp.moveaxis(g, 0, -2)
    return g.reshape(g.shape[:-2] + (N_DEV * g.shape[-1],))


def _block_diag(w):
    nb, b, _ = w.shape
    return (w[:, :, None, :] * jnp.eye(nb, dtype=w.dtype)[:, None, :, None]).reshape(nb * b, nb * b)


def _diag_blocks(dense, nb):
    b = dense.shape[0] // nb
    d4 = dense.reshape(nb, b, nb, b)
    return jnp.stack([d4[h, :, h, :] for h in range(nb)])


def kernel(x, norm_mix_w, norm_ffn_w, norm_final_w, ev_w_in, lru_conv_w, lru_conv_b, lru_w_r, lru_b_r, lru_w_i, lru_b_i, lru_lambda, hg_lower_bounds, hg_norm_w, ev_w_out, ssd_w_in, ssd_conv_w, ssd_conv_b, ssd_dt_bias, ssd_a_log, ssd_d, ssd_norm_w, ssd_w_out, ffn_w_up, ffn_conv_w, ffn_conv_b, ffn_w_down, loss_target, m_norm_mix_w, m_norm_ffn_w, m_norm_final_w, m_ev_w_in, m_lru_conv_w, m_lru_conv_b, m_lru_w_r, m_lru_b_r, m_lru_w_i, m_lru_b_i, m_lru_lambda, m_hg_lower_bounds, m_hg_norm_w, m_ev_w_out, m_ssd_w_in, m_ssd_conv_w, m_ssd_conv_b, m_ssd_dt_bias, m_ssd_a_log, m_ssd_d, m_ssd_norm_w, m_ssd_w_out, m_ffn_w_up, m_ffn_conv_w, m_ffn_conv_b, m_ffn_w_down, v_norm_mix_w, v_norm_ffn_w, v_norm_final_w, v_ev_w_in, v_lru_conv_w, v_lru_conv_b, v_lru_w_r, v_lru_b_r, v_lru_w_i, v_lru_b_i, v_lru_lambda, v_hg_lower_bounds, v_hg_norm_w, v_ev_w_out, v_ssd_w_in, v_ssd_conv_w, v_ssd_conv_b, v_ssd_dt_bias, v_ssd_a_log, v_ssd_d, v_ssd_norm_w, v_ssd_w_out, v_ffn_w_up, v_ffn_conv_w, v_ffn_conv_b, v_ffn_w_down):
    given = dict(locals())
    wts = {n: given[n] for n in WEIGHTS}
    mom1 = {n: given["m_" + n] for n in WEIGHTS}
    mom2 = {n: given["v_" + n] for n in WEIGHTS}
    me = 4 * lax.axis_index("x") + 2 * lax.axis_index("y") + lax.axis_index("c")

    depth, d = norm_mix_w.shape
    t = x.shape[1]
    x0 = x.reshape(t, d)
    target = loss_target.reshape(t, d)
    n_even, lru_w = lru_lambda.shape
    hg_w = hg_lower_bounds.shape[1]
    n_odd, heads = ssd_dt_bias.shape
    inner = N_DEV * ssd_norm_w.shape[1]
    cdim = N_DEV * ssd_conv_b.shape[1]
    groups = (cdim - inner) // (2 * STATE)
    assert inner == heads * HEADDIM and heads <= STATE and (inner // groups) % (2 * HEADDIM) == 0

    full = {}
    shard_buf, offs = _pack([wts[n].reshape(-1) for n in SMALL_SHARDED])
    gathered = all_gather("ag_small", shard_buf).reshape(N_DEV, -1)
    for n, off in zip(SMALL_SHARDED, offs, strict=True):
        full[n] = _unshard_last(gathered[:, off:off + wts[n].size].reshape((N_DEV,) + wts[n].shape))
    shard16 = {n: wts[n].astype(BF16) for n in BIG}
    mixer = lambda l: (('ev_w_in', l // 2), ('ev_w_out', l // 2)) if l % 2 == 0 else (('ssd_w_in', l // 2), ('ssd_w_out', l // 2))
    weight = {}

    as_pieces = {n: n in COL_SHARDED and wts[n].shape[2] % 128 == 0 for n in BIG}

    def arrived(key, g8):
        if as_pieces[key[0]]:
            weight[key] = g8
        elif key[0] in COL_SHARDED:
            weight[key] = jnp.transpose(g8, (1, 0, 2)).reshape(g8.shape[1], -1)
        else:
            weight[key] = g8.reshape(-1, g8.shape[2])

    def gather_ride(key):
        return ("gather", shard16[key[0]][key[1]], None)

    for key in mixer(0):
        arrived(key, all_gather(f"ag_{key[0]}", shard16[key[0]][key[1]]))
    w_zx = lambda o: weight['ssd_w_in', o][:, :inner + cdim]
    w_dt = lambda o: jnp.pad(weight['ssd_w_in', o][:, inner + cdim:], ((0, 0), (0, STATE - heads)))
    pad_h = lambda a: jnp.pad(a.reshape(1, heads), ((0, 0), (0, STATE - heads)))
    row = lambda a: a.reshape(1, -1)

    lbs = whole_call("lbs_fwd", lambda hb: [f_lbs(hb)], [hg_lower_bounds], [hg_lower_bounds.shape])[0]

    def mm_carry(name, a, b, key, **kw):
        if key is None or key in weight:
            return matmul(name, a, b, **kw)
        out, (g8,) = matmul(name, a, b, rides=[gather_ride(key)], **kw)
        arrived(key, g8)
        return out

    saved = []
    xcur = x0
    for l in range(depth):
        sv = {'x0': xcur}
        k_in, k_out = mixer(l)
        nxt_in, nxt_out = mixer(l + 1) if l + 1 < depth else (None, None)
        h = rms_fwd(f"rms_mix_fwd", xcur, row(norm_mix_w[l]))
        sv['h'] = h
        if l % 2 == 0:
            e = l // 2
            wr = _block_diag(lru_w_r[e]).astype(BF16)
            wi = _block_diag(lru_w_i[e]).astype(BF16)
            lru_p = (full['lru_conv_w'][e], row(lru_conv_b[e]), wr, row(lru_b_r[e]), wi, row(lru_b_i[e]), row(lru_lambda[e]))
            proj = mm_carry("ev_in", h, weight[k_in], ('ffn_w_up', l), b_pieces=as_pieces[k_in[0]])
            xc, a, u = lru_pre_fwd(f"lru_pre_fwd", proj, *lru_p)
            hseq = lin_scan(f"lru_scan_fwd", a, u, False)
            ob, states = hg_fwd(f"hg_fwd", proj, row(lbs[e]), lru_w)
            y = even_post_fwd(f"even_post_fwd", proj, hseq, ob, row(hg_norm_w[e]))
            xmid = mm_carry("ev_out", y, weight[k_out], ('ffn_w_down', l), res=xcur)
            sv.update(proj=proj, xc=xc, a=a, hseq=hseq, ob=ob, states=states, y=y, lru_p=lru_p)
        else:
            o = l // 2
            ssd_p = (pad_h(ssd_dt_bias[o]), pad_h(ssd_a_log[o]), row(jnp.repeat(ssd_d[o], HEADDIM)))
            wzx, wdt = w_zx(o), w_dt(o)
            zx = mm_carry("ssd_in", h, wzx, ('ffn_w_up', l))
            dtr = matmul("ssd_dt", h, wdt)
            sv.update(wzx=wzx, wdt=wdt)
            act = ssd_conv_fwd(f"ssd_conv_fwd", zx, full['ssd_conv_w'][o], row(full['ssd_conv_b'][o]), inner)
            prep = ssd_prep_fwd("ssd_prep_fwd", dtr, ssd_p[0], ssd_p[1], inner)
            ys, states = ssd_fwd(f"ssd_fwd", act, *prep, ssd_p[2], inner, groups)
            sv['prep'] = prep
            yn = ssd_post_fwd(f"ssd_post_fwd", ys, zx, row(full['ssd_norm_w'][o]), groups)
            xmid = mm_carry("ssd_out", yn, weight[k_out], ('ffn_w_down', l), res=xcur)
            sv.update(zx=zx, dtr=dtr, act=act, ys=ys, states=states, yn=yn, ssd_p=ssd_p)
        h2 = rms_fwd(f"rms_ffn_fwd", xmid, row(norm_ffn_w[l]))
        u0 = mm_carry("ffn_up", h2, weight['ffn_w_up', l], nxt_in, b_pieces=as_pieces['ffn_w_up'])
        actf = ffn_act_fwd(f"ffn_act_fwd", u0, full['ffn_conv_w'][l], row(ffn_conv_b[l]))
        xcur = mm_carry("ffn_down", actf, weight['ffn_w_down', l], nxt_out, res=xmid)
        sv.update(x1=xmid, h2=h2, u0=u0, actf=actf)
        saved.append(sv)

    gcur, d_nfw, loss_row = loss_head("loss_head", xcur, row(norm_final_w), target)
    loss = lax.psum(loss_row[0, 0], ("x", "y", "c"))

    gl = {n: [None] * wts[n].shape[0] for n in SMALL if n != 'norm_final_w'}
    d_lbs = [None] * n_even
    slots = {}
    dw_opts = dict(ta=True, out_dtype=BF16, tm=2048, tk=1024)

    def pieces(key, dw):
        if as_pieces[key[0]]:
            return dw
        if key[0] in COL_SHARDED:
            return jnp.transpose(dw.reshape(dw.shape[0], N_DEV, -1), (1, 0, 2))
        return dw.reshape(N_DEV, -1, dw.shape[1])

    def halves(key, p):
        r = p.shape[1] // 2
        return ((*key, 0), p, (0, r)), ((*key, 1), p, (r, p.shape[1] - r))

    def whole(key, p):
        return ((*key, 0), p, None)

    def mm_send(name, a, b, sends, **kw):
        if not sends:
            return matmul(name, a, b, **kw)
        out, got = matmul(name, a, b, rides=[("exchange", p, rows) for _, p, rows in sends], **kw)
        for (key, _, rows), g8 in zip(sends, got, strict=True):
            slots[key] = (g8, 0 if rows is None else rows[0])
        return out

    carried = {}
    for l in reversed(range(depth)):
        sv = saved[l]
        k_in, k_out = mixer(l)
        fcw = full['ffn_conv_w'][l]
        dact = mm_send("ffn_down_dx", gcur, weight['ffn_w_down', l], carried.get('down_dx', ()), tb=True)
        dw_down = mm_send("ffn_down_dw", sv['actf'], gcur, carried.get('down_dw', ()), **dw_opts)
        dug, duv, dwg, dwv, dbg, dbv = ffn_act_bwd(f"ffn_act_bwd", sv['u0'], fcw, row(ffn_conv_b[l]), dact)
        kf = fcw.shape[0]
        gl['ffn_conv_w'][l] = jnp.concatenate([dwg[:kf], dwv[:kf]], axis=1)
        gl['ffn_conv_b'][l] = jnp.concatenate([dbg, dbv], axis=1)[0]
        down_send = [whole(('ffn_w_down', l), pieces(('ffn_w_down', l), dw_down))]
        if as_pieces['ffn_w_up']:
            w_up, shard = weight['ffn_w_up', l], wts['ffn_w_up'].shape[2]
            dh2 = mm_send("ffn_up_dx", dug, w_up[:N_DEV // 2], down_send, tb=True, b_pieces=True)
            dh2 = matmul("ffn_up_dx", duv, w_up[N_DEV // 2:], tb=True, b_pieces=True, res=dh2)
            dw_up = jnp.concatenate([
                mm_send("ffn_up_dw", sv['h2'], dug, carried.get('up_dw', ()), out_pieces=shard, **dw_opts),
                matmul("ffn_up_dw", sv['h2'], duv, out_pieces=shard, **dw_opts)], axis=0)
        else:
            du0 = jnp.concatenate([dug, duv], axis=1)
            dh2 = mm_send("ffn_up_dx", du0, weight['ffn_w_up', l], down_send, tb=True)
            dw_up = mm_send("ffn_up_dw", sv['h2'], du0, carried.get('up_dw', ()), **dw_opts)
        up_a, up_b = halves(('ffn_w_up', l), pieces(('ffn_w_up', l), dw_up))
        in_dx_sends = [up_a]
        gmid, dnw = rms_bwd(f"rms_ffn_bwd", sv['x1'], row(norm_ffn_w[l]), dh2, gcur)
        gl['norm_ffn_w'][l] = dnw[0]
        if l % 2 == 0:
            e = l // 2
            proj, lru_p = sv['proj'], sv['lru_p']
            dy = matmul("ev_out_dx", gmid, weight[k_out], tb=True)
            dw_out = matmul("ev_out_dw", sv['y'], gmid, **dw_opts)
            dhs, dga, dob, dgb, dhn = even_post_bwd(f"even_post_bwd", proj, sv['hseq'], sv['ob'], row(hg_norm_w[e]), dy)
            gl['hg_norm_w'][e] = dhn[0]
            dq, df, di, dlb = hg_bwd(f"hg_bwd", proj, row(lbs[e]), sv['states'], dob, lru_w)
            d_lbs[e] = dlb
            lamb = lin_scan(f"lru_scan_bwd", sv['a'], dhs, True)
            dxc, dcw, dcb, dwr, dbr, dwi, dbi, dlam = lru_pre_bwd(f"lru_pre_bwd", proj, *lru_p, sv['xc'], sv['hseq'], lamb)
            nb = lru_w_r.shape[1]
            gl['lru_conv_w'][e], gl['lru_conv_b'][e] = dcw[:lru_p[0].shape[0]], dcb[0]
            gl['lru_w_r'][e], gl['lru_b_r'][e] = _diag_blocks(dwr, nb), dbr[0]
            gl['lru_w_i'][e], gl['lru_b_i'][e] = _diag_blocks(dwi, nb), dbi[0]
            gl['lru_lambda'][e] = dlam[0]
            dxa = conv_t_call(f"lru_convt", dxc, lru_p[0], 0, lru_w)
            dproj = jnp.concatenate([dxa, dga, dq, df, di, dgb], axis=1)
            in_dw_sends = [up_b] + ([whole(k_out, pieces(k_out, dw_out))] if l == 0 else [])
            dh = mm_send("ev_in_dx", dproj, weight[k_in], in_dx_sends, tb=True, b_pieces=as_pieces[k_in[0]])
            dw_in = mm_send("ev_in_dw", sv['h'], dproj, in_dw_sends,
                            out_pieces=wts[k_in[0]].shape[2] if as_pieces[k_in[0]] else 0, **dw_opts)
        else:
            o = l // 2
            zx, scw = sv['zx'], full['ssd_conv_w'][o]
            dyn = matmul("ssd_out_dx", gmid, weight[k_out], tb=True)
            dw_out = matmul("ssd_out_dw", sv['yn'], gmid, **dw_opts)
            dys, dz, dnw = ssd_post_bwd(f"ssd_post_bwd", sv['ys'], zx, row(full['ssd_norm_w'][o]), dyn, groups)
            gl['ssd_norm_w'][o] = dnw[0]
            dxs, dbm, dcm, ddt, dcum, dcr, ddexp = ssd_bwd(f"ssd_bwd", sv['act'], *sv['prep'], sv['ssd_p'][2],
                                                          sv['states'], dys, inner, groups)
            ddtr, ddtb, dalog = ssd_prep_bwd("ssd_prep_bwd", sv['dtr'], sv['ssd_p'][0], sv['ssd_p'][1], ddt, dcum, dcr, inner)
            gl['ssd_dt_bias'][o], gl['ssd_a_log'][o] = ddtb[0, :heads], dalog[0, :heads]
            gl['ssd_d'][o] = jnp.sum(ddexp.reshape(heads, HEADDIM), axis=1)
            dact = jnp.concatenate([dxs, dbm, dcm], axis=1)
            dpre, dcw, dcb = ssd_conv_bwd(f"ssd_conv_bwd", zx, scw, row(full['ssd_conv_b'][o]), dact, inner)
            gl['ssd_conv_w'][o], gl['ssd_conv_b'][o] = dcw[:scw.shape[0]], dcb[0]
            dxbc = conv_t_call(f"ssd_convt", dpre, scw, 0, _ssd_conv_width(inner, cdim))
            dzx = jnp.concatenate([dz, dxbc], axis=1)
            in_dw_sends = [up_b] + ([whole(k_out, pieces(k_out, dw_out))] if l == 0 else [])
            dh = mm_send("ssd_in_dx", dzx, sv['wzx'], in_dx_sends, tb=True)
            dh = matmul("ssd_dt_dx", ddtr, sv['wdt'], tb=True, res=dh)
            dwzx = mm_send("ssd_in_dw", sv['h'], dzx, in_dw_sends, **dw_opts)
            dwdt = matmul("ssd_dt_dw", sv['h'], ddtr, **dw_opts)
            dw_in = jnp.concatenate([dwzx, dwdt[:, :heads]], axis=1)
        if l > 0:
            in_a, in_b = halves(k_in, pieces(k_in, dw_in))
            carried = {'down_dx': [in_a], 'down_dw': [in_b], 'up_dw': [whole(k_out, pieces(k_out, dw_out))]}
        gcur, dnw = rms_bwd(f"rms_mix_bwd", sv['x0'], row(norm_mix_w[l]), dh, gmid)
        gl['norm_mix_w'][l] = dnw[0]

    def lbs_bwd(hb, dl):
        _, vjp = jax.vjp(f_lbs, hb)
        return [vjp(dl)[0]]
    d_hlb = whole_call("lbs_bwd", lbs_bwd, [hg_lower_bounds, jnp.concatenate(d_lbs, axis=0)], [hg_lower_bounds.shape])[0]

    part = {n: jnp.stack(v) for n, v in gl.items() if n != 'hg_lower_bounds'}
    part['hg_lower_bounds'] = d_hlb
    part['norm_final_w'] = d_nfw[0]

    slots[(*k_in, 0)] = (exchange("rs_last", pieces(k_in, dw_in)), 0)
    out = {}
    for n in BIG:
        res = None
        for idx in range(wts[n].shape[0]):
            for half_no in range(2):
                if (n, idx, half_no) in slots:
                    got, row0 = slots[n, idx, half_no]
                    res = adamw(f"adamw_{n}", got, wts[n], mom1[n], mom2[n], idx, row0, res)
        out[n] = res

    small_buf, _ = _pack([jnp.concatenate([part[n].astype(F32).reshape(-1) for n in SMALL])])
    sm = all_gather("ag_small_grads", small_buf).reshape(N_DEV, -1)
    own, at = [], 0
    for n in SMALL:
        size = part[n].size
        g8 = sm[:, at:at + size].reshape((N_DEV,) + part[n].shape)
        at += size
        if n in SMALL_SHARDED:
            g8 = lax.dynamic_slice_in_dim(g8, me * wts[n].shape[-1], wts[n].shape[-1], axis=g8.ndim - 1)
        own.append(g8.reshape(N_DEV, -1))
    sslots, _ = _pack([jnp.concatenate(own, axis=1)])
    cat = lambda dct: _pad_to(jnp.concatenate([dct[n].reshape(-1) for n in SMALL]), SEG).reshape(1, -1, 128)
    res = adamw("adamw_small", sslots, cat(wts), cat(mom1), cat(mom2))
    at = 0
    for n in SMALL:
        out[n] = [r.reshape(-1)[at:at + wts[n].size].reshape(wts[n].shape) for r in res]
        at += wts[n].size

    grad_x = gcur.reshape(x.shape)
    return (loss, grad_x, *[out[n][0] for n in WEIGHTS], *[out[n][1] for n in WEIGHTS],
            *[out[n][2] for n in WEIGHTS], *[out[n][3] for n in WEIGHTS])
```
